```python
import math
import jax, jax.numpy as jnp
from jax import lax
import numpy as np

D_MODEL = 4096
BATCH = 8
SEQ = 2048
DEPTH = 2

HEAD_DIM = 128
DIL_WINDOWS = (128, 512, 2048)
DIL_RATES = (1, 4, 16)
N_DIL_GROUPS = 3
DIL_HEADS = D_MODEL // HEAD_DIM // 2
DIL_WIDTH = DIL_HEADS * HEAD_DIM
A_IN_COLS = N_DIL_GROUPS * 3 * DIL_WIDTH
FOX_HEADS = D_MODEL // HEAD_DIM
FOX_WIDTH = FOX_HEADS * HEAD_DIM
B_IN_COLS = 3 * FOX_WIDTH + FOX_HEADS
Q_BLOCK = 128
PEER_HEADS = 8
PEER_NKEYS = 128
PEER_EXPERTS = PEER_NKEYS * PEER_NKEYS
PEER_QDIM = 256
PEER_TOPK = 16
PEER_CHUNK = 128
DN_ALPHA = (2 * DEPTH) ** 0.25
DN_BETA = (8 * DEPTH) ** -0.25
LN_EPS = 1e-5
N_A_LAYERS = (DEPTH + 1) // 2
N_B_LAYERS = DEPTH // 2
NEG = -1e30

kernel_name = "hybrid_dilated_fox_peer_deepnorm"


def alibi_slopes(n_heads):
    return jnp.exp2(-8.0 * (jnp.arange(n_heads, dtype=jnp.float32) + 1.0) / n_heads)


def layer_norm(x, g, b):
    xf = x.astype(jnp.float32)
    mu = jnp.mean(xf, axis=-1, keepdims=True)
    var = jnp.mean(jnp.square(xf - mu), axis=-1, keepdims=True)
    return ((xf - mu) * lax.rsqrt(var + LN_EPS) * g.astype(jnp.float32) + b.astype(jnp.float32)).astype(x.dtype)


def dilated_window_attention(q, k, v, dil, window, slopes):
    B, S, H, Dh = q.shape
    n = window // dil
    L = S // dil
    nb = -(-L // n)
    Lp = nb * n

    def strided(t):
        t = t.reshape(B, L, dil, H, Dh).transpose(0, 2, 3, 1, 4)
        return jnp.pad(t, ((0, 0), (0, 0), (0, 0), (0, Lp - L), (0, 0)))

    def banded(t):
        t = jnp.pad(t, ((0, 0), (0, 0), (0, 0), (n, 0), (0, 0))).reshape(B, dil, H, nb + 1, n, Dh)
        return jnp.concatenate([t[:, :, :, :-1], t[:, :, :, 1:]], axis=4)

    qb = strided(q).reshape(B, dil, H, nb, n, Dh)
    kb = banded(strided(k))
    vb = banded(strided(v))
    logits = jnp.einsum('brhnqc,brhnkc->brhnqk', qb, kb).astype(jnp.float32) / math.sqrt(Dh)
    i = jnp.arange(n)[:, None]
    j = jnp.arange(2 * n)[None, :]
    dist = n + i - j
    blk = jnp.arange(nb)[:, None, None]
    valid = (dist >= 0) & (dist <= n) & (blk * n + j - n >= 0)
    bias = -slopes[:, None, None, None] * (dil * dist).astype(jnp.float32)[None, None]
    logits = jnp.where(valid, logits + bias, NEG)
    mx = jnp.max(logits, axis=-1, keepdims=True)
    p = jnp.exp(logits - mx)
    den = jnp.sum(p, axis=-1, keepdims=True)
    o = jnp.einsum('brhnqk,brhnkc->brhnqc', p, vb.astype(jnp.float32)) / den
    lse = (mx + jnp.log(den))[..., 0]
    o = o.reshape(B, dil, H, Lp, Dh)[:, :, :, :L].transpose(0, 3, 1, 2, 4).reshape(B, S, H, Dh)
    lse = lse.reshape(B, dil, H, Lp)[:, :, :, :L].transpose(0, 3, 1, 2).reshape(B, S, H)
    return o, lse


def dilated_mixer(x, w_in, w_out):
    B, S, _ = x.shape
    proj = (x @ w_in).reshape(B, S, N_DIL_GROUPS, 3, DIL_HEADS, HEAD_DIM)
    slopes = alibi_slopes(DIL_HEADS)
    outs, lses = [], []
    for g in range(N_DIL_GROUPS):
        o, lse = dilated_window_attention(proj[:, :, g, 0], proj[:, :, g, 1], proj[:, :, g, 2],
                                          DIL_RATES[g], DIL_WINDOWS[g], slopes)
        outs.append(o)
        lses.append(lse)
    wgt = jax.nn.softmax(jnp.stack(lses), axis=0)
    o = jnp.einsum('gbsh,gbshc->bshc', wgt, jnp.stack(outs))
    return o.reshape(B, S, DIL_WIDTH).astype(x.dtype) @ w_out


def forgetting_mixer(x, w_in, f_bias, w_out):
    B, S, _ = x.shape
    proj = x @ w_in
    qkv = proj[..., :3 * FOX_WIDTH].reshape(B, S, 3, FOX_HEADS, HEAD_DIM)
    q = qkv[:, :, 0].transpose(0, 2, 1, 3)
    k = qkv[:, :, 1].transpose(0, 2, 1, 3)
    v = qkv[:, :, 2].transpose(0, 2, 1, 3).astype(jnp.float32)
    log_f = jax.nn.log_sigmoid((proj[..., 3 * FOX_WIDTH:] + f_bias).astype(jnp.float32))
    c = jnp.cumsum(log_f, axis=1).transpose(0, 2, 1)
    nb = S // Q_BLOCK
    qb = q.reshape(B, FOX_HEADS, nb, Q_BLOCK, HEAD_DIM).transpose(2, 0, 1, 3, 4)
    cb = c.reshape(B, FOX_HEADS, nb, Q_BLOCK).transpose(2, 0, 1, 3)
    kpos = jnp.arange(S)
    scale = 1.0 / math.sqrt(HEAD_DIM)

    def block(args):
        qi, ci, start = args
        logits = jnp.einsum('bhqc,bhkc->bhqk', qi, k).astype(jnp.float32) * scale
        logits = logits + ci[..., None] - c[:, :, None, :]
        qpos = start + jnp.arange(Q_BLOCK)
        logits = jnp.where(kpos[None, :] <= qpos[:, None], logits, NEG)
        p = jax.nn.softmax(logits, axis=-1)
        return jnp.einsum('bhqk,bhkc->bhqc', p, v)

    o = lax.map(block, (qb, cb, jnp.arange(nb) * Q_BLOCK))
    o = o.transpose(1, 0, 3, 2, 4).reshape(B, S, FOX_WIDTH)
    return o.astype(x.dtype) @ w_out


def peer_ffn(x, w_q, sub_keys, u, v):
    B, S, D = x.shape
    T = B * S
    xt = x.reshape(T, D)
    q = (xt @ w_q).reshape(T, PEER_HEADS, 2, PEER_QDIM // 2)
    s = jnp.einsum('thpc,pnc->thpn', q, sub_keys).astype(jnp.float32)
    sv, si = lax.top_k(s, PEER_TOPK)
    cand = (sv[:, :, 0, :, None] + sv[:, :, 1, None, :]).reshape(T, PEER_HEADS, PEER_TOPK * PEER_TOPK)
    cidx = (si[:, :, 0, :, None] * PEER_NKEYS + si[:, :, 1, None, :]).reshape(T, PEER_HEADS, PEER_TOPK * PEER_TOPK)
    top_s, pos = lax.top_k(cand, PEER_TOPK)
    eidx = jnp.take_along_axis(cidx, pos, axis=-1)
    gate = jax.nn.softmax(top_s, axis=-1)
    nc = T // PEER_CHUNK

    def chunk(args):
        xc, ec, gc = args
        act = jax.nn.gelu(jnp.einsum('cd,chkd->chk', xc, u[ec]).astype(jnp.float32), approximate=False)
        return jnp.einsum('chk,chkd->cd', (gc * act).astype(x.dtype), v[ec])

    out = lax.map(chunk, (xt.reshape(nc, PEER_CHUNK, D),
                          eidx.reshape(nc, PEER_CHUNK, PEER_HEADS, PEER_TOPK),
                          gate.reshape(nc, PEER_CHUNK, PEER_HEADS, PEER_TOPK)))
    return out.reshape(B, S, D)


def setup_inputs(seed: int = 0) -> dict:
    key = jax.random.key(seed)
    ks = jax.random.split(key, 20)
    D = D_MODEL
    nrm = lambda k, shape, scale: jax.random.normal(k, shape, jnp.float32) * scale
    x = nrm(ks[0], (BATCH, SEQ, D), 1.0)
    a_qk = nrm(ks[1], (N_A_LAYERS, D, N_DIL_GROUPS, 2, DIL_WIDTH), D ** -0.5)
    a_v = nrm(ks[2], (N_A_LAYERS, D, N_DIL_GROUPS, 1, DIL_WIDTH), D ** -0.5 * DN_BETA)
    a_w_in = jnp.concatenate([a_qk, a_v], axis=3).reshape(N_A_LAYERS, D, A_IN_COLS)
    a_w_out = nrm(ks[3], (N_A_LAYERS, DIL_WIDTH, D), DIL_WIDTH ** -0.5 * DN_BETA)
    b_qk = nrm(ks[4], (N_B_LAYERS, D, 2 * FOX_WIDTH), D ** -0.5)
    b_v = nrm(ks[5], (N_B_LAYERS, D, FOX_WIDTH), D ** -0.5 * DN_BETA)
    b_fw = nrm(ks[6], (N_B_LAYERS, D, FOX_HEADS), D ** -0.5)
    b_w_in = jnp.concatenate([b_qk, b_v, b_fw], axis=-1)
    b_f_bias = 1.0 + 2.0 * jax.random.uniform(ks[7], (N_B_LAYERS, FOX_HEADS), jnp.float32)
    b_w_out = nrm(ks[8], (N_B_LAYERS, FOX_WIDTH, D), FOX_WIDTH ** -0.5 * DN_BETA)
    peer_w_q = nrm(ks[9], (DEPTH, D, PEER_HEADS * PEER_QDIM), D ** -0.5)
    peer_sub_keys = nrm(ks[10], (DEPTH, 2, PEER_NKEYS, PEER_QDIM // 2), (PEER_QDIM // 2) ** -0.5)
    peer_u = nrm(ks[11], (DEPTH, PEER_EXPERTS, D), D ** -0.5)
    peer_v = nrm(ks[12], (DEPTH, PEER_EXPERTS, D), DN_BETA * PEER_HEADS ** -0.5)
    ln_mix_g = 1.0 + nrm(ks[13], (DEPTH, D), 0.02)
    ln_mix_b = nrm(ks[14], (DEPTH, D), 0.02)
    ln_ffn_g = 1.0 + nrm(ks[15], (DEPTH, D), 0.02)
    ln_ffn_b = nrm(ks[16], (DEPTH, D), 0.02)
    return {"x": x, "a_w_in": a_w_in, "a_w_out": a_w_out, "b_w_in": b_w_in, "b_f_bias": b_f_bias,
            "b_w_out": b_w_out, "peer_w_q": peer_w_q, "peer_sub_keys": peer_sub_keys,
            "peer_u": peer_u, "peer_v": peer_v, "ln_mix_g": ln_mix_g, "ln_mix_b": ln_mix_b,
            "ln_ffn_g": ln_ffn_g, "ln_ffn_b": ln_ffn_b}


def reference(x, a_w_in, a_w_out, b_w_in, b_f_bias, b_w_out, peer_w_q, peer_sub_keys,
              peer_u, peer_v, ln_mix_g, ln_mix_b, ln_ffn_g, ln_ffn_b):
    h = x
    for i in range(DEPTH):
        j = i // 2
        if i % 2 == 0:
            mix = dilated_mixer(h, a_w_in[j], a_w_out[j])
        else:
            mix = forgetting_mixer(h, b_w_in[j], b_f_bias[j], b_w_out[j])
        h = layer_norm(DN_ALPHA * h + mix, ln_mix_g[i], ln_mix_b[i])
        ffn = peer_ffn(h, peer_w_q[i], peer_sub_keys[i], peer_u[i], peer_v[i])
        h = layer_norm(DN_ALPHA * h + ffn, ln_ffn_g[i], ln_ffn_b[i])
    return h
```

```python
import functools
import math

import jax
import jax.numpy as jnp
from jax import lax
from jax.experimental import pallas as pl
from jax.experimental.pallas import tpu as pltpu

_F32 = jnp.float32
_BF16 = jnp.bfloat16

_HEAD_DIM = 128
_DIL_WINDOWS = (128, 512, 2048)
_DIL_RATES = (1, 4, 16)
_PEER_TOPK = 16
_LN_EPS = 1e-5
_NEG = -1e30
_LANES = 128
_VMEM_LIMIT = 56 * 1024 * 1024


def _dot_nt(a, b):
    return lax.dot_general(a, b, (((1,), (1,)), ((), ())), preferred_element_type=_F32)


def _params(sem):
    return pltpu.CompilerParams(dimension_semantics=sem, vmem_limit_bytes=_VMEM_LIMIT)


def _mm_kernel(a_ref, b_ref, o_ref):
    o_ref[...] = jnp.dot(a_ref[...], b_ref[...], preferred_element_type=_F32).astype(o_ref.dtype)


def _matmul(a, b, out_dtype, tm=1024, tn=512):
    m, k = a.shape
    n = b.shape[1]
    tm = min(tm, m)
    tn = min(tn, n)
    while n % tn:
        tn -= _LANES
    assert m % tm == 0 and tn > 0
    return pl.pallas_call(
        _mm_kernel,
        grid=(m // tm, n // tn),
        in_specs=[pl.BlockSpec((tm, k), lambda i, j: (i, 0)),
                  pl.BlockSpec((k, tn), lambda i, j: (0, j))],
        out_specs=pl.BlockSpec((tm, tn), lambda i, j: (i, j)),
        out_shape=jax.ShapeDtypeStruct((m, n), out_dtype),
        compiler_params=_params(("parallel", "parallel")),
        name="matmul",
    )(a, b)


def _ln_body(y, g_ref, b_ref, o_ref, ob_ref):
    mu = jnp.mean(y, axis=-1, keepdims=True)
    d = y - mu
    var = jnp.mean(d * d, axis=-1, keepdims=True)
    out = d * lax.rsqrt(var + _LN_EPS) * g_ref[...] + b_ref[...]
    o_ref[...] = out
    ob_ref[...] = out.astype(_BF16)


def _ln_kernel(h_ref, m_ref, g_ref, b_ref, o_ref, ob_ref, *, alpha):
    _ln_body(alpha * h_ref[...] + m_ref[...], g_ref, b_ref, o_ref, ob_ref)


def _ln_t_kernel(h_ref, mt_ref, g_ref, b_ref, o_ref, ob_ref, *, alpha):
    _ln_body(alpha * h_ref[...] + mt_ref[...].T, g_ref, b_ref, o_ref, ob_ref)


def _ln_residual(h, mix, g, b, alpha, transposed_mix=False, tr=256):
    t, d = h.shape
    tr = min(tr, t)
    assert t % tr == 0
    if transposed_mix:
        body = _ln_t_kernel
        mix_spec = pl.BlockSpec((d, tr), lambda i: (0, i))
    else:
        body = _ln_kernel
        mix_spec = pl.BlockSpec((tr, d), lambda i: (i, 0))
    row = pl.BlockSpec((tr, d), lambda i: (i, 0))
    vec = pl.BlockSpec((1, d), lambda i: (0, 0))
    return pl.pallas_call(
        functools.partial(body, alpha=alpha),
        grid=(t // tr,),
        in_specs=[row, mix_spec, vec, vec],
        out_specs=[row, row],
        out_shape=[jax.ShapeDtypeStruct((t, d), _F32), jax.ShapeDtypeStruct((t, d), _BF16)],
        compiler_params=_params(("parallel",)),
        name="ln_residual",
    )(h, mix, g.reshape(1, d), b.reshape(1, d))


def _dil_kernel(q_ref, kc_ref, kp_ref, vc_ref, vp_ref, o_ref, lse_ref, *, dil, n_heads):
    n = _LANES
    has_prev = pl.program_id(2) > 0
    row = lax.broadcasted_iota(jnp.int32, (n, n), 0)
    col = lax.broadcasted_iota(jnp.int32, (n, n), 1)
    diff = row - col
    valid_c = diff >= 0
    valid_p = (diff <= 0) & has_prev
    dist_c = diff.astype(_F32)
    dist_p = dist_c + float(n)
    scale = 1.0 / math.sqrt(_HEAD_DIM)
    lse_tile = jnp.zeros((n, n), _F32)
    for h in range(n_heads):
        slope = float(dil) * 2.0 ** (-8.0 * (h + 1) / n_heads)
        sl = slice(h * _HEAD_DIM, (h + 1) * _HEAD_DIM)
        q = q_ref[0, :, sl]
        lc = jnp.where(valid_c, _dot_nt(q, kc_ref[0, :, sl]) * scale - slope * dist_c, _NEG)
        lp = jnp.where(valid_p, _dot_nt(q, kp_ref[0, :, sl]) * scale - slope * dist_p, _NEG)
        mx = jnp.maximum(jnp.max(lc, axis=1, keepdims=True), jnp.max(lp, axis=1, keepdims=True))
        pc = jnp.exp(lc - mx)
        pp = jnp.exp(lp - mx)
        den = jnp.sum(pc, axis=1, keepdims=True) + jnp.sum(pp, axis=1, keepdims=True)
        acc = (jnp.dot(pc.astype(_BF16), vc_ref[0, :, sl], preferred_element_type=_F32)
               + jnp.dot(pp.astype(_BF16), vp_ref[0, :, sl], preferred_element_type=_F32))
        o_ref[0, :, sl] = acc / den
        lse_tile = jnp.where(col == h, mx + jnp.log(den), lse_tile)
    lse_ref[0] = lse_tile


def _dilated_group(proj, g, n_groups, n_heads):
    b, s, c = proj.shape
    w = n_heads * _HEAD_DIM
    dil = _DIL_RATES[g]
    n = _DIL_WINDOWS[g] // dil
    assert n == _LANES and n_heads <= _LANES
    l = s // dil
    nb = l // n
    assert l % n == 0
    cw = c // w
    view = proj.reshape(b, l, dil * c)

    def spec(which, prev):
        def index(bi, r, i):
            blk = jnp.maximum(i - 1, 0) if prev else i
            return (bi, blk, r * cw + g * 3 + which)
        return pl.BlockSpec((1, n, w), index)

    o, lse = pl.pallas_call(
        functools.partial(_dil_kernel, dil=dil, n_heads=n_heads),
        grid=(b, dil, nb),
        in_specs=[spec(0, False), spec(1, False), spec(1, True), spec(2, False), spec(2, True)],
        out_specs=[pl.BlockSpec((1, n, w), lambda bi, r, i: (bi, i, r)),
                   pl.BlockSpec((1, n, _LANES), lambda bi, r, i: (bi, i, r))],
        out_shape=[jax.ShapeDtypeStruct((b, l, dil * w), _F32),
                   jax.ShapeDtypeStruct((b, l, dil * _LANES), _F32)],
        compiler_params=_params(("parallel", "parallel", "parallel")),
        name=f"dilated_attention_g{g}",
    )(view, view, view, view, view)
    return o.reshape(b * s, w), lse.reshape(b * s, _LANES)


def _merge_kernel(*refs, n_groups, n_heads):
    o_refs = refs[:n_groups]
    l_refs = refs[n_groups:2 * n_groups]
    out_ref = refs[2 * n_groups]
    lses = [r[...] for r in l_refs]
    for h in range(n_heads):
        sl = slice(h * _HEAD_DIM, (h + 1) * _HEAD_DIM)
        lh = [x[:, h:h + 1] for x in lses]
        mx = functools.reduce(jnp.maximum, lh)
        ws = [jnp.exp(x - mx) for x in lh]
        num = sum(wg * o[:, sl] for wg, o in zip(ws, o_refs))
        out_ref[:, sl] = (num / sum(ws)).astype(out_ref.dtype)


def _merge_groups(outs, lses, n_heads, tr=256):
    t, w = outs[0].shape
    tr = min(tr, t)
    ng = len(outs)
    return pl.pallas_call(
        functools.partial(_merge_kernel, n_groups=ng, n_heads=n_heads),
        grid=(t // tr,),
        in_specs=[pl.BlockSpec((tr, w), lambda i: (i, 0))] * ng
        + [pl.BlockSpec((tr, _LANES), lambda i: (i, 0))] * ng,
        out_specs=pl.BlockSpec((tr, w), lambda i: (i, 0)),
        out_shape=jax.ShapeDtypeStruct((t, w), _BF16),
        compiler_params=_params(("parallel",)),
        name="dilated_merge",
    )(*outs, *lses)


def _dilated_mixer(hb, w_in, w_out, b, s):
    d = hb.shape[1]
    n_groups = len(_DIL_RATES)
    n_heads = d // _HEAD_DIM // 2
    proj = _matmul(hb, w_in.astype(_BF16), _BF16).reshape(b, s, -1)
    outs, lses = zip(*[_dilated_group(proj, g, n_groups, n_heads) for g in range(n_groups)])
    merged = _merge_groups(outs, lses, n_heads)
    return _matmul(merged, w_out.astype(_BF16), _F32)


def _fox_gate_kernel(wf_ref, h_ref, bias_ref, c_ref, carry_ref):
    @pl.when(pl.program_id(1) == 0)
    def _():
        carry_ref[...] = jnp.zeros_like(carry_ref)

    z = _dot_nt(wf_ref[...], h_ref[0]) + bias_ref[...]
    log_f = jnp.minimum(z, 0.0) - jnp.log1p(jnp.exp(-jnp.abs(z)))
    nh, ts = log_f.shape
    lane = lax.broadcasted_iota(jnp.int32, (nh, _LANES), 1)
    carry = carry_ref[...]
    for c in range(ts // _LANES):
        x = log_f[:, c * _LANES:(c + 1) * _LANES]
        shift = 1
        while shift < _LANES:
            x = x + jnp.where(lane >= shift, pltpu.roll(x, shift, 1), 0.0)
            shift *= 2
        x = x + carry
        c_ref[0, :, c * _LANES:(c + 1) * _LANES] = x
        carry = jnp.broadcast_to(x[:, _LANES - 1:_LANES], (nh, _LANES))
    carry_ref[...] = carry


def _fox_gate(hb3, wf_t, bias, ts=512):
    b, s, d = hb3.shape
    nh = wf_t.shape[0]
    ts = min(ts, s)
    return pl.pallas_call(
        _fox_gate_kernel,
        grid=(b, s // ts),
        in_specs=[pl.BlockSpec((nh, d), lambda bi, si: (0, 0)),
                  pl.BlockSpec((1, ts, d), lambda bi, si: (bi, si, 0)),
                  pl.BlockSpec((nh, 1), lambda bi, si: (0, 0))],
        out_specs=pl.BlockSpec((1, nh, ts), lambda bi, si: (bi, 0, si)),
        out_shape=jax.ShapeDtypeStruct((b, nh, s), _F32),
        scratch_shapes=[pltpu.VMEM((nh, _LANES), _F32)],
        compiler_params=_params(("parallel", "arbitrary")),
        name="fox_gate_cumsum",
    )(wf_t, hb3, bias.reshape(nh, 1).astype(_F32))


def _fox_kernel(q_ref, k_ref, v_ref, c_ref, o_ref, *, tq):
    qi = pl.program_id(2)
    q_start = pl.multiple_of(qi * tq, tq)
    q = q_ref[0]
    row = lax.broadcasted_iota(jnp.int32, (tq, tq), 0)
    col = lax.broadcasted_iota(jnp.int32, (tq, tq), 1)
    c_q = c_ref[0, 0, :, pl.ds(q_start, tq)]
    c_col = jnp.sum(jnp.where(row == col, jnp.broadcast_to(c_q, (tq, tq)), 0.0),
                    axis=1, keepdims=True)
    scale = 1.0 / math.sqrt(_HEAD_DIM)

    def body(kb, carry):
        m, l, acc = carry
        k_start = pl.multiple_of(kb * tq, tq)
        k = k_ref[0, pl.ds(k_start, tq), :]
        v = v_ref[0, pl.ds(k_start, tq), :]
        s = _dot_nt(q, k) * scale + c_col - c_ref[0, 0, :, pl.ds(k_start, tq)]
        s = jnp.where(k_start + col <= q_start + row, s, _NEG)
        m_new = jnp.maximum(m, jnp.max(s, axis=1, keepdims=True))
        alpha = jnp.exp(m - m_new)
        p = jnp.exp(s - m_new)
        l = alpha * l + jnp.sum(p, axis=1, keepdims=True)
        acc = alpha * acc + jnp.dot(p.astype(_BF16), v, preferred_element_type=_F32)
        return m_new, l, acc

    init = (jnp.full((tq, 1), _NEG, _F32), jnp.zeros((tq, 1), _F32),
            jnp.zeros((tq, _HEAD_DIM), _F32))
    _, l, acc = lax.fori_loop(0, qi + 1, body, init)
    o_ref[0] = (acc / l).astype(o_ref.dtype)


def _fox_attention(qkv, c, n_heads, tq=256):
    b, s, _ = qkv.shape
    tq = min(tq, s)
    return pl.pallas_call(
        functools.partial(_fox_kernel, tq=tq),
        grid=(b, n_heads, s // tq),
        in_specs=[pl.BlockSpec((1, tq, _HEAD_DIM), lambda bi, h, qi: (bi, qi, h)),
                  pl.BlockSpec((1, s, _HEAD_DIM), lambda bi, h, qi: (bi, 0, n_heads + h)),
                  pl.BlockSpec((1, s, _HEAD_DIM), lambda bi, h, qi: (bi, 0, 2 * n_heads + h)),
                  pl.BlockSpec((1, 1, 1, s), lambda bi, h, qi: (bi, h, 0, 0))],
        out_specs=pl.BlockSpec((1, tq, _HEAD_DIM), lambda bi, h, qi: (bi, qi, h)),
        out_shape=jax.ShapeDtypeStruct((b, s, n_heads * _HEAD_DIM), _BF16),
        compiler_params=_params(("parallel", "parallel", "parallel")),
        name="fox_attention",
    )(qkv, qkv, qkv, c.reshape(b, n_heads, 1, s))


def _forgetting_mixer(hb, w_in, f_bias, w_out, b, s):
    d = hb.shape[1]
    n_heads = d // _HEAD_DIM
    width = n_heads * _HEAD_DIM
    qkv = _matmul(hb, w_in[:, :3 * width].astype(_BF16), _BF16).reshape(b, s, 3 * width)
    wf_t = w_in[:, 3 * width:].T.astype(_BF16)
    c = _fox_gate(hb.reshape(b, s, d), wf_t, f_bias)
    o = _fox_attention(qkv, c, n_heads)
    return _matmul(o.reshape(b * s, width), w_out.astype(_BF16), _F32)


def _top_values(scores, count, rows_out):
    n, t = scores.shape
    row = lax.broadcasted_iota(jnp.int32, (n, t), 0)
    row_out = lax.broadcasted_iota(jnp.int32, (rows_out, t), 0)
    packed = jnp.full((rows_out, t), -jnp.inf, _F32)
    vals = []
    for k in range(count):
        m = jnp.max(scores, axis=0, keepdims=True)
        vals.append(m)
        packed = jnp.where(row_out == k, m, packed)
        first = jnp.min(jnp.where(scores == m, row, n), axis=0, keepdims=True)
        scores = jnp.where(row == first, -jnp.inf, scores)
    return vals, packed


def _peer_gate_kernel(q_ref, keys_ref, thr_ref, e0_ref, s1_ref, e1z_ref, *, n_heads, topk):
    half = keys_ref.shape[2]
    tt = q_ref.shape[0]
    k0 = keys_ref[0]
    k1 = keys_ref[1]
    nkeep = topk + 1
    rows = 8 * (-(-nkeep // 8))
    for h in range(n_heads):
        s0 = _dot_nt(k0, q_ref[:, (2 * h) * half:(2 * h + 1) * half])
        s1 = _dot_nt(k1, q_ref[:, (2 * h + 1) * half:(2 * h + 2) * half])
        a_vals, _ = _top_values(s0, nkeep, rows)
        b_vals, b_packed = _top_values(s1, nkeep, rows)
        cands = []
        for i in range(nkeep):
            nj = nkeep // (i + 1)
            rj = 8 * (-(-nj // 8))
            rowj = lax.broadcasted_iota(jnp.int32, (rj, tt), 0)
            cands.append((jnp.where(rowj < nj, a_vals[i] + b_packed[:rj], -jnp.inf),
                          rowj + i * rows))
        top = []
        for k in range(nkeep):
            m = functools.reduce(jnp.maximum, [jnp.max(cv, axis=0, keepdims=True) for cv, _ in cands])
            top.append(m)
            big = nkeep * rows
            first = functools.reduce(jnp.minimum, [
                jnp.min(jnp.where(cv == m, ci, big), axis=0, keepdims=True) for cv, ci in cands])
            cands = [(jnp.where(ci == first, -jnp.inf, cv), ci) for cv, ci in cands]
        z = sum(jnp.exp(tv - top[0]) for tv in top[:topk])
        thr = 0.5 * (top[topk - 1] + top[topk])
        thr_ref[h] = thr - s0
        e0_ref[h] = jnp.exp(s0 - a_vals[0])
        s1_ref[h] = s1
        e1z_ref[h] = jnp.exp(s1 - b_vals[0]) / z


def _peer_gate(q, sub_keys, n_heads, tt=128):
    t = q.shape[0]
    nk, half = sub_keys.shape[1:]
    tt = min(tt, t)
    out_spec = pl.BlockSpec((n_heads, nk, tt), lambda i: (0, 0, i))
    out_shape = jax.ShapeDtypeStruct((n_heads, nk, t), _F32)
    return pl.pallas_call(
        functools.partial(_peer_gate_kernel, n_heads=n_heads, topk=_PEER_TOPK),
        grid=(t // tt,),
        in_specs=[pl.BlockSpec((tt, q.shape[1]), lambda i: (i, 0)),
                  pl.BlockSpec((2, nk, half), lambda i: (0, 0, 0))],
        out_specs=[out_spec] * 4,
        out_shape=[out_shape] * 4,
        compiler_params=_params(("parallel",)),
        name="peer_gate",
    )(q, sub_keys)


def _gelu(x):
    return 0.5 * x * (1.0 + lax.erf(x * (1.0 / math.sqrt(2.0))))


def _peer_kernel(hb_ref, u_ref, vt_ref, thr_ref, e0_ref, s1_ref, e1z_ref, o_ref, a_ref, w_ref,
                 *, ni, n_heads):
    e = pl.program_id(1)

    @pl.when(e == 0)
    def _():
        o_ref[...] = jnp.zeros_like(o_ref)

    a_ref[...] = _dot_nt(u_ref[...], hb_ref[...])
    nk = s1_ref.shape[1]
    tt = hb_ref.shape[0]
    for ii in range(ni):
        rs = slice(ii * nk, (ii + 1) * nk)
        for tc in range(tt // _LANES):
            cs = slice(tc * _LANES, (tc + 1) * _LANES)
            gate = jnp.zeros((nk, _LANES), _F32)
            for h in range(n_heads):
                thr = thr_ref[h, ii:ii + 1, cs]
                e0 = e0_ref[h, ii:ii + 1, cs]
                gate = gate + jnp.where(s1_ref[h, :, cs] >= thr, e1z_ref[h, :, cs] * e0, 0.0)
            w_ref[rs, cs] = (_gelu(a_ref[rs, cs]) * gate).astype(_BF16)
    o_ref[...] += jnp.dot(vt_ref[...], w_ref[...], preferred_element_type=_F32)


def _peer_dense(hb, u, vt, thr, e0, s1, e1z, n_heads, tt=512, ni=4):
    t, d = hb.shape
    nk = s1.shape[1]
    tt = min(tt, t)
    et = ni * nk
    n_tiles = nk // ni
    row_view = lambda x: x.reshape(n_heads, n_tiles, ni, t)
    row_spec = pl.BlockSpec((n_heads, None, ni, tt), lambda ti, e: (0, e, 0, ti))
    col_spec = pl.BlockSpec((n_heads, nk, tt), lambda ti, e: (0, 0, ti))
    return pl.pallas_call(
        functools.partial(_peer_kernel, ni=ni, n_heads=n_heads),
        grid=(t // tt, n_tiles),
        in_specs=[pl.BlockSpec((tt, d), lambda ti, e: (ti, 0)),
                  pl.BlockSpec((et, d), lambda ti, e: (e, 0)),
                  pl.BlockSpec((d, et), lambda ti, e: (0, e)),
                  row_spec, row_spec, col_spec, col_spec],
        out_specs=pl.BlockSpec((d, tt), lambda ti, e: (0, ti)),
        out_shape=jax.ShapeDtypeStruct((d, t), _F32),
        scratch_shapes=[pltpu.VMEM((et, tt), _F32), pltpu.VMEM((et, tt), _BF16)],
        compiler_params=_params(("parallel", "arbitrary")),
        name="peer_dense",
    )(hb, u, vt, row_view(thr), row_view(e0), s1, e1z)


def _peer_ffn(hb, w_q, sub_keys, u, v):
    half = sub_keys.shape[2]
    n_heads = w_q.shape[1] // (2 * half)
    q = _matmul(hb, w_q.astype(_BF16), _BF16)
    thr, e0, s1, e1z = _peer_gate(q, sub_keys.astype(_BF16), n_heads)
    return _peer_dense(hb, u.astype(_BF16), v.T.astype(_BF16), thr, e0, s1, e1z, n_heads)


def kernel(x, a_w_in, a_w_out, b_w_in, b_f_bias, b_w_out, peer_w_q, peer_sub_keys, peer_u, peer_v,
           ln_mix_g, ln_mix_b, ln_ffn_g, ln_ffn_b):
    b, s, d = x.shape
    depth = ln_mix_g.shape[0]
    alpha = (2 * depth) ** 0.25
    h = x.reshape(b * s, d)
    hb = h.astype(_BF16)
    for i in range(depth):
        j = i // 2
        if i % 2 == 0:
            mix = _dilated_mixer(hb, a_w_in[j], a_w_out[j], b, s)
        else:
            mix = _forgetting_mixer(hb, b_w_in[j], b_f_bias[j], b_w_out[j], b, s)
        h, hb = _ln_residual(h, mix, ln_mix_g[i], ln_mix_b[i], alpha)
        ffn_t = _peer_ffn(hb, peer_w_q[i], peer_sub_keys[i], peer_u[i], peer_v[i])
        h, hb = _ln_residual(h, ffn_t, ln_ffn_g[i], ln_ffn_b[i], alpha, transposed_mix=True)
    return h.reshape(b, s, d)
```

```python
import functools
import math

import jax
import jax.numpy as jnp
from jax import lax
from jax.experimental import pallas as pl
from jax.experimental.pallas import tpu as pltpu

_F32 = jnp.float32
_BF16 = jnp.bfloat16

_HEAD_DIM = 128
_DIL_WINDOWS = (128, 512, 2048)
_DIL_RATES = (1, 4, 16)
_PEER_TOPK = 16
_LN_EPS = 1e-5
_NEG = -1e30
_LANES = 128
_LOG2E = math.log2(math.e)
_VMEM_LIMIT = 56 * 1024 * 1024


def _dot_nt(a, b):
    return lax.dot_general(a, b, (((1,), (1,)), ((), ())), preferred_element_type=_F32)


def _params(sem):
    return pltpu.CompilerParams(dimension_semantics=sem, vmem_limit_bytes=_VMEM_LIMIT)


def _mm_kernel(a_ref, b_ref, o_ref):
    o_ref[...] = jnp.dot(a_ref[...], b_ref[...], preferred_element_type=_F32).astype(o_ref.dtype)


def _matmul(a, b, out_dtype, tm=1024, tn=512):
    m, k = a.shape
    n = b.shape[1]
    tm = min(tm, m)
    tn = min(tn, n)
    while n % tn:
        tn -= _LANES
    assert m % tm == 0 and tn > 0
    return pl.pallas_call(
        _mm_kernel,
        grid=(m // tm, n // tn),
        in_specs=[pl.BlockSpec((tm, k), lambda i, j: (i, 0)),
                  pl.BlockSpec((k, tn), lambda i, j: (0, j))],
        out_specs=pl.BlockSpec((tm, tn), lambda i, j: (i, j)),
        out_shape=jax.ShapeDtypeStruct((m, n), out_dtype),
        compiler_params=_params(("parallel", "parallel")),
        name="matmul",
    )(a, b)


def _ln_body(y, g_ref, b_ref, o_ref, ob_ref):
    mu = jnp.mean(y, axis=-1, keepdims=True)
    d = y - mu
    var = jnp.mean(d * d, axis=-1, keepdims=True)
    out = d * lax.rsqrt(var + _LN_EPS) * g_ref[...] + b_ref[...]
    o_ref[...] = out
    ob_ref[...] = out.astype(_BF16)


def _ln_kernel(h_ref, m_ref, g_ref, b_ref, o_ref, ob_ref, *, alpha):
    _ln_body(alpha * h_ref[...] + m_ref[...], g_ref, b_ref, o_ref, ob_ref)


def _ln_t_kernel(h_ref, mt_ref, g_ref, b_ref, o_ref, ob_ref, *, alpha):
    _ln_body(alpha * h_ref[...] + mt_ref[...].T, g_ref, b_ref, o_ref, ob_ref)


def _ln_residual(h, mix, g, b, alpha, transposed_mix=False, tr=256):
    t, d = h.shape
    tr = min(tr, t)
    assert t % tr == 0
    if transposed_mix:
        body = _ln_t_kernel
        mix_spec = pl.BlockSpec((d, tr), lambda i: (0, i))
    else:
        body = _ln_kernel
        mix_spec = pl.BlockSpec((tr, d), lambda i: (i, 0))
    row = pl.BlockSpec((tr, d), lambda i: (i, 0))
    vec = pl.BlockSpec((1, d), lambda i: (0, 0))
    return pl.pallas_call(
        functools.partial(body, alpha=alpha),
        grid=(t // tr,),
        in_specs=[row, mix_spec, vec, vec],
        out_specs=[row, row],
        out_shape=[jax.ShapeDtypeStruct((t, d), _F32), jax.ShapeDtypeStruct((t, d), _BF16)],
        compiler_params=_params(("parallel",)),
        name="ln_residual",
    )(h, mix, g.reshape(1, d), b.reshape(1, d))


def _dil_kernel(q_ref, kc_ref, kp_ref, vc_ref, vp_ref, o_ref, lse_ref, *, dil, n_heads):
    n = _LANES
    has_prev = pl.program_id(2) > 0
    row = lax.broadcasted_iota(jnp.int32, (n, n), 0)
    col = lax.broadcasted_iota(jnp.int32, (n, n), 1)
    diff = row - col
    valid_c = diff >= 0
    valid_p = (diff <= 0) & has_prev
    dist_c = diff.astype(_F32)
    dist_p = dist_c + float(n)
    scale = 1.0 / math.sqrt(_HEAD_DIM)
    lse_tile = jnp.zeros((n, n), _F32)
    for h in range(n_heads):
        slope = float(dil) * 2.0 ** (-8.0 * (h + 1) / n_heads)
        sl = slice(h * _HEAD_DIM, (h + 1) * _HEAD_DIM)
        q = q_ref[0, :, sl]
        lc = jnp.where(valid_c, _dot_nt(q, kc_ref[0, :, sl]) * scale - slope * dist_c, _NEG)
        lp = jnp.where(valid_p, _dot_nt(q, kp_ref[0, :, sl]) * scale - slope * dist_p, _NEG)
        mx = jnp.maximum(jnp.max(lc, axis=1, keepdims=True), jnp.max(lp, axis=1, keepdims=True))
        pc = jnp.exp(lc - mx)
        pp = jnp.exp(lp - mx)
        den = jnp.sum(pc, axis=1, keepdims=True) + jnp.sum(pp, axis=1, keepdims=True)
        acc = (jnp.dot(pc.astype(_BF16), vc_ref[0, :, sl], preferred_element_type=_F32)
               + jnp.dot(pp.astype(_BF16), vp_ref[0, :, sl], preferred_element_type=_F32))
        o_ref[0, :, sl] = acc / den
        lse_tile = jnp.where(col == h, mx + jnp.log(den), lse_tile)
    lse_ref[0] = lse_tile


def _dilated_group(proj, g, n_groups, n_heads):
    b, s, c = proj.shape
    w = n_heads * _HEAD_DIM
    dil = _DIL_RATES[g]
    n = _DIL_WINDOWS[g] // dil
    assert n == _LANES and n_heads <= _LANES
    l = s // dil
    nb = l // n
    assert l % n == 0
    cw = c // w
    view = proj.reshape(b, l, dil * c)

    def spec(which, prev):
        def index(bi, r, i):
            blk = jnp.maximum(i - 1, 0) if prev else i
            return (bi, blk, r * cw + g * 3 + which)
        return pl.BlockSpec((1, n, w), index)

    o, lse = pl.pallas_call(
        functools.partial(_dil_kernel, dil=dil, n_heads=n_heads),
        grid=(b, dil, nb),
        in_specs=[spec(0, False), spec(1, False), spec(1, True), spec(2, False), spec(2, True)],
        out_specs=[pl.BlockSpec((1, n, w), lambda bi, r, i: (bi, i, r)),
                   pl.BlockSpec((1, n, _LANES), lambda bi, r, i: (bi, i, r))],
        out_shape=[jax.ShapeDtypeStruct((b, l, dil * w), _F32),
                   jax.ShapeDtypeStruct((b, l, dil * _LANES), _F32)],
        compiler_params=_params(("parallel", "parallel", "parallel")),
        name=f"dilated_attention_g{g}",
    )(view, view, view, view, view)
    return o.reshape(b * s, w), lse.reshape(b * s, _LANES)


def _merge_kernel(*refs, n_groups, n_heads):
    o_refs = refs[:n_groups]
    l_refs = refs[n_groups:2 * n_groups]
    out_ref = refs[2 * n_groups]
    lses = [r[...] for r in l_refs]
    for h in range(n_heads):
        sl = slice(h * _HEAD_DIM, (h + 1) * _HEAD_DIM)
        lh = [x[:, h:h + 1] for x in lses]
        mx = functools.reduce(jnp.maximum, lh)
        ws = [jnp.exp(x - mx) for x in lh]
        num = sum(wg * o[:, sl] for wg, o in zip(ws, o_refs))
        out_ref[:, sl] = (num / sum(ws)).astype(out_ref.dtype)


def _merge_groups(outs, lses, n_heads, tr=256):
    t, w = outs[0].shape
    tr = min(tr, t)
    ng = len(outs)
    return pl.pallas_call(
        functools.partial(_merge_kernel, n_groups=ng, n_heads=n_heads),
        grid=(t // tr,),
        in_specs=[pl.BlockSpec((tr, w), lambda i: (i, 0))] * ng
        + [pl.BlockSpec((tr, _LANES), lambda i: (i, 0))] * ng,
        out_specs=pl.BlockSpec((tr, w), lambda i: (i, 0)),
        out_shape=jax.ShapeDtypeStruct((t, w), _BF16),
        compiler_params=_params(("parallel",)),
        name="dilated_merge",
    )(*outs, *lses)


def _dilated_mixer(hb, w_in, w_out, b, s):
    d = hb.shape[1]
    n_groups = len(_DIL_RATES)
    n_heads = d // _HEAD_DIM // 2
    proj = _matmul(hb, w_in.astype(_BF16), _BF16).reshape(b, s, -1)
    outs, lses = zip(*[_dilated_group(proj, g, n_groups, n_heads) for g in range(n_groups)])
    merged = _merge_groups(outs, lses, n_heads)
    return _matmul(merged, w_out.astype(_BF16), _F32)


def _fox_gate_kernel(wf_ref, h_ref, bias_ref, c_ref, carry_ref):
    @pl.when(pl.program_id(1) == 0)
    def _():
        carry_ref[...] = jnp.zeros_like(carry_ref)

    z = _dot_nt(wf_ref[...], h_ref[0]) + bias_ref[...]
    log_f = jnp.minimum(z, 0.0) - jnp.log1p(jnp.exp(-jnp.abs(z)))
    log_f = log_f * _LOG2E
    nh, ts = log_f.shape
    lane = lax.broadcasted_iota(jnp.int32, (nh, _LANES), 1)
    carry = carry_ref[...]
    for c in range(ts // _LANES):
        x = log_f[:, c * _LANES:(c + 1) * _LANES]
        shift = 1
        while shift < _LANES:
            x = x + jnp.where(lane >= shift, pltpu.roll(x, shift, 1), 0.0)
            shift *= 2
        x = x + carry
        c_ref[0, :, c * _LANES:(c + 1) * _LANES] = x
        carry = jnp.broadcast_to(x[:, _LANES - 1:_LANES], (nh, _LANES))
    carry_ref[...] = carry


def _fox_gate(hb3, wf_t, bias, ts=512):
    b, s, d = hb3.shape
    nh = wf_t.shape[0]
    ts = min(ts, s)
    return pl.pallas_call(
        _fox_gate_kernel,
        grid=(b, s // ts),
        in_specs=[pl.BlockSpec((nh, d), lambda bi, si: (0, 0)),
                  pl.BlockSpec((1, ts, d), lambda bi, si: (bi, si, 0)),
                  pl.BlockSpec((nh, 1), lambda bi, si: (0, 0))],
        out_specs=pl.BlockSpec((1, nh, ts), lambda bi, si: (bi, 0, si)),
        out_shape=jax.ShapeDtypeStruct((b, nh, s), _F32),
        scratch_shapes=[pltpu.VMEM((nh, _LANES), _F32)],
        compiler_params=_params(("parallel", "arbitrary")),
        name="fox_gate_cumsum",
    )(wf_t, hb3, bias.reshape(nh, 1).astype(_F32))


def _fox_kernel(q_ref, k_ref, v_ref, c_ref, o_ref, m_ref, acc_ref, *, tq, heads):
    qi = pl.program_id(2)
    hd = _HEAD_DIM
    m_ref[...] = jnp.full_like(m_ref, _NEG)
    acc_ref[...] = jnp.zeros_like(acc_ref)
    ones = jnp.ones((tq, hd), _BF16)

    def step(kb, diagonal):
        k_start = pl.multiple_of(kb * tq, tq)
        for g in range(heads):
            sl = slice(g * hd, (g + 1) * hd)
            k = k_ref[0, pl.ds(k_start, tq), sl]
            v1 = jnp.concatenate([v_ref[0, pl.ds(k_start, tq), sl], ones], axis=1)
            s = _dot_nt(q_ref[0, :, sl], k) - c_ref[0, g, :, pl.ds(k_start, tq)]
            if diagonal:
                row = lax.broadcasted_iota(jnp.int32, (tq, tq), 0)
                col = lax.broadcasted_iota(jnp.int32, (tq, tq), 1)
                s = jnp.where(col <= row, s, _NEG)
            m_old = m_ref[g]
            m_new = jnp.maximum(m_old, jnp.max(s, axis=1, keepdims=True))
            alpha = jnp.exp2(m_old - m_new)
            p = jnp.exp2(s - jnp.concatenate([m_new] * (tq // hd), axis=1))
            acc_ref[g] = (jnp.concatenate([alpha, alpha], axis=1) * acc_ref[g]
                          + jnp.dot(p.astype(_BF16), v1, preferred_element_type=_F32))
            m_ref[g] = m_new

    lax.fori_loop(0, qi, lambda kb, carry: step(kb, False), None)
    step(qi, True)
    for g in range(heads):
        o_ref[0, :, g * hd:(g + 1) * hd] = (acc_ref[g, :, :hd] / acc_ref[g, :, hd:]).astype(o_ref.dtype)


def _fox_attention(qkv, c, n_heads, tq=256, heads=4):
    b, s, _ = qkv.shape
    tq = min(tq, s)
    heads = min(heads, n_heads)
    assert n_heads % heads == 0 and tq % _HEAD_DIM == 0
    ng = n_heads // heads
    gw = heads * _HEAD_DIM
    return pl.pallas_call(
        functools.partial(_fox_kernel, tq=tq, heads=heads),
        grid=(b, ng, s // tq),
        in_specs=[pl.BlockSpec((1, tq, gw), lambda bi, h, qi: (bi, qi, h)),
                  pl.BlockSpec((1, s, gw), lambda bi, h, qi: (bi, 0, ng + h)),
                  pl.BlockSpec((1, s, gw), lambda bi, h, qi: (bi, 0, 2 * ng + h)),
                  pl.BlockSpec((1, heads, 1, s), lambda bi, h, qi: (bi, h, 0, 0))],
        out_specs=pl.BlockSpec((1, tq, gw), lambda bi, h, qi: (bi, qi, h)),
        out_shape=jax.ShapeDtypeStruct((b, s, n_heads * _HEAD_DIM), _BF16),
        scratch_shapes=[pltpu.VMEM((heads, tq, _HEAD_DIM), _F32),
                        pltpu.VMEM((heads, tq, 2 * _HEAD_DIM), _F32)],
        compiler_params=_params(("parallel", "parallel", "parallel")),
        name="fox_attention",
    )(qkv, qkv, qkv, c.reshape(b, n_heads, 1, s))


def _forgetting_mixer(hb, w_in, f_bias, w_out, b, s):
    d = hb.shape[1]
    n_heads = d // _HEAD_DIM
    width = n_heads * _HEAD_DIM
    q_scale = _LOG2E / math.sqrt(_HEAD_DIM)
    w_qkv = jnp.concatenate([w_in[:, :width] * q_scale, w_in[:, width:3 * width]], axis=1)
    qkv = _matmul(hb, w_qkv.astype(_BF16), _BF16).reshape(b, s, 3 * width)
    wf_t = w_in[:, 3 * width:].T.astype(_BF16)
    c = _fox_gate(hb.reshape(b, s, d), wf_t, f_bias)
    o = _fox_attention(qkv, c, n_heads)
    return _matmul(o.reshape(b * s, width), w_out.astype(_BF16), _F32)


_SUBLANES = 8


def _sorting_network(n):
    comps = []

    def merge(lo, m, r):
        step = 2 * r
        if step < m:
            merge(lo, m, step)
            merge(lo + r, m, step)
            comps.extend((i, i + r) for i in range(lo + r, lo + m - r, step))
        else:
            comps.append((lo, lo + r))

    def sort(lo, m):
        if m > 1:
            sort(lo, m // 2)
            sort(lo + m // 2, m // 2)
            merge(lo, m, 1)

    sort(0, n)
    return comps


def _pop_columns(cols, count, singles=None):
    r, t = cols[0].shape
    row = lax.broadcasted_iota(jnp.int32, (r, t), 0)
    ninf = jnp.full((r, t), -jnp.inf, _F32)
    cols = list(cols)
    if singles is not None:
        r2 = singles.shape[0]
        row2 = lax.broadcasted_iota(jnp.int32, (r2, t), 0) + r
    vals = []
    for k in range(count):
        head = cols[0]
        m = jnp.max(head, axis=0, keepdims=True)
        if singles is not None:
            m = jnp.maximum(m, jnp.max(singles, axis=0, keepdims=True))
        vals.append(m)
        left = count - 1 - k
        if left == 0:
            break
        none = r + (0 if singles is None else singles.shape[0])
        first = jnp.min(jnp.where(head == m, row, none), axis=0, keepdims=True)
        if singles is not None:
            first = jnp.minimum(first, jnp.min(jnp.where(singles == m, row2, none),
                                               axis=0, keepdims=True))
            singles = jnp.where(row2 == first, -jnp.inf, singles)
        hit = row == first
        depth = min(left, len(cols))
        cols = [jnp.where(hit, cols[d + 1] if d + 1 < len(cols) else ninf, cols[d])
                for d in range(depth)]
    return vals


def _top_values(scores, count):
    n = scores.shape[0]
    stacks = [scores[v * _SUBLANES:(v + 1) * _SUBLANES] for v in range(n // _SUBLANES)]
    for i, j in _sorting_network(len(stacks)):
        stacks[i], stacks[j] = jnp.maximum(stacks[i], stacks[j]), jnp.minimum(stacks[i], stacks[j])
    return _pop_columns(stacks, count)


def _stack_rows(rows_1t, n_rows):
    t = rows_1t[0].shape[1]
    row = lax.broadcasted_iota(jnp.int32, (n_rows, t), 0)
    out = jnp.full((n_rows, t), -jnp.inf, _F32)
    for k, v in enumerate(rows_1t):
        out = jnp.where(row == k, v, out)
    return out


def _peer_gate_kernel(q_ref, keys_ref, thr_ref, e0_ref, s1_ref, e1z_ref, *, n_heads, topk):
    half = keys_ref.shape[2]
    tt = q_ref.shape[0]
    k0 = keys_ref[0]
    k1 = keys_ref[1]
    nkeep = topk + 1
    rows = _SUBLANES * (-(-nkeep // _SUBLANES))
    row = lax.broadcasted_iota(jnp.int32, (rows, tt), 0)
    for h in range(n_heads):
        s0 = _dot_nt(k0, q_ref[:, (2 * h) * half:(2 * h + 1) * half])
        s1 = _dot_nt(k1, q_ref[:, (2 * h + 1) * half:(2 * h + 2) * half])
        a_vals = _top_values(s0, nkeep)
        b_vals = _top_values(s1, nkeep)
        a_rows = _stack_rows(a_vals, rows)
        stacks = [jnp.where(row[:_SUBLANES] < nkeep // (j + 1), a_rows[:_SUBLANES] + b_vals[j],
                            -jnp.inf) for j in range(nkeep)]
        assert nkeep // 2 <= _SUBLANES
        singles = jnp.where(row[_SUBLANES:] < nkeep, a_rows[_SUBLANES:] + b_vals[0], -jnp.inf)
        top = _pop_columns(stacks, nkeep, singles)
        z = sum(jnp.exp(tv - top[0]) for tv in top[:topk])
        thr = 0.5 * (top[topk - 1] + top[topk])
        thr_ref[h] = thr - s0
        e0_ref[h] = jnp.exp(s0 - a_vals[0])
        s1_ref[h] = s1
        e1z_ref[h] = jnp.exp(s1 - b_vals[0]) / z


def _peer_gate(q, sub_keys, n_heads, tt=128):
    t = q.shape[0]
    nk, half = sub_keys.shape[1:]
    tt = min(tt, t)
    out_spec = pl.BlockSpec((n_heads, nk, tt), lambda i: (0, 0, i))
    out_shape = jax.ShapeDtypeStruct((n_heads, nk, t), _F32)
    return pl.pallas_call(
        functools.partial(_peer_gate_kernel, n_heads=n_heads, topk=_PEER_TOPK),
        grid=(t // tt,),
        in_specs=[pl.BlockSpec((tt, q.shape[1]), lambda i: (i, 0)),
                  pl.BlockSpec((2, nk, half), lambda i: (0, 0, 0))],
        out_specs=[out_spec] * 4,
        out_shape=[out_shape] * 4,
        compiler_params=_params(("parallel",)),
        name="peer_gate",
    )(q, sub_keys)


def _gelu(x):
    return 0.5 * x * (1.0 + lax.erf(x * (1.0 / math.sqrt(2.0))))


def _peer_kernel(hb_ref, u_ref, vt_ref, thr_ref, e0_ref, s1_ref, e1z_ref, o_ref,
                 a0_ref, a1_ref, w_ref, *, ni, n_heads, n_tiles, n_work):
    k = pl.program_id(0)

    @pl.when(k == 0)
    def _():
        a0_ref[...] = jnp.zeros_like(a0_ref)
        a1_ref[...] = jnp.zeros_like(a1_ref)

    @pl.when(jnp.clip(k - 1, 0, n_work - 1) % n_tiles == 0)
    def _():
        o_ref[...] = jnp.zeros_like(o_ref)

    nk = s1_ref.shape[1]
    tt = hb_ref.shape[0]
    gate_rows = 32

    def step(a_new, a_old, row0):
        a_new[...] = _dot_nt(u_ref[...], hb_ref[...])
        for ii in range(ni):
            for tc in range(tt // _LANES):
                cs = slice(tc * _LANES, (tc + 1) * _LANES)
                for j0 in range(0, nk, gate_rows):
                    js = slice(j0, j0 + gate_rows)
                    gate = jnp.zeros((gate_rows, _LANES), _F32)
                    for h in range(n_heads):
                        thr = thr_ref[h, row0 + ii:row0 + ii + 1, cs]
                        e0 = e0_ref[h, row0 + ii:row0 + ii + 1, cs]
                        gate = gate + jnp.where(s1_ref[h, js, cs] >= thr, e1z_ref[h, js, cs] * e0, 0.0)
                    rs = slice(ii * nk + j0, ii * nk + j0 + gate_rows)
                    w_ref[rs, cs] = (_gelu(a_old[rs, cs]) * gate).astype(_BF16)
        o_ref[...] += jnp.dot(vt_ref[...], w_ref[...], preferred_element_type=_F32)

    assert 2 * ni == _SUBLANES and n_tiles % 2 == 0

    @pl.when(k % 2 == 0)
    def _():
        step(a0_ref, a1_ref, ni)

    @pl.when(k % 2 == 1)
    def _():
        step(a1_ref, a0_ref, 0)


def _peer_dense(hb, u, vt, thr, e0, s1, e1z, n_heads, tt=512, ni=4):
    t, d = hb.shape
    nk = s1.shape[1]
    tt = min(tt, t)
    et = ni * nk
    n_tiles = nk // ni
    n_work = (t // tt) * n_tiles

    def stage(lag):
        return lambda k: jnp.clip(k - lag, 0, n_work - 1)

    once = pl.Buffered(1)
    gate_spec = pl.BlockSpec((n_heads, nk, tt), lambda k: (0, 0, stage(1)(k) // n_tiles),
                             pipeline_mode=once)
    row_spec = pl.BlockSpec(
        (n_heads, _SUBLANES, tt),
        lambda k: (0, (stage(1)(k) % n_tiles) // (_SUBLANES // ni), stage(1)(k) // n_tiles))
    return pl.pallas_call(
        functools.partial(_peer_kernel, ni=ni, n_heads=n_heads, n_tiles=n_tiles, n_work=n_work),
        grid=(n_work + 1,),
        in_specs=[pl.BlockSpec((tt, d), lambda k: (stage(0)(k) // n_tiles, 0), pipeline_mode=once),
                  pl.BlockSpec((et, d), lambda k: (stage(0)(k) % n_tiles, 0)),
                  pl.BlockSpec((d, et), lambda k: (0, stage(1)(k) % n_tiles)),
                  row_spec, row_spec, gate_spec, gate_spec],
        out_specs=pl.BlockSpec((d, tt), lambda k: (0, stage(1)(k) // n_tiles)),
        out_shape=jax.ShapeDtypeStruct((d, t), _F32),
        scratch_shapes=[pltpu.VMEM((et, tt), _F32)] * 2 + [pltpu.VMEM((et, tt), _BF16)],
        compiler_params=_params(("arbitrary",)),
        name="peer_dense",
    )(hb, u, vt, thr, e0, s1, e1z)


def _peer_ffn(hb, w_q, sub_keys, u, v):
    half = sub_keys.shape[2]
    n_heads = w_q.shape[1] // (2 * half)
    q = _matmul(hb, w_q.astype(_BF16), _BF16)
    thr, e0, s1, e1z = _peer_gate(q, sub_keys.astype(_BF16), n_heads)
    return _peer_dense(hb, u.astype(_BF16), v.T.astype(_BF16), thr, e0, s1, e1z, n_heads)


def kernel(x, a_w_in, a_w_out, b_w_in, b_f_bias, b_w_out, peer_w_q, peer_sub_keys, peer_u, peer_v,
           ln_mix_g, ln_mix_b, ln_ffn_g, ln_ffn_b):
    b, s, d = x.shape
    depth = ln_mix_g.shape[0]
    alpha = (2 * depth) ** 0.25
    h = x.reshape(b * s, d)
    hb = h.astype(_BF16)
    for i in range(depth):
        j = i // 2
        if i % 2 == 0:
            mix = _dilated_mixer(hb, a_w_in[j], a_w_out[j], b, s)
        else:
            mix = _forgetting_mixer(hb, b_w_in[j], b_f_bias[j], b_w_out[j], b, s)
        h, hb = _ln_residual(h, mix, ln_mix_g[i], ln_mix_b[i], alpha)
        ffn_t = _peer_ffn(hb, peer_w_q[i], peer_sub_keys[i], peer_u[i], peer_v[i])
        h, hb = _ln_residual(h, ffn_t, ln_ffn_g[i], ln_ffn_b[i], alpha, transposed_mix=True)
    return h.reshape(b, s, d)
```

```python
import functools
import math

import jax
import jax.numpy as jnp
from jax import lax
from jax.experimental import pallas as pl
from jax.experimental.pallas import tpu as pltpu

_F32 = jnp.float32
_BF16 = jnp.bfloat16

_HEAD_DIM = 128
_DIL_WINDOWS = (128, 512, 2048)
_DIL_RATES = (1, 4, 16)
_PEER_TOPK = 16
_LN_EPS = 1e-5
_NEG = -1e30
_LANES = 128
_SUBLANES = 8
_LOG2E = math.log2(math.e)
_VMEM_LIMIT = 56 * 1024 * 1024


def _dot_nt(a, b):
    return lax.dot_general(a, b, (((1,), (1,)), ((), ())), preferred_element_type=_F32)


def _params(sem):
    return pltpu.CompilerParams(dimension_semantics=sem, vmem_limit_bytes=_VMEM_LIMIT)


def _mm_kernel(a_ref, b_ref, o_ref):
    o_ref[...] = jnp.dot(a_ref[...], b_ref[...], preferred_element_type=_F32).astype(o_ref.dtype)


def _matmul(a, b, out_dtype, tm=1024, tn=512, col_start=0, n_cols=None):
    m, k = a.shape
    n = b.shape[1] if n_cols is None else n_cols
    tm = min(tm, m)
    tn = min(tn, n)
    while n % tn or col_start % tn:
        tn -= _LANES
    assert m % tm == 0 and tn > 0
    first = col_start // tn
    return pl.pallas_call(
        _mm_kernel,
        grid=(m // tm, n // tn),
        in_specs=[pl.BlockSpec((tm, k), lambda i, j: (i, 0)),
                  pl.BlockSpec((k, tn), lambda i, j: (0, j + first))],
        out_specs=pl.BlockSpec((tm, tn), lambda i, j: (i, j)),
        out_shape=jax.ShapeDtypeStruct((m, n), out_dtype),
        compiler_params=_params(("parallel", "parallel")),
        name="matmul",
    )(a, b)


def _ln_body(y, g_ref, b_ref, o_ref, ob_ref):
    mu = jnp.mean(y, axis=-1, keepdims=True)
    d = y - mu
    var = jnp.mean(d * d, axis=-1, keepdims=True)
    out = d * lax.rsqrt(var + _LN_EPS) * g_ref[...] + b_ref[...]
    o_ref[...] = out
    ob_ref[...] = out.astype(_BF16)


def _ln_kernel(h_ref, m_ref, g_ref, b_ref, o_ref, ob_ref, *, alpha):
    _ln_body(alpha * h_ref[...] + m_ref[...], g_ref, b_ref, o_ref, ob_ref)


def _ln_t_kernel(h_ref, mt_ref, g_ref, b_ref, o_ref, ob_ref, *, alpha):
    _ln_body(alpha * h_ref[...] + mt_ref[...].T, g_ref, b_ref, o_ref, ob_ref)


def _ln_residual(h, mix, g, b, alpha, transposed_mix=False, tr=256):
    t, d = h.shape
    tr = min(tr, t)
    assert t % tr == 0
    if transposed_mix:
        body = _ln_t_kernel
        mix_spec = pl.BlockSpec((d, tr), lambda i: (0, i))
    else:
        body = _ln_kernel
        mix_spec = pl.BlockSpec((tr, d), lambda i: (i, 0))
    row = pl.BlockSpec((tr, d), lambda i: (i, 0))
    vec = pl.BlockSpec((1, d), lambda i: (0, 0))
    return pl.pallas_call(
        functools.partial(body, alpha=alpha),
        grid=(t // tr,),
        in_specs=[row, mix_spec, vec, vec],
        out_specs=[row, row],
        out_shape=[jax.ShapeDtypeStruct((t, d), _F32), jax.ShapeDtypeStruct((t, d), _BF16)],
        compiler_params=_params(("parallel",)),
        name="ln_residual",
    )(h, mix, g.reshape(1, d), b.reshape(1, d))


def _dil_kernel(q_ref, kc_ref, kp_ref, vc_ref, vp_ref, o_ref, lse_ref, *, dil, n_heads):
    n = _LANES
    has_prev = pl.program_id(1) > 0
    row = lax.broadcasted_iota(jnp.int32, (n, n), 0)
    col = lax.broadcasted_iota(jnp.int32, (n, n), 1)
    diff = row - col
    valid_c = diff >= 0
    valid_p = (diff <= 0) & has_prev
    dist_c = diff.astype(_F32)
    dist_p = dist_c + float(n)
    scale = 1.0 / math.sqrt(_HEAD_DIM)
    lse_tile = jnp.zeros((n, n), _F32)
    for h in range(n_heads):
        slope = float(dil) * 2.0 ** (-8.0 * (h + 1) / n_heads)
        sl = slice(h * _HEAD_DIM, (h + 1) * _HEAD_DIM)
        q = q_ref[0, :, sl]
        lc = jnp.where(valid_c, _dot_nt(q, kc_ref[0, :, sl]) * scale - slope * dist_c, _NEG)
        lp = jnp.where(valid_p, _dot_nt(q, kp_ref[0, :, sl]) * scale - slope * dist_p, _NEG)
        mx = jnp.maximum(jnp.max(lc, axis=1, keepdims=True), jnp.max(lp, axis=1, keepdims=True))
        pc = jnp.exp(lc - mx)
        pp = jnp.exp(lp - mx)
        den = jnp.sum(pc, axis=1, keepdims=True) + jnp.sum(pp, axis=1, keepdims=True)
        acc = (jnp.dot(pc.astype(_BF16), vc_ref[0, :, sl], preferred_element_type=_F32)
               + jnp.dot(pp.astype(_BF16), vp_ref[0, :, sl], preferred_element_type=_F32))
        o_ref[0, :, sl] = acc / den
        lse_tile = jnp.where(col == h, mx + jnp.log(den), lse_tile)
    lse_ref[0] = lse_tile


def _deinterleave_kernel(x_ref, *o_refs, dils):
    s = x_ref.shape[1]
    for o_ref, dil in zip(o_refs, dils):
        l = s // dil
        for r in range(dil):
            o_ref[0, r * l:(r + 1) * l, :] = x_ref[0, pl.ds(r, l, stride=dil), :].astype(o_ref.dtype)


def _deinterleave(h3, dils, cw=_LANES):
    b, s, d = h3.shape
    cw = min(cw, d)
    spec = pl.BlockSpec((1, s, cw), lambda bi, ci: (bi, 0, ci))
    return pl.pallas_call(
        functools.partial(_deinterleave_kernel, dils=dils),
        grid=(b, d // cw),
        in_specs=[spec],
        out_specs=[spec] * len(dils),
        out_shape=[jax.ShapeDtypeStruct((b, s, d), _BF16)] * len(dils),
        compiler_params=_params(("parallel", "parallel")),
        name="deinterleave",
    )(h3)


def _dilated_group(proj, g, n_heads, b, s):
    w = n_heads * _HEAD_DIM
    dil = _DIL_RATES[g]
    n = _DIL_WINDOWS[g] // dil
    assert n == _LANES and n_heads <= _LANES
    l = s // dil
    nb = l // n
    assert l % n == 0
    view = proj.reshape(b * dil, l, 3 * w)

    def spec(which, prev):
        def index(p, i):
            return (p, jnp.maximum(i - 1, 0) if prev else i, which)
        return pl.BlockSpec((1, n, w), index)

    return pl.pallas_call(
        functools.partial(_dil_kernel, dil=dil, n_heads=n_heads),
        grid=(b * dil, nb),
        in_specs=[spec(0, False), spec(1, False), spec(1, True), spec(2, False), spec(2, True)],
        out_specs=[pl.BlockSpec((1, n, w), lambda p, i: (p, i, 0)),
                   pl.BlockSpec((1, n, _LANES), lambda p, i: (p, i, 0))],
        out_shape=[jax.ShapeDtypeStruct((b * dil, l, w), _F32),
                   jax.ShapeDtypeStruct((b * dil, l, _LANES), _F32)],
        compiler_params=_params(("parallel", "parallel")),
        name=f"dilated_attention_g{g}",
    )(view, view, view, view, view)


def _merge_kernel(*refs, dils, n_heads):
    ng = len(dils)
    o_refs, l_refs, out_ref = refs[:ng], refs[ng:2 * ng], refs[2 * ng]
    scratch = iter(refs[2 * ng + 1:])
    tr = out_ref.shape[1]
    outs, lses = [], []
    for o_ref, l_ref, dil in zip(o_refs, l_refs, dils):
        if dil == 1:
            outs.append(o_ref.at[0, 0])
            lses.append(l_ref[0, 0])
            continue
        o_scr, l_scr = next(scratch), next(scratch)
        rows = tr // dil
        for r in range(dil):
            dst = pl.ds(r, rows, stride=dil)
            l_scr[dst, :] = l_ref[0, r]
            for h in range(n_heads):
                o_scr[h, dst, :] = o_ref[0, r, :, h * _HEAD_DIM:(h + 1) * _HEAD_DIM]
        outs.append(o_scr)
        lses.append(l_scr[...])
    for h in range(n_heads):
        sl = slice(h * _HEAD_DIM, (h + 1) * _HEAD_DIM)
        lh = [x[:, h:h + 1] for x in lses]
        mx = functools.reduce(jnp.maximum, lh)
        ws = [jnp.exp(x - mx) for x in lh]
        num = sum(wg * (o[:, sl] if dil == 1 else o[h]) for wg, o, dil in zip(ws, outs, dils))
        out_ref[0, :, sl] = (num / sum(ws)).astype(out_ref.dtype)


def _merge_groups(outs, lses, dils, n_heads, b, s, tr=256):
    w = outs[0].shape[-1]
    tr = min(tr, s)
    assert all(tr % (dil * _SUBLANES) == 0 for dil in dils)

    def spec(dil, width):
        return pl.BlockSpec((1, dil, tr // dil, width), lambda bi, i: (bi, 0, i, 0))

    view = lambda x, dil: x.reshape(b, dil, s // dil, x.shape[-1])
    scratch = []
    for dil in dils:
        if dil != 1:
            scratch += [pltpu.VMEM((n_heads, tr, _HEAD_DIM), _F32), pltpu.VMEM((tr, _LANES), _F32)]
    return pl.pallas_call(
        functools.partial(_merge_kernel, dils=dils, n_heads=n_heads),
        grid=(b, s // tr),
        in_specs=[spec(dil, w) for dil in dils] + [spec(dil, _LANES) for dil in dils],
        out_specs=pl.BlockSpec((1, tr, w), lambda bi, i: (bi, i, 0)),
        out_shape=jax.ShapeDtypeStruct((b, s, w), _BF16),
        scratch_shapes=scratch,
        compiler_params=_params(("parallel", "parallel")),
        name="dilated_merge",
    )(*[view(o, dil) for o, dil in zip(outs, dils)], *[view(x, dil) for x, dil in zip(lses, dils)])


def _dilated_mixer(h, hb, w_in, w_out, b, s):
    d = hb.shape[1]
    n_heads = d // _HEAD_DIM // 2
    w = n_heads * _HEAD_DIM
    strided = tuple(dil for dil in _DIL_RATES if dil != 1)
    copies = dict(zip(strided, _deinterleave(h.reshape(b, s, d), strided)))
    w_in = w_in.astype(_BF16)
    outs, lses = [], []
    for g, dil in enumerate(_DIL_RATES):
        rows = hb if dil == 1 else copies[dil].reshape(b * s, d)
        proj = _matmul(rows, w_in, _BF16, col_start=g * 3 * w, n_cols=3 * w)
        o, lse = _dilated_group(proj, g, n_heads, b, s)
        outs.append(o)
        lses.append(lse)
    merged = _merge_groups(outs, lses, _DIL_RATES, n_heads, b, s)
    return _matmul(merged.reshape(b * s, w), w_out.astype(_BF16), _F32)


def _fox_gate_kernel(wf_ref, h_ref, bias_ref, c_ref, carry_ref):
    @pl.when(pl.program_id(1) == 0)
    def _():
        carry_ref[...] = jnp.zeros_like(carry_ref)

    z = _dot_nt(wf_ref[...], h_ref[0]) + bias_ref[...]
    log_f = jnp.minimum(z, 0.0) - jnp.log1p(jnp.exp(-jnp.abs(z)))
    log_f = log_f * _LOG2E
    nh, ts = log_f.shape
    lane = lax.broadcasted_iota(jnp.int32, (nh, _LANES), 1)
    carry = carry_ref[...]
    for c in range(ts // _LANES):
        x = log_f[:, c * _LANES:(c + 1) * _LANES]
        shift = 1
        while shift < _LANES:
            x = x + jnp.where(lane >= shift, pltpu.roll(x, shift, 1), 0.0)
            shift *= 2
        x = x + carry
        c_ref[0, :, c * _LANES:(c + 1) * _LANES] = x
        carry = jnp.broadcast_to(x[:, _LANES - 1:_LANES], (nh, _LANES))
    carry_ref[...] = carry


def _fox_gate(hb3, wf_t, bias, ts=512):
    b, s, d = hb3.shape
    nh = wf_t.shape[0]
    ts = min(ts, s)
    return pl.pallas_call(
        _fox_gate_kernel,
        grid=(b, s // ts),
        in_specs=[pl.BlockSpec((nh, d), lambda bi, si: (0, 0)),
                  pl.BlockSpec((1, ts, d), lambda bi, si: (bi, si, 0)),
                  pl.BlockSpec((nh, 1), lambda bi, si: (0, 0))],
        out_specs=pl.BlockSpec((1, nh, ts), lambda bi, si: (bi, 0, si)),
        out_shape=jax.ShapeDtypeStruct((b, nh, s), _F32),
        scratch_shapes=[pltpu.VMEM((nh, _LANES), _F32)],
        compiler_params=_params(("parallel", "arbitrary")),
        name="fox_gate_cumsum",
    )(wf_t, hb3, bias.reshape(nh, 1).astype(_F32))


def _fox_kernel(q_ref, k_ref, v_ref, c_ref, o_ref, m_ref, acc_ref, *, tq, heads):
    qi = pl.program_id(2)
    hd = _HEAD_DIM
    m_ref[...] = jnp.full_like(m_ref, _NEG)
    acc_ref[...] = jnp.zeros_like(acc_ref)
    ones = jnp.ones((tq, hd), _BF16)

    def step(kb, diagonal):
        k_start = pl.multiple_of(kb * tq, tq)
        for g in range(heads):
            sl = slice(g * hd, (g + 1) * hd)
            k = k_ref[0, pl.ds(k_start, tq), sl]
            v1 = jnp.concatenate([v_ref[0, pl.ds(k_start, tq), sl], ones], axis=1)
            s = _dot_nt(q_ref[0, :, sl], k) - c_ref[0, g, :, pl.ds(k_start, tq)]
            if diagonal:
                row = lax.broadcasted_iota(jnp.int32, (tq, tq), 0)
                col = lax.broadcasted_iota(jnp.int32, (tq, tq), 1)
                s = jnp.where(col <= row, s, _NEG)
            m_old = m_ref[g]
            m_new = jnp.maximum(m_old, jnp.max(s, axis=1, keepdims=True))
            alpha = jnp.exp2(m_old - m_new)
            p = jnp.exp2(s - jnp.concatenate([m_new] * (tq // hd), axis=1))
            acc_ref[g] = (jnp.concatenate([alpha, alpha], axis=1) * acc_ref[g]
                          + jnp.dot(p.astype(_BF16), v1, preferred_element_type=_F32))
            m_ref[g] = m_new

    lax.fori_loop(0, qi, lambda kb, carry: step(kb, False), None)
    step(qi, True)
    for g in range(heads):
        o_ref[0, :, g * hd:(g + 1) * hd] = (acc_ref[g, :, :hd] / acc_ref[g, :, hd:]).astype(o_ref.dtype)


def _fox_attention(qkv, c, n_heads, tq=256, heads=4):
    b, s, _ = qkv.shape
    tq = min(tq, s)
    heads = min(heads, n_heads)
    assert n_heads % heads == 0 and tq % _HEAD_DIM == 0
    ng = n_heads // heads
    gw = heads * _HEAD_DIM
    return pl.pallas_call(
        functools.partial(_fox_kernel, tq=tq, heads=heads),
        grid=(b, ng, s // tq),
        in_specs=[pl.BlockSpec((1, tq, gw), lambda bi, h, qi: (bi, qi, h)),
                  pl.BlockSpec((1, s, gw), lambda bi, h, qi: (bi, 0, ng + h)),
                  pl.BlockSpec((1, s, gw), lambda bi, h, qi: (bi, 0, 2 * ng + h)),
                  pl.BlockSpec((1, heads, 1, s), lambda bi, h, qi: (bi, h, 0, 0))],
        out_specs=pl.BlockSpec((1, tq, gw), lambda bi, h, qi: (bi, qi, h)),
        out_shape=jax.ShapeDtypeStruct((b, s, n_heads * _HEAD_DIM), _BF16),
        scratch_shapes=[pltpu.VMEM((heads, tq, _HEAD_DIM), _F32),
                        pltpu.VMEM((heads, tq, 2 * _HEAD_DIM), _F32)],
        compiler_params=_params(("parallel", "parallel", "parallel")),
        name="fox_attention",
    )(qkv, qkv, qkv, c.reshape(b, n_heads, 1, s))


def _forgetting_mixer(hb, w_in, f_bias, w_out, b, s):
    d = hb.shape[1]
    n_heads = d // _HEAD_DIM
    width = n_heads * _HEAD_DIM
    q_scale = _LOG2E / math.sqrt(_HEAD_DIM)
    w_qkv = jnp.concatenate([w_in[:, :width] * q_scale, w_in[:, width:3 * width]], axis=1)
    qkv = _matmul(hb, w_qkv.astype(_BF16), _BF16).reshape(b, s, 3 * width)
    wf_t = w_in[:, 3 * width:].T.astype(_BF16)
    c = _fox_gate(hb.reshape(b, s, d), wf_t, f_bias)
    o = _fox_attention(qkv, c, n_heads)
    return _matmul(o.reshape(b * s, width), w_out.astype(_BF16), _F32)


def _sorting_network(n):
    comps = []

    def merge(lo, m, r):
        step = 2 * r
        if step < m:
            merge(lo, m, step)
            merge(lo + r, m, step)
            comps.extend((i, i + r) for i in range(lo + r, lo + m - r, step))
        else:
            comps.append((lo, lo + r))

    def sort(lo, m):
        if m > 1:
            sort(lo, m // 2)
            sort(lo + m // 2, m // 2)
            merge(lo, m, 1)

    sort(0, n)
    return comps


def _pop_columns(cols, count, singles=None):
    r, t = cols[0].shape
    row = lax.broadcasted_iota(jnp.int32, (r, t), 0)
    ninf = jnp.full((r, t), -jnp.inf, _F32)
    cols = list(cols)
    if singles is not None:
        r2 = singles.shape[0]
        row2 = lax.broadcasted_iota(jnp.int32, (r2, t), 0) + r
    vals = []
    for k in range(count):
        head = cols[0]
        m = jnp.max(head, axis=0, keepdims=True)
        if singles is not None:
            m = jnp.maximum(m, jnp.max(singles, axis=0, keepdims=True))
        vals.append(m)
        left = count - 1 - k
        if left == 0:
            break
        none = r + (0 if singles is None else singles.shape[0])
        first = jnp.min(jnp.where(head == m, row, none), axis=0, keepdims=True)
        if singles is not None:
            first = jnp.minimum(first, jnp.min(jnp.where(singles == m, row2, none),
                                               axis=0, keepdims=True))
            singles = jnp.where(row2 == first, -jnp.inf, singles)
        hit = row == first
        depth = min(left, len(cols))
        cols = [jnp.where(hit, cols[d + 1] if d + 1 < len(cols) else ninf, cols[d])
                for d in range(depth)]
    return vals


def _top_values(scores, count):
    n = scores.shape[0]
    stacks = [scores[v * _SUBLANES:(v + 1) * _SUBLANES] for v in range(n // _SUBLANES)]
    for i, j in _sorting_network(len(stacks)):
        stacks[i], stacks[j] = jnp.maximum(stacks[i], stacks[j]), jnp.minimum(stacks[i], stacks[j])
    return _pop_columns(stacks, count)


def _stack_rows(rows_1t, n_rows):
    t = rows_1t[0].shape[1]
    row = lax.broadcasted_iota(jnp.int32, (n_rows, t), 0)
    out = jnp.full((n_rows, t), -jnp.inf, _F32)
    for k, v in enumerate(rows_1t):
        out = jnp.where(row == k, v, out)
    return out


def _peer_gate_kernel(q_ref, keys_ref, thr_ref, e0_ref, s1_ref, e1z_ref, *, n_heads, topk):
    half = keys_ref.shape[2]
    tt = q_ref.shape[0]
    k0 = keys_ref[0]
    k1 = keys_ref[1]
    nkeep = topk + 1
    rows = _SUBLANES * (-(-nkeep // _SUBLANES))
    row = lax.broadcasted_iota(jnp.int32, (rows, tt), 0)
    for h in range(n_heads):
        s0 = _dot_nt(k0, q_ref[:, (2 * h) * half:(2 * h + 1) * half])
        s1 = _dot_nt(k1, q_ref[:, (2 * h + 1) * half:(2 * h + 2) * half])
        a_vals = _top_values(s0, nkeep)
        b_vals = _top_values(s1, nkeep)
        a_rows = _stack_rows(a_vals, rows)
        stacks = [jnp.where(row[:_SUBLANES] < nkeep // (j + 1), a_rows[:_SUBLANES] + b_vals[j],
                            -jnp.inf) for j in range(nkeep)]
        assert nkeep // 2 <= _SUBLANES
        singles = jnp.where(row[_SUBLANES:] < nkeep, a_rows[_SUBLANES:] + b_vals[0], -jnp.inf)
        top = _pop_columns(stacks, nkeep, singles)
        z = sum(jnp.exp(tv - top[0]) for tv in top[:topk])
        thr = 0.5 * (top[topk - 1] + top[topk])
        thr_ref[h] = thr - s0
        e0_ref[h] = jnp.exp(s0 - a_vals[0])
        s1_ref[h] = s1
        e1z_ref[h] = jnp.exp(s1 - b_vals[0]) / z


def _peer_gate(q, sub_keys, n_heads, tt=128):
    t = q.shape[0]
    nk, half = sub_keys.shape[1:]
    tt = min(tt, t)
    out_spec = pl.BlockSpec((n_heads, nk, tt), lambda i: (0, 0, i))
    out_shape = jax.ShapeDtypeStruct((n_heads, nk, t), _F32)
    return pl.pallas_call(
        functools.partial(_peer_gate_kernel, n_heads=n_heads, topk=_PEER_TOPK),
        grid=(t // tt,),
        in_specs=[pl.BlockSpec((tt, q.shape[1]), lambda i: (i, 0)),
                  pl.BlockSpec((2, nk, half), lambda i: (0, 0, 0))],
        out_specs=[out_spec] * 4,
        out_shape=[out_shape] * 4,
        compiler_params=_params(("parallel",)),
        name="peer_gate",
    )(q, sub_keys)


def _gelu(x):
    return 0.5 * x * (1.0 + lax.erf(x * (1.0 / math.sqrt(2.0))))


def _peer_kernel(hb_ref, u_ref, vt_ref, thr_ref, e0_ref, s1_ref, e1z_ref, o_ref,
                 a0_ref, a1_ref, w_ref, *, ni, n_heads, n_tiles, n_work):
    k = pl.program_id(0)

    @pl.when(k == 0)
    def _():
        a0_ref[...] = jnp.zeros_like(a0_ref)
        a1_ref[...] = jnp.zeros_like(a1_ref)

    @pl.when(jnp.clip(k - 1, 0, n_work - 1) % n_tiles == 0)
    def _():
        o_ref[...] = jnp.zeros_like(o_ref)

    nk = s1_ref.shape[1]
    tt = hb_ref.shape[0]
    gate_rows = 32

    def step(a_new, a_old, row0):
        a_new[...] = _dot_nt(u_ref[...], hb_ref[...])
        for ii in range(ni):
            for tc in range(tt // _LANES):
                cs = slice(tc * _LANES, (tc + 1) * _LANES)
                for j0 in range(0, nk, gate_rows):
                    js = slice(j0, j0 + gate_rows)
                    gate = jnp.zeros((gate_rows, _LANES), _F32)
                    for h in range(n_heads):
                        thr = thr_ref[h, row0 + ii:row0 + ii + 1, cs]
                        e0 = e0_ref[h, row0 + ii:row0 + ii + 1, cs]
                        gate = gate + jnp.where(s1_ref[h, js, cs] >= thr, e1z_ref[h, js, cs] * e0, 0.0)
                    rs = slice(ii * nk + j0, ii * nk + j0 + gate_rows)
                    w_ref[rs, cs] = (_gelu(a_old[rs, cs]) * gate).astype(_BF16)
        o_ref[...] += jnp.dot(vt_ref[...], w_ref[...], preferred_element_type=_F32)

    assert 2 * ni == _SUBLANES and n_tiles % 2 == 0

    @pl.when(k % 2 == 0)
    def _():
        step(a0_ref, a1_ref, ni)

    @pl.when(k % 2 == 1)
    def _():
        step(a1_ref, a0_ref, 0)


def _peer_dense(hb, u, v, thr, e0, s1, e1z, n_heads, tt=512, ni=4):
    t, d = hb.shape
    nk = s1.shape[1]
    tt = min(tt, t)
    et = ni * nk
    n_tiles = nk // ni
    n_work = (t // tt) * n_tiles
    vt = v.reshape(n_tiles, et, d).transpose(0, 2, 1)

    def stage(lag):
        return lambda k: jnp.clip(k - lag, 0, n_work - 1)

    once = pl.Buffered(1)
    gate_spec = pl.BlockSpec((n_heads, nk, tt), lambda k: (0, 0, stage(1)(k) // n_tiles),
                             pipeline_mode=once)
    row_spec = pl.BlockSpec(
        (n_heads, _SUBLANES, tt),
        lambda k: (0, (stage(1)(k) % n_tiles) // (_SUBLANES // ni), stage(1)(k) // n_tiles))
    return pl.pallas_call(
        functools.partial(_peer_kernel, ni=ni, n_heads=n_heads, n_tiles=n_tiles, n_work=n_work),
        grid=(n_work + 1,),
        in_specs=[pl.BlockSpec((tt, d), lambda k: (stage(0)(k) // n_tiles, 0), pipeline_mode=once),
                  pl.BlockSpec((et, d), lambda k: (stage(0)(k) % n_tiles, 0)),
                  pl.BlockSpec((None, d, et), lambda k: (stage(1)(k) % n_tiles, 0, 0)),
                  row_spec, row_spec, gate_spec, gate_spec],
        out_specs=pl.BlockSpec((d, tt), lambda k: (0, stage(1)(k) // n_tiles)),
        out_shape=jax.ShapeDtypeStruct((d, t), _F32),
        scratch_shapes=[pltpu.VMEM((et, tt), _F32)] * 2 + [pltpu.VMEM((et, tt), _BF16)],
        compiler_params=_params(("arbitrary",)),
        name="peer_dense",
    )(hb, u, vt, thr, e0, s1, e1z)


def _peer_ffn(hb, w_q, sub_keys, u, v):
    half = sub_keys.shape[2]
    n_heads = w_q.shape[1] // (2 * half)
    q = _matmul(hb, w_q.astype(_BF16), _BF16)
    thr, e0, s1, e1z = _peer_gate(q, sub_keys.astype(_BF16), n_heads)
    return _peer_dense(hb, u.astype(_BF16), v.astype(_BF16), thr, e0, s1, e1z, n_heads)


def kernel(x, a_w_in, a_w_out, b_w_in, b_f_bias, b_w_out, peer_w_q, peer_sub_keys, peer_u, peer_v,
           ln_mix_g, ln_mix_b, ln_ffn_g, ln_ffn_b):
    b, s, d = x.shape
    depth = ln_mix_g.shape[0]
    alpha = (2 * depth) ** 0.25
    h = x.reshape(b * s, d)
    hb = h.astype(_BF16)
    for i in range(depth):
        j = i // 2
        if i % 2 == 0:
            mix = _dilated_mixer(h, hb, a_w_in[j], a_w_out[j], b, s)
        else:
            mix = _forgetting_mixer(hb, b_w_in[j], b_f_bias[j], b_w_out[j], b, s)
        h, hb = _ln_residual(h, mix, ln_mix_g[i], ln_mix_b[i], alpha)
        ffn_t = _peer_ffn(hb, peer_w_q[i], peer_sub_keys[i], peer_u[i], peer_v[i])
        h, hb = _ln_residual(h, ffn_t, ln_ffn_g[i], ln_ffn_b[i], alpha, transposed_mix=True)
    return h.reshape(b, s, d)
```

```python
import functools
import math

import jax
import jax.numpy as jnp
from jax import lax
from jax.experimental import pallas as pl
from jax.experimental.pallas import tpu as pltpu

_F32 = jnp.float32
_BF16 = jnp.bfloat16

_HEAD_DIM = 128
_DIL_WINDOWS = (128, 512, 2048)
_DIL_RATES = (1, 4, 16)
_PEER_TOPK = 16
_LN_EPS = 1e-5
_NEG = -1e30
_LANES = 128
_SUBLANES = 8
_LOG2E = math.log2(math.e)
_VMEM_LIMIT = 56 * 1024 * 1024


def _dot_nt(a, b):
    return lax.dot_general(a, b, (((1,), (1,)), ((), ())), preferred_element_type=_F32)


def _params(sem):
    return pltpu.CompilerParams(dimension_semantics=sem, vmem_limit_bytes=_VMEM_LIMIT)


def _mm_kernel(a_ref, b_ref, o_ref):
    o_ref[...] = jnp.dot(a_ref[...], b_ref[...], preferred_element_type=_F32).astype(o_ref.dtype)


def _matmul(a, b, out_dtype, tm=1024, tn=512, col_start=0, n_cols=None):
    m, k = a.shape
    n = b.shape[1] if n_cols is None else n_cols
    tm = min(tm, m)
    tn = min(tn, n)
    while n % tn or col_start % tn:
        tn -= _LANES
    assert m % tm == 0 and tn > 0
    first = col_start // tn
    return pl.pallas_call(
        _mm_kernel,
        grid=(m // tm, n // tn),
        in_specs=[pl.BlockSpec((tm, k), lambda i, j: (i, 0)),
                  pl.BlockSpec((k, tn), lambda i, j: (0, j + first))],
        out_specs=pl.BlockSpec((tm, tn), lambda i, j: (i, j)),
        out_shape=jax.ShapeDtypeStruct((m, n), out_dtype),
        compiler_params=_params(("parallel", "parallel")),
        name="matmul",
    )(a, b)


def _ln_body(y, g_ref, b_ref, o_ref, ob_ref):
    mu = jnp.mean(y, axis=-1, keepdims=True)
    d = y - mu
    var = jnp.mean(d * d, axis=-1, keepdims=True)
    out = d * lax.rsqrt(var + _LN_EPS) * g_ref[...] + b_ref[...]
    o_ref[...] = out
    ob_ref[...] = out.astype(_BF16)


def _ln_kernel(h_ref, m_ref, g_ref, b_ref, o_ref, ob_ref, *, alpha):
    _ln_body(alpha * h_ref[...] + m_ref[...], g_ref, b_ref, o_ref, ob_ref)


def _ln_t_kernel(h_ref, mt_ref, g_ref, b_ref, o_ref, ob_ref, *, alpha):
    _ln_body(alpha * h_ref[...] + mt_ref[...].T, g_ref, b_ref, o_ref, ob_ref)


def _ln_residual(h, mix, g, b, alpha, transposed_mix=False, tr=256):
    t, d = h.shape
    tr = min(tr, t)
    assert t % tr == 0
    if transposed_mix:
        body = _ln_t_kernel
        mix_spec = pl.BlockSpec((d, tr), lambda i: (0, i))
    else:
        body = _ln_kernel
        mix_spec = pl.BlockSpec((tr, d), lambda i: (i, 0))
    row = pl.BlockSpec((tr, d), lambda i: (i, 0))
    vec = pl.BlockSpec((1, d), lambda i: (0, 0))
    return pl.pallas_call(
        functools.partial(body, alpha=alpha),
        grid=(t // tr,),
        in_specs=[row, mix_spec, vec, vec],
        out_specs=[row, row],
        out_shape=[jax.ShapeDtypeStruct((t, d), _F32), jax.ShapeDtypeStruct((t, d), _BF16)],
        compiler_params=_params(("parallel",)),
        name="ln_residual",
    )(h, mix, g.reshape(1, d), b.reshape(1, d))


def _dil_kernel(q_ref, kc_ref, kp_ref, vc_ref, vp_ref, o_ref, lse_ref, *, dil, n_heads):
    n = _LANES
    hd = _HEAD_DIM
    has_prev = pl.program_id(1) > 0
    row = lax.broadcasted_iota(jnp.int32, (n, 2 * n), 0)
    col = lax.broadcasted_iota(jnp.int32, (n, 2 * n), 1)
    dist = n + row - col
    valid = (dist >= 0) & (dist <= n) & ((col >= n) | has_prev)
    key_pos = (lax.broadcasted_iota(jnp.int32, (1, 2 * n), 1) - n).astype(_F32)
    query_pos = lax.broadcasted_iota(jnp.int32, (n, 1), 0).astype(_F32)
    lane = lax.broadcasted_iota(jnp.int32, (n, n), 1)
    ones = jnp.ones((2 * n, hd), _BF16)
    lse_tile = jnp.zeros((n, n), _F32)
    group = 4
    for h0 in range(0, n_heads, group):
        heads = range(h0, min(h0 + group, n_heads))
        slopes = [float(dil) * _LOG2E * 2.0 ** (-8.0 * (h + 1) / n_heads) for h in heads]
        sls = [slice(h * hd, (h + 1) * hd) for h in heads]
        logits = []
        for sl, c in zip(sls, slopes):
            k2 = jnp.concatenate([kp_ref[0, :, sl], kc_ref[0, :, sl]], axis=0)
            logits.append(jnp.where(valid, _dot_nt(q_ref[0, :, sl], k2) + c * key_pos, _NEG))
        probs, maxes = [], []
        for s in logits:
            mx = jnp.max(s, axis=1, keepdims=True)
            maxes.append(mx)
            probs.append(jnp.exp2(s - mx).astype(_BF16))
        for h, sl, c, p, mx in zip(heads, sls, slopes, probs, maxes):
            v2 = jnp.concatenate([vp_ref[0, :, sl], vc_ref[0, :, sl]], axis=0)
            r = jnp.dot(p, jnp.concatenate([v2, ones], axis=1), preferred_element_type=_F32)
            den = r[:, hd:hd + 1]
            o_ref[0, :, sl] = r[:, :hd] / den
            lse = (mx + jnp.log2(den) - c * query_pos) * (1.0 / _LOG2E)
            lse_tile = jnp.where(lane == h, lse, lse_tile)
    lse_ref[0] = lse_tile


def _deinterleave_kernel(x_ref, *o_refs, dils):
    s = x_ref.shape[1]
    for o_ref, dil in zip(o_refs, dils):
        l = s // dil
        for r in range(dil):
            o_ref[0, r * l:(r + 1) * l, :] = x_ref[0, pl.ds(r, l, stride=dil), :].astype(o_ref.dtype)


def _deinterleave(h3, dils, cw=_LANES):
    b, s, d = h3.shape
    cw = min(cw, d)
    spec = pl.BlockSpec((1, s, cw), lambda bi, ci: (bi, 0, ci))
    return pl.pallas_call(
        functools.partial(_deinterleave_kernel, dils=dils),
        grid=(b, d // cw),
        in_specs=[spec],
        out_specs=[spec] * len(dils),
        out_shape=[jax.ShapeDtypeStruct((b, s, d), _BF16)] * len(dils),
        compiler_params=_params(("parallel", "parallel")),
        name="deinterleave",
    )(h3)


def _dilated_group(proj, g, n_heads, b, s):
    w = n_heads * _HEAD_DIM
    dil = _DIL_RATES[g]
    n = _DIL_WINDOWS[g] // dil
    assert n == _LANES and n_heads <= _LANES
    l = s // dil
    nb = l // n
    assert l % n == 0
    view = proj.reshape(b * dil, l, 3 * w)

    def spec(which, prev):
        def index(p, i):
            return (p, jnp.maximum(i - 1, 0) if prev else i, which)
        return pl.BlockSpec((1, n, w), index)

    return pl.pallas_call(
        functools.partial(_dil_kernel, dil=dil, n_heads=n_heads),
        grid=(b * dil, nb),
        in_specs=[spec(0, False), spec(1, False), spec(1, True), spec(2, False), spec(2, True)],
        out_specs=[pl.BlockSpec((1, n, w), lambda p, i: (p, i, 0)),
                   pl.BlockSpec((1, n, _LANES), lambda p, i: (p, i, 0))],
        out_shape=[jax.ShapeDtypeStruct((b * dil, l, w), _F32),
                   jax.ShapeDtypeStruct((b * dil, l, _LANES), _F32)],
        compiler_params=_params(("parallel", "parallel")),
        name=f"dilated_attention_g{g}",
    )(view, view, view, view, view)


def _merge_kernel(*refs, dils, n_heads):
    ng = len(dils)
    o_refs, l_refs, out_ref = refs[:ng], refs[ng:2 * ng], refs[2 * ng]
    scratch = iter(refs[2 * ng + 1:])
    tr = out_ref.shape[1]
    outs, lses = [], []
    for o_ref, l_ref, dil in zip(o_refs, l_refs, dils):
        if dil == 1:
            outs.append(o_ref.at[0, 0])
            lses.append(l_ref[0, 0])
            continue
        o_scr, l_scr = next(scratch), next(scratch)
        rows = tr // dil
        for r in range(dil):
            dst = pl.ds(r, rows, stride=dil)
            l_scr[dst, :] = l_ref[0, r]
            for h in range(n_heads):
                o_scr[h, dst, :] = o_ref[0, r, :, h * _HEAD_DIM:(h + 1) * _HEAD_DIM]
        outs.append(o_scr)
        lses.append(l_scr[...])
    for h in range(n_heads):
        sl = slice(h * _HEAD_DIM, (h + 1) * _HEAD_DIM)
        lh = [x[:, h:h + 1] for x in lses]
        mx = functools.reduce(jnp.maximum, lh)
        ws = [jnp.exp(x - mx) for x in lh]
        num = sum(wg * (o[:, sl] if dil == 1 else o[h]) for wg, o, dil in zip(ws, outs, dils))
        out_ref[0, :, sl] = (num / sum(ws)).astype(out_ref.dtype)


def _merge_groups(outs, lses, dils, n_heads, b, s, tr=256):
    w = outs[0].shape[-1]
    tr = min(tr, s)
    assert all(tr % (dil * _SUBLANES) == 0 for dil in dils)

    def spec(dil, width):
        return pl.BlockSpec((1, dil, tr // dil, width), lambda bi, i: (bi, 0, i, 0))

    view = lambda x, dil: x.reshape(b, dil, s // dil, x.shape[-1])
    scratch = []
    for dil in dils:
        if dil != 1:
            scratch += [pltpu.VMEM((n_heads, tr, _HEAD_DIM), _F32), pltpu.VMEM((tr, _LANES), _F32)]
    return pl.pallas_call(
        functools.partial(_merge_kernel, dils=dils, n_heads=n_heads),
        grid=(b, s // tr),
        in_specs=[spec(dil, w) for dil in dils] + [spec(dil, _LANES) for dil in dils],
        out_specs=pl.BlockSpec((1, tr, w), lambda bi, i: (bi, i, 0)),
        out_shape=jax.ShapeDtypeStruct((b, s, w), _BF16),
        scratch_shapes=scratch,
        compiler_params=_params(("parallel", "parallel")),
        name="dilated_merge",
    )(*[view(o, dil) for o, dil in zip(outs, dils)], *[view(x, dil) for x, dil in zip(lses, dils)])


def _dilated_mixer(h, hb, w_in, w_out, b, s):
    d = hb.shape[1]
    n_heads = d // _HEAD_DIM // 2
    w = n_heads * _HEAD_DIM
    strided = tuple(dil for dil in _DIL_RATES if dil != 1)
    copies = dict(zip(strided, _deinterleave(h.reshape(b, s, d), strided)))
    col_scale = jnp.ones((1, len(_DIL_RATES), 3, 1), _F32).at[:, :, 0].set(_LOG2E / math.sqrt(_HEAD_DIM))
    w_in = (w_in.reshape(d, len(_DIL_RATES), 3, w) * col_scale).reshape(d, -1).astype(_BF16)
    outs, lses = [], []
    for g, dil in enumerate(_DIL_RATES):
        rows = hb if dil == 1 else copies[dil].reshape(b * s, d)
        proj = _matmul(rows, w_in, _BF16, col_start=g * 3 * w, n_cols=3 * w)
        o, lse = _dilated_group(proj, g, n_heads, b, s)
        outs.append(o)
        lses.append(lse)
    merged = _merge_groups(outs, lses, _DIL_RATES, n_heads, b, s)
    return _matmul(merged.reshape(b * s, w), w_out.astype(_BF16), _F32)


def _fox_gate_kernel(wf_ref, h_ref, bias_ref, c_ref, carry_ref):
    @pl.when(pl.program_id(1) == 0)
    def _():
        carry_ref[...] = jnp.zeros_like(carry_ref)

    z = _dot_nt(wf_ref[...], h_ref[0]) + bias_ref[...]
    log_f = jnp.minimum(z, 0.0) - jnp.log1p(jnp.exp(-jnp.abs(z)))
    log_f = log_f * _LOG2E
    nh, ts = log_f.shape
    lane = lax.broadcasted_iota(jnp.int32, (nh, _LANES), 1)
    carry = carry_ref[...]
    for c in range(ts // _LANES):
        x = log_f[:, c * _LANES:(c + 1) * _LANES]
        shift = 1
        while shift < _LANES:
            x = x + jnp.where(lane >= shift, pltpu.roll(x, shift, 1), 0.0)
            shift *= 2
        x = x + carry
        c_ref[0, :, c * _LANES:(c + 1) * _LANES] = x
        carry = jnp.broadcast_to(x[:, _LANES - 1:_LANES], (nh, _LANES))
    carry_ref[...] = carry


def _fox_gate(hb3, wf_t, bias, ts=512):
    b, s, d = hb3.shape
    nh = wf_t.shape[0]
    ts = min(ts, s)
    return pl.pallas_call(
        _fox_gate_kernel,
        grid=(b, s // ts),
        in_specs=[pl.BlockSpec((nh, d), lambda bi, si: (0, 0)),
                  pl.BlockSpec((1, ts, d), lambda bi, si: (bi, si, 0)),
                  pl.BlockSpec((nh, 1), lambda bi, si: (0, 0))],
        out_specs=pl.BlockSpec((1, nh, ts), lambda bi, si: (bi, 0, si)),
        out_shape=jax.ShapeDtypeStruct((b, nh, s), _F32),
        scratch_shapes=[pltpu.VMEM((nh, _LANES), _F32)],
        compiler_params=_params(("parallel", "arbitrary")),
        name="fox_gate_cumsum",
    )(wf_t, hb3, bias.reshape(nh, 1).astype(_F32))


def _fox_kernel(q_ref, k_ref, v_ref, c_ref, o_ref, m_ref, acc_ref, *, tq, heads):
    qi = pl.program_id(2)
    hd = _HEAD_DIM
    m_ref[...] = jnp.full_like(m_ref, _NEG)
    acc_ref[...] = jnp.zeros_like(acc_ref)
    ones = jnp.ones((tq, hd), _BF16)

    def step(kb, diagonal):
        k_start = pl.multiple_of(kb * tq, tq)
        sls = [slice(g * hd, (g + 1) * hd) for g in range(heads)]
        logits = []
        for g, sl in enumerate(sls):
            s = _dot_nt(q_ref[0, :, sl], k_ref[0, pl.ds(k_start, tq), sl]) - c_ref[0, g, :, pl.ds(k_start, tq)]
            if diagonal:
                row = lax.broadcasted_iota(jnp.int32, (tq, tq), 0)
                col = lax.broadcasted_iota(jnp.int32, (tq, tq), 1)
                s = jnp.where(col <= row, s, _NEG)
            logits.append(s)
        probs, alphas = [], []
        for g, s in enumerate(logits):
            m_old = m_ref[g]
            m_new = jnp.maximum(m_old, jnp.max(s, axis=1, keepdims=True))
            m_ref[g] = m_new
            alphas.append(jnp.exp2(m_old - m_new))
            probs.append(jnp.exp2(s - jnp.concatenate([m_new] * (tq // hd), axis=1)).astype(_BF16))
        for g, (sl, p, alpha) in enumerate(zip(sls, probs, alphas)):
            v1 = jnp.concatenate([v_ref[0, pl.ds(k_start, tq), sl], ones], axis=1)
            acc_ref[g] = (jnp.concatenate([alpha, alpha], axis=1) * acc_ref[g]
                          + jnp.dot(p, v1, preferred_element_type=_F32))

    lax.fori_loop(0, qi, lambda kb, carry: step(kb, False), None)
    step(qi, True)
    for g in range(heads):
        o_ref[0, :, g * hd:(g + 1) * hd] = (acc_ref[g, :, :hd] / acc_ref[g, :, hd:]).astype(o_ref.dtype)


def _fox_attention(qkv, c, n_heads, tq=512, heads=4):
    b, s, _ = qkv.shape
    tq = min(tq, s)
    heads = min(heads, n_heads)
    assert n_heads % heads == 0 and tq % _HEAD_DIM == 0
    ng = n_heads // heads
    gw = heads * _HEAD_DIM
    return pl.pallas_call(
        functools.partial(_fox_kernel, tq=tq, heads=heads),
        grid=(b, ng, s // tq),
        in_specs=[pl.BlockSpec((1, tq, gw), lambda bi, h, qi: (bi, qi, h)),
                  pl.BlockSpec((1, s, gw), lambda bi, h, qi: (bi, 0, ng + h)),
                  pl.BlockSpec((1, s, gw), lambda bi, h, qi: (bi, 0, 2 * ng + h)),
                  pl.BlockSpec((1, heads, 1, s), lambda bi, h, qi: (bi, h, 0, 0))],
        out_specs=pl.BlockSpec((1, tq, gw), lambda bi, h, qi: (bi, qi, h)),
        out_shape=jax.ShapeDtypeStruct((b, s, n_heads * _HEAD_DIM), _BF16),
        scratch_shapes=[pltpu.VMEM((heads, tq, _HEAD_DIM), _F32),
                        pltpu.VMEM((heads, tq, 2 * _HEAD_DIM), _F32)],
        compiler_params=_params(("parallel", "parallel", "parallel")),
        name="fox_attention",
    )(qkv, qkv, qkv, c.reshape(b, n_heads, 1, s))


def _forgetting_mixer(hb, w_in, f_bias, w_out, b, s):
    d = hb.shape[1]
    n_heads = d // _HEAD_DIM
    width = n_heads * _HEAD_DIM
    q_scale = _LOG2E / math.sqrt(_HEAD_DIM)
    w_qkv = jnp.concatenate([w_in[:, :width] * q_scale, w_in[:, width:3 * width]], axis=1)
    qkv = _matmul(hb, w_qkv.astype(_BF16), _BF16).reshape(b, s, 3 * width)
    wf_t = w_in[:, 3 * width:].T.astype(_BF16)
    c = _fox_gate(hb.reshape(b, s, d), wf_t, f_bias)
    o = _fox_attention(qkv, c, n_heads)
    return _matmul(o.reshape(b * s, width), w_out.astype(_BF16), _F32)


def _sorting_network(n):
    comps = []

    def merge(lo, m, r):
        step = 2 * r
        if step < m:
            merge(lo, m, step)
            merge(lo + r, m, step)
            comps.extend((i, i + r) for i in range(lo + r, lo + m - r, step))
        else:
            comps.append((lo, lo + r))

    def sort(lo, m):
        if m > 1:
            sort(lo, m // 2)
            sort(lo + m // 2, m // 2)
            merge(lo, m, 1)

    sort(0, n)
    return comps


def _pop_columns(cols, count, singles=None):
    r, t = cols[0].shape
    row = lax.broadcasted_iota(jnp.int32, (r, t), 0)
    ninf = jnp.full((r, t), -jnp.inf, _F32)
    cols = list(cols)
    if singles is not None:
        r2 = singles.shape[0]
        row2 = lax.broadcasted_iota(jnp.int32, (r2, t), 0) + r
    vals = []
    for k in range(count):
        head = cols[0]
        m = jnp.max(head, axis=0, keepdims=True)
        if singles is not None:
            m = jnp.maximum(m, jnp.max(singles, axis=0, keepdims=True))
        vals.append(m)
        left = count - 1 - k
        if left == 0:
            break
        none = r + (0 if singles is None else singles.shape[0])
        first = jnp.min(jnp.where(head == m, row, none), axis=0, keepdims=True)
        if singles is not None:
            first = jnp.minimum(first, jnp.min(jnp.where(singles == m, row2, none),
                                               axis=0, keepdims=True))
            singles = jnp.where(row2 == first, -jnp.inf, singles)
        hit = row == first
        depth = min(left, len(cols))
        cols = [jnp.where(hit, cols[d + 1] if d + 1 < len(cols) else ninf, cols[d])
                for d in range(depth)]
    return vals


def _top_values(scores, count):
    n = scores.shape[0]
    stacks = [scores[v * _SUBLANES:(v + 1) * _SUBLANES] for v in range(n // _SUBLANES)]
    for i, j in _sorting_network(len(stacks)):
        stacks[i], stacks[j] = jnp.maximum(stacks[i], stacks[j]), jnp.minimum(stacks[i], stacks[j])
    return _pop_columns(stacks, count)


def _stack_rows(rows_1t, n_rows):
    t = rows_1t[0].shape[1]
    row = lax.broadcasted_iota(jnp.int32, (n_rows, t), 0)
    out = jnp.full((n_rows, t), -jnp.inf, _F32)
    for k, v in enumerate(rows_1t):
        out = jnp.where(row == k, v, out)
    return out


def _peer_gate_kernel(q_ref, keys_ref, thr_ref, e0_ref, s1_ref, e1z_ref, *, n_heads, topk):
    half = keys_ref.shape[2]
    tt = q_ref.shape[0]
    k0 = keys_ref[0]
    k1 = keys_ref[1]
    nkeep = topk + 1
    rows = _SUBLANES * (-(-nkeep // _SUBLANES))
    row = lax.broadcasted_iota(jnp.int32, (rows, tt), 0)
    for h in range(n_heads):
        s0 = _dot_nt(k0, q_ref[:, (2 * h) * half:(2 * h + 1) * half])
        s1 = _dot_nt(k1, q_ref[:, (2 * h + 1) * half:(2 * h + 2) * half])
        a_vals = _top_values(s0, nkeep)
        b_vals = _top_values(s1, nkeep)
        a_rows = _stack_rows(a_vals, rows)
        stacks = [jnp.where(row[:_SUBLANES] < nkeep // (j + 1), a_rows[:_SUBLANES] + b_vals[j],
                            -jnp.inf) for j in range(nkeep)]
        assert nkeep // 2 <= _SUBLANES
        singles = jnp.where(row[_SUBLANES:] < nkeep, a_rows[_SUBLANES:] + b_vals[0], -jnp.inf)
        top = _pop_columns(stacks, nkeep, singles)
        z = sum(jnp.exp(tv - top[0]) for tv in top[:topk])
        thr = 0.5 * (top[topk - 1] + top[topk])
        thr_ref[h] = thr - s0
        e0_ref[h] = jnp.exp(s0 - a_vals[0])
        s1_ref[h] = s1
        e1z_ref[h] = jnp.exp(s1 - b_vals[0]) / z


def _peer_gate(q, sub_keys, n_heads, tt=128):
    t = q.shape[0]
    nk, half = sub_keys.shape[1:]
    tt = min(tt, t)
    out_spec = pl.BlockSpec((n_heads, nk, tt), lambda i: (0, 0, i))
    out_shape = jax.ShapeDtypeStruct((n_heads, nk, t), _F32)
    return pl.pallas_call(
        functools.partial(_peer_gate_kernel, n_heads=n_heads, topk=_PEER_TOPK),
        grid=(t // tt,),
        in_specs=[pl.BlockSpec((tt, q.shape[1]), lambda i: (i, 0)),
                  pl.BlockSpec((2, nk, half), lambda i: (0, 0, 0))],
        out_specs=[out_spec] * 4,
        out_shape=[out_shape] * 4,
        compiler_params=_params(("parallel",)),
        name="peer_gate",
    )(q, sub_keys)


def _gelu(x):
    return 0.5 * x * (1.0 + lax.erf(x * (1.0 / math.sqrt(2.0))))


def _peer_kernel(hb_ref, u_ref, vt_ref, thr_ref, e0_ref, s1_ref, e1z_ref, o_ref,
                 a0_ref, a1_ref, w_ref, *, ni, n_heads, n_tiles, n_work):
    k = pl.program_id(0)

    @pl.when(k == 0)
    def _():
        a0_ref[...] = jnp.zeros_like(a0_ref)
        a1_ref[...] = jnp.zeros_like(a1_ref)

    @pl.when(jnp.clip(k - 1, 0, n_work - 1) % n_tiles == 0)
    def _():
        o_ref[...] = jnp.zeros_like(o_ref)

    nk = s1_ref.shape[1]
    tt = hb_ref.shape[0]
    gate_rows = 16
    halves = 2

    def gate_block(a_old, row0, ii):
        for tc in range(tt // _LANES):
            cs = slice(tc * _LANES, (tc + 1) * _LANES)
            for j0 in range(0, nk, gate_rows):
                js = slice(j0, j0 + gate_rows)
                gate = jnp.zeros((gate_rows, _LANES), _F32)
                for h in range(n_heads):
                    thr = thr_ref[h, row0 + ii:row0 + ii + 1, cs]
                    e0 = e0_ref[h, row0 + ii:row0 + ii + 1, cs]
                    gate = gate + jnp.where(s1_ref[h, js, cs] >= thr, e1z_ref[h, js, cs] * e0, 0.0)
                rs = slice(ii * nk + j0, ii * nk + j0 + gate_rows)
                w_ref[rs, cs] = (_gelu(a_old[rs, cs]) * gate).astype(_BF16)

    def step(a_new, a_old, row0):
        per = ni // halves
        for half in range(halves):
            for ii in range(half * per, (half + 1) * per):
                gate_block(a_old, row0, ii)
            if half == 0:
                a_new[...] = _dot_nt(u_ref[...], hb_ref[...])
            es = slice(half * per * nk, (half + 1) * per * nk)
            o_ref[...] += jnp.dot(vt_ref[:, es], w_ref[es, :], preferred_element_type=_F32)

    assert 2 * ni == _SUBLANES and n_tiles % 2 == 0

    @pl.when(k % 2 == 0)
    def _():
        step(a0_ref, a1_ref, ni)

    @pl.when(k % 2 == 1)
    def _():
        step(a1_ref, a0_ref, 0)


def _peer_dense(hb, u, v, thr, e0, s1, e1z, n_heads, tt=512, ni=4):
    t, d = hb.shape
    nk = s1.shape[1]
    tt = min(tt, t)
    et = ni * nk
    n_tiles = nk // ni
    n_work = (t // tt) * n_tiles
    vt = v.reshape(n_tiles, et, d).transpose(0, 2, 1)

    def stage(lag):
        return lambda k: jnp.clip(k - lag, 0, n_work - 1)

    once = pl.Buffered(1)
    gate_spec = pl.BlockSpec((n_heads, nk, tt), lambda k: (0, 0, stage(1)(k) // n_tiles),
                             pipeline_mode=once)
    row_spec = pl.BlockSpec(
        (n_heads, _SUBLANES, tt),
        lambda k: (0, (stage(1)(k) % n_tiles) // (_SUBLANES // ni), stage(1)(k) // n_tiles))
    return pl.pallas_call(
        functools.partial(_peer_kernel, ni=ni, n_heads=n_heads, n_tiles=n_tiles, n_work=n_work),
        grid=(n_work + 1,),
        in_specs=[pl.BlockSpec((tt, d), lambda k: (stage(0)(k) // n_tiles, 0), pipeline_mode=once),
                  pl.BlockSpec((et, d), lambda k: (stage(0)(k) % n_tiles, 0)),
                  pl.BlockSpec((None, d, et), lambda k: (stage(1)(k) % n_tiles, 0, 0)),
                  row_spec, row_spec, gate_spec, gate_spec],
        out_specs=pl.BlockSpec((d, tt), lambda k: (0, stage(1)(k) // n_tiles)),
        out_shape=jax.ShapeDtypeStruct((d, t), _F32),
        scratch_shapes=[pltpu.VMEM((et, tt), _F32)] * 2 + [pltpu.VMEM((et, tt), _BF16)],
        compiler_params=_params(("arbitrary",)),
        name="peer_dense",
    )(hb, u, vt, thr, e0, s1, e1z)


def _peer_ffn(hb, w_q, sub_keys, u, v):
    half = sub_keys.shape[2]
    n_heads = w_q.shape[1] // (2 * half)
    q = _matmul(hb, w_q.astype(_BF16), _BF16)
    thr, e0, s1, e1z = _peer_gate(q, sub_keys.astype(_BF16), n_heads)
    return _peer_dense(hb, u.astype(_BF16), v.astype(_BF16), thr, e0, s1, e1z, n_heads)


def kernel(x, a_w_in, a_w_out, b_w_in, b_f_bias, b_w_out, peer_w_q, peer_sub_keys, peer_u, peer_v,
           ln_mix_g, ln_mix_b, ln_ffn_g, ln_ffn_b):
    b, s, d = x.shape
    depth = ln_mix_g.shape[0]
    alpha = (2 * depth) ** 0.25
    h = x.reshape(b * s, d)
    hb = h.astype(_BF16)
    for i in range(depth):
        j = i // 2
        if i % 2 == 0:
            mix = _dilated_mixer(h, hb, a_w_in[j], a_w_out[j], b, s)
        else:
            mix = _forgetting_mixer(hb, b_w_in[j], b_f_bias[j], b_w_out[j], b, s)
        h, hb = _ln_residual(h, mix, ln_mix_g[i], ln_mix_b[i], alpha)
        ffn_t = _peer_ffn(hb, peer_w_q[i], peer_sub_keys[i], peer_u[i], peer_v[i])
        h, hb = _ln_residual(h, ffn_t, ln_ffn_g[i], ln_ffn_b[i], alpha, transposed_mix=True)
    return h.reshape(b, s, d)
```

```python
import functools
import math

import jax
import jax.numpy as jnp
from jax import lax
from jax.experimental import pallas as pl
from jax.experimental.pallas import tpu as pltpu

_F32 = jnp.float32
_BF16 = jnp.bfloat16

_HEAD_DIM = 128
_DIL_WINDOWS = (128, 512, 2048)
_DIL_RATES = (1, 4, 16)
_PEER_TOPK = 16
_LN_EPS = 1e-5
_NEG = -1e30
_LANES = 128
_SUBLANES = 8
_LOG2E = math.log2(math.e)
_VMEM_LIMIT = 56 * 1024 * 1024


def _dot_nt(a, b):
    return lax.dot_general(a, b, (((1,), (1,)), ((), ())), preferred_element_type=_F32)


def _params(sem):
    return pltpu.CompilerParams(dimension_semantics=sem, vmem_limit_bytes=_VMEM_LIMIT)


def _mm_kernel(a_ref, b_ref, o_ref):
    o_ref[...] = jnp.dot(a_ref[...], b_ref[...], preferred_element_type=_F32).astype(o_ref.dtype)


def _matmul(a, b, out_dtype, tm=1024, tn=512, col_start=0, n_cols=None):
    m, k = a.shape
    n = b.shape[1] if n_cols is None else n_cols
    tm = min(tm, m)
    tn = min(tn, n)
    while n % tn or col_start % tn:
        tn -= _LANES
    assert m % tm == 0 and tn > 0
    first = col_start // tn
    return pl.pallas_call(
        _mm_kernel,
        grid=(m // tm, n // tn),
        in_specs=[pl.BlockSpec((tm, k), lambda i, j: (i, 0)),
                  pl.BlockSpec((k, tn), lambda i, j: (0, j + first))],
        out_specs=pl.BlockSpec((tm, tn), lambda i, j: (i, j)),
        out_shape=jax.ShapeDtypeStruct((m, n), out_dtype),
        compiler_params=_params(("parallel", "parallel")),
        name="matmul",
    )(a, b)


def _ln_body(y, g_ref, b_ref, o_ref, ob_ref):
    mu = jnp.mean(y, axis=-1, keepdims=True)
    d = y - mu
    var = jnp.mean(d * d, axis=-1, keepdims=True)
    out = d * lax.rsqrt(var + _LN_EPS) * g_ref[...] + b_ref[...]
    o_ref[...] = out
    ob_ref[...] = out.astype(_BF16)


def _ln_kernel(h_ref, m_ref, g_ref, b_ref, o_ref, ob_ref, *, alpha):
    _ln_body(alpha * h_ref[...] + m_ref[...], g_ref, b_ref, o_ref, ob_ref)


def _ln_t_kernel(h_ref, mt_ref, g_ref, b_ref, o_ref, ob_ref, *, alpha):
    _ln_body(alpha * h_ref[...] + mt_ref[...].T, g_ref, b_ref, o_ref, ob_ref)


def _ln_residual(h, mix, g, b, alpha, transposed_mix=False, tr=256):
    t, d = h.shape
    tr = min(tr, t)
    assert t % tr == 0
    if transposed_mix:
        body = _ln_t_kernel
        mix_spec = pl.BlockSpec((d, tr), lambda i: (0, i))
    else:
        body = _ln_kernel
        mix_spec = pl.BlockSpec((tr, d), lambda i: (i, 0))
    row = pl.BlockSpec((tr, d), lambda i: (i, 0))
    vec = pl.BlockSpec((1, d), lambda i: (0, 0))
    return pl.pallas_call(
        functools.partial(body, alpha=alpha),
        grid=(t // tr,),
        in_specs=[row, mix_spec, vec, vec],
        out_specs=[row, row],
        out_shape=[jax.ShapeDtypeStruct((t, d), _F32), jax.ShapeDtypeStruct((t, d), _BF16)],
        compiler_params=_params(("parallel",)),
        name="ln_residual",
    )(h, mix, g.reshape(1, d), b.reshape(1, d))


def _dil_kernel(q_ref, kc_ref, kp_ref, vc_ref, vp_ref, o_ref, lse_ref, *, dil, n_heads):
    n = _LANES
    hd = _HEAD_DIM
    has_prev = pl.program_id(1) > 0
    row = lax.broadcasted_iota(jnp.int32, (n, 2 * n), 0)
    col = lax.broadcasted_iota(jnp.int32, (n, 2 * n), 1)
    dist = n + row - col
    valid = (dist >= 0) & (dist <= n) & ((col >= n) | has_prev)
    key_pos = (lax.broadcasted_iota(jnp.int32, (1, 2 * n), 1) - n).astype(_F32)
    query_pos = lax.broadcasted_iota(jnp.int32, (n, 1), 0).astype(_F32)
    lane = lax.broadcasted_iota(jnp.int32, (n, n), 1)
    ones = jnp.ones((2 * n, hd), _BF16)
    lse_tile = jnp.zeros((n, n), _F32)
    group = 4
    for h0 in range(0, n_heads, group):
        heads = range(h0, min(h0 + group, n_heads))
        slopes = [float(dil) * _LOG2E * 2.0 ** (-8.0 * (h + 1) / n_heads) for h in heads]
        sls = [slice(h * hd, (h + 1) * hd) for h in heads]
        logits = []
        for sl, c in zip(sls, slopes):
            k2 = jnp.concatenate([kp_ref[0, :, sl], kc_ref[0, :, sl]], axis=0)
            logits.append(jnp.where(valid, _dot_nt(q_ref[0, :, sl], k2) + c * key_pos, _NEG))
        probs, maxes = [], []
        for s in logits:
            mx = jnp.max(s, axis=1, keepdims=True)
            maxes.append(mx)
            probs.append(jnp.exp2(s - mx).astype(_BF16))
        for h, sl, c, p, mx in zip(heads, sls, slopes, probs, maxes):
            v2 = jnp.concatenate([vp_ref[0, :, sl], vc_ref[0, :, sl]], axis=0)
            r = jnp.dot(p, jnp.concatenate([v2, ones], axis=1), preferred_element_type=_F32)
            den = r[:, hd:hd + 1]
            o_ref[0, :, sl] = r[:, :hd] / den
            lse = (mx + jnp.log2(den) - c * query_pos) * (1.0 / _LOG2E)
            lse_tile = jnp.where(lane == h, lse, lse_tile)
    lse_ref[0] = lse_tile


def _deinterleave_kernel(x_ref, *o_refs, dils):
    s = x_ref.shape[1]
    for o_ref, dil in zip(o_refs, dils):
        l = s // dil
        for r in range(dil):
            o_ref[0, r * l:(r + 1) * l, :] = x_ref[0, pl.ds(r, l, stride=dil), :].astype(o_ref.dtype)


def _deinterleave(h3, dils, cw=_LANES):
    b, s, d = h3.shape
    cw = min(cw, d)
    spec = pl.BlockSpec((1, s, cw), lambda bi, ci: (bi, 0, ci))
    return pl.pallas_call(
        functools.partial(_deinterleave_kernel, dils=dils),
        grid=(b, d // cw),
        in_specs=[spec],
        out_specs=[spec] * len(dils),
        out_shape=[jax.ShapeDtypeStruct((b, s, d), _BF16)] * len(dils),
        compiler_params=_params(("parallel", "parallel")),
        name="deinterleave",
    )(h3)


def _dilated_group(proj, g, n_heads, b, s):
    w = n_heads * _HEAD_DIM
    dil = _DIL_RATES[g]
    n = _DIL_WINDOWS[g] // dil
    assert n == _LANES and n_heads <= _LANES
    l = s // dil
    nb = l // n
    assert l % n == 0
    view = proj.reshape(b * dil, l, 3 * w)

    def spec(which, prev):
        def index(p, i):
            return (p, jnp.maximum(i - 1, 0) if prev else i, which)
        return pl.BlockSpec((1, n, w), index)

    return pl.pallas_call(
        functools.partial(_dil_kernel, dil=dil, n_heads=n_heads),
        grid=(b * dil, nb),
        in_specs=[spec(0, False), spec(1, False), spec(1, True), spec(2, False), spec(2, True)],
        out_specs=[pl.BlockSpec((1, n, w), lambda p, i: (p, i, 0)),
                   pl.BlockSpec((1, n, _LANES), lambda p, i: (p, i, 0))],
        out_shape=[jax.ShapeDtypeStruct((b * dil, l, w), _F32),
                   jax.ShapeDtypeStruct((b * dil, l, _LANES), _F32)],
        compiler_params=_params(("parallel", "parallel")),
        name=f"dilated_attention_g{g}",
    )(view, view, view, view, view)


def _merge_kernel(*refs, dils, n_heads):
    ng = len(dils)
    o_refs, l_refs, out_ref = refs[:ng], refs[ng:2 * ng], refs[2 * ng]
    scratch = iter(refs[2 * ng + 1:])
    tr = out_ref.shape[1]
    outs, lses = [], []
    for o_ref, l_ref, dil in zip(o_refs, l_refs, dils):
        if dil == 1:
            outs.append(o_ref.at[0, 0])
            lses.append(l_ref[0, 0])
            continue
        o_scr, l_scr = next(scratch), next(scratch)
        rows = tr // dil
        for r in range(dil):
            dst = pl.ds(r, rows, stride=dil)
            l_scr[dst, :] = l_ref[0, r]
            for h in range(n_heads):
                o_scr[h, dst, :] = o_ref[0, r, :, h * _HEAD_DIM:(h + 1) * _HEAD_DIM]
        outs.append(o_scr)
        lses.append(l_scr[...])
    for h in range(n_heads):
        sl = slice(h * _HEAD_DIM, (h + 1) * _HEAD_DIM)
        lh = [x[:, h:h + 1] for x in lses]
        mx = functools.reduce(jnp.maximum, lh)
        ws = [jnp.exp(x - mx) for x in lh]
        num = sum(wg * (o[:, sl] if dil == 1 else o[h]) for wg, o, dil in zip(ws, outs, dils))
        out_ref[0, :, sl] = (num / sum(ws)).astype(out_ref.dtype)


def _merge_groups(outs, lses, dils, n_heads, b, s, tr=256):
    w = outs[0].shape[-1]
    tr = min(tr, s)
    assert all(tr % (dil * _SUBLANES) == 0 for dil in dils)

    def spec(dil, width):
        return pl.BlockSpec((1, dil, tr // dil, width), lambda bi, i: (bi, 0, i, 0))

    view = lambda x, dil: x.reshape(b, dil, s // dil, x.shape[-1])
    scratch = []
    for dil in dils:
        if dil != 1:
            scratch += [pltpu.VMEM((n_heads, tr, _HEAD_DIM), _F32), pltpu.VMEM((tr, _LANES), _F32)]
    return pl.pallas_call(
        functools.partial(_merge_kernel, dils=dils, n_heads=n_heads),
        grid=(b, s // tr),
        in_specs=[spec(dil, w) for dil in dils] + [spec(dil, _LANES) for dil in dils],
        out_specs=pl.BlockSpec((1, tr, w), lambda bi, i: (bi, i, 0)),
        out_shape=jax.ShapeDtypeStruct((b, s, w), _BF16),
        scratch_shapes=scratch,
        compiler_params=_params(("parallel", "parallel")),
        name="dilated_merge",
    )(*[view(o, dil) for o, dil in zip(outs, dils)], *[view(x, dil) for x, dil in zip(lses, dils)])


def _dilated_mixer(h, hb, w_in, w_out, b, s):
    d = hb.shape[1]
    n_heads = d // _HEAD_DIM // 2
    w = n_heads * _HEAD_DIM
    strided = tuple(dil for dil in _DIL_RATES if dil != 1)
    copies = dict(zip(strided, _deinterleave(h.reshape(b, s, d), strided)))
    col_scale = jnp.ones((1, len(_DIL_RATES), 3, 1), _F32).at[:, :, 0].set(_LOG2E / math.sqrt(_HEAD_DIM))
    w_in = (w_in.reshape(d, len(_DIL_RATES), 3, w) * col_scale).reshape(d, -1).astype(_BF16)
    outs, lses = [], []
    for g, dil in enumerate(_DIL_RATES):
        rows = hb if dil == 1 else copies[dil].reshape(b * s, d)
        proj = _matmul(rows, w_in, _BF16, col_start=g * 3 * w, n_cols=3 * w)
        o, lse = _dilated_group(proj, g, n_heads, b, s)
        outs.append(o)
        lses.append(lse)
    merged = _merge_groups(outs, lses, _DIL_RATES, n_heads, b, s)
    return _matmul(merged.reshape(b * s, w), w_out.astype(_BF16), _F32)


def _fox_gate_kernel(wf_ref, h_ref, bias_ref, c_ref, carry_ref):
    @pl.when(pl.program_id(1) == 0)
    def _():
        carry_ref[...] = jnp.zeros_like(carry_ref)

    z = _dot_nt(wf_ref[...], h_ref[0]) + bias_ref[...]
    log_f = jnp.minimum(z, 0.0) - jnp.log1p(jnp.exp(-jnp.abs(z)))
    log_f = log_f * _LOG2E
    nh, ts = log_f.shape
    lane = lax.broadcasted_iota(jnp.int32, (nh, _LANES), 1)
    carry = carry_ref[...]
    for c in range(ts // _LANES):
        x = log_f[:, c * _LANES:(c + 1) * _LANES]
        shift = 1
        while shift < _LANES:
            x = x + jnp.where(lane >= shift, pltpu.roll(x, shift, 1), 0.0)
            shift *= 2
        x = x + carry
        c_ref[0, :, c * _LANES:(c + 1) * _LANES] = x
        carry = jnp.broadcast_to(x[:, _LANES - 1:_LANES], (nh, _LANES))
    carry_ref[...] = carry


def _fox_gate(hb3, wf_t, bias, ts=512):
    b, s, d = hb3.shape
    nh = wf_t.shape[0]
    ts = min(ts, s)
    return pl.pallas_call(
        _fox_gate_kernel,
        grid=(b, s // ts),
        in_specs=[pl.BlockSpec((nh, d), lambda bi, si: (0, 0)),
                  pl.BlockSpec((1, ts, d), lambda bi, si: (bi, si, 0)),
                  pl.BlockSpec((nh, 1), lambda bi, si: (0, 0))],
        out_specs=pl.BlockSpec((1, nh, ts), lambda bi, si: (bi, 0, si)),
        out_shape=jax.ShapeDtypeStruct((b, nh, s), _F32),
        scratch_shapes=[pltpu.VMEM((nh, _LANES), _F32)],
        compiler_params=_params(("parallel", "arbitrary")),
        name="fox_gate_cumsum",
    )(wf_t, hb3, bias.reshape(nh, 1).astype(_F32))


def _fox_kernel(q_ref, k_ref, v_ref, c_ref, o_ref, m_ref, acc_ref, *, tq, heads):
    qi = pl.program_id(2)
    hd = _HEAD_DIM
    m_ref[...] = jnp.full_like(m_ref, _NEG)
    acc_ref[...] = jnp.zeros_like(acc_ref)
    ones = jnp.ones((tq, hd), _BF16)

    def step(kb, diagonal):
        k_start = pl.multiple_of(kb * tq, tq)
        sls = [slice(g * hd, (g + 1) * hd) for g in range(heads)]
        logits = []
        for g, sl in enumerate(sls):
            s = _dot_nt(q_ref[0, :, sl], k_ref[0, pl.ds(k_start, tq), sl]) - c_ref[0, g, :, pl.ds(k_start, tq)]
            if diagonal:
                row = lax.broadcasted_iota(jnp.int32, (tq, tq), 0)
                col = lax.broadcasted_iota(jnp.int32, (tq, tq), 1)
                s = jnp.where(col <= row, s, _NEG)
            logits.append(s)
        probs, alphas = [], []
        for g, s in enumerate(logits):
            m_old = m_ref[g]
            m_new = jnp.maximum(m_old, jnp.max(s, axis=1, keepdims=True))
            m_ref[g] = m_new
            alphas.append(jnp.exp2(m_old - m_new))
            probs.append(jnp.exp2(s - jnp.concatenate([m_new] * (tq // hd), axis=1)).astype(_BF16))
        for g, (sl, p, alpha) in enumerate(zip(sls, probs, alphas)):
            v1 = jnp.concatenate([v_ref[0, pl.ds(k_start, tq), sl], ones], axis=1)
            acc_ref[g] = (jnp.concatenate([alpha, alpha], axis=1) * acc_ref[g]
                          + jnp.dot(p, v1, preferred_element_type=_F32))

    lax.fori_loop(0, qi, lambda kb, carry: step(kb, False), None)
    step(qi, True)
    for g in range(heads):
        o_ref[0, :, g * hd:(g + 1) * hd] = (acc_ref[g, :, :hd] / acc_ref[g, :, hd:]).astype(o_ref.dtype)


def _fox_attention(qkv, c, n_heads, tq=512, heads=4):
    b, s, _ = qkv.shape
    tq = min(tq, s)
    heads = min(heads, n_heads)
    assert n_heads % heads == 0 and tq % _HEAD_DIM == 0
    ng = n_heads // heads
    gw = heads * _HEAD_DIM
    return pl.pallas_call(
        functools.partial(_fox_kernel, tq=tq, heads=heads),
        grid=(b, ng, s // tq),
        in_specs=[pl.BlockSpec((1, tq, gw), lambda bi, h, qi: (bi, qi, h)),
                  pl.BlockSpec((1, s, gw), lambda bi, h, qi: (bi, 0, ng + h)),
                  pl.BlockSpec((1, s, gw), lambda bi, h, qi: (bi, 0, 2 * ng + h)),
                  pl.BlockSpec((1, heads, 1, s), lambda bi, h, qi: (bi, h, 0, 0))],
        out_specs=pl.BlockSpec((1, tq, gw), lambda bi, h, qi: (bi, qi, h)),
        out_shape=jax.ShapeDtypeStruct((b, s, n_heads * _HEAD_DIM), _BF16),
        scratch_shapes=[pltpu.VMEM((heads, tq, _HEAD_DIM), _F32),
                        pltpu.VMEM((heads, tq, 2 * _HEAD_DIM), _F32)],
        compiler_params=_params(("parallel", "parallel", "parallel")),
        name="fox_attention",
    )(qkv, qkv, qkv, c.reshape(b, n_heads, 1, s))


def _forgetting_mixer(hb, w_in, f_bias, w_out, b, s):
    d = hb.shape[1]
    n_heads = d // _HEAD_DIM
    width = n_heads * _HEAD_DIM
    q_scale = _LOG2E / math.sqrt(_HEAD_DIM)
    w_qkv = jnp.concatenate([w_in[:, :width] * q_scale, w_in[:, width:3 * width]], axis=1)
    qkv = _matmul(hb, w_qkv.astype(_BF16), _BF16).reshape(b, s, 3 * width)
    wf_t = w_in[:, 3 * width:].T.astype(_BF16)
    c = _fox_gate(hb.reshape(b, s, d), wf_t, f_bias)
    o = _fox_attention(qkv, c, n_heads)
    return _matmul(o.reshape(b * s, width), w_out.astype(_BF16), _F32)


def _sorting_network(n):
    comps = []

    def merge(lo, m, r):
        step = 2 * r
        if step < m:
            merge(lo, m, step)
            merge(lo + r, m, step)
            comps.extend((i, i + r) for i in range(lo + r, lo + m - r, step))
        else:
            comps.append((lo, lo + r))

    def sort(lo, m):
        if m > 1:
            sort(lo, m // 2)
            sort(lo + m // 2, m // 2)
            merge(lo, m, 1)

    sort(0, n)
    return comps


def _pop_columns(cols, count, singles=None):
    r, t = cols[0].shape
    row = lax.broadcasted_iota(jnp.int32, (r, t), 0)
    ninf = jnp.full((r, t), -jnp.inf, _F32)
    cols = list(cols)
    if singles is not None:
        r2 = singles.shape[0]
        row2 = lax.broadcasted_iota(jnp.int32, (r2, t), 0) + r
    vals = []
    for k in range(count):
        head = cols[0]
        m = jnp.max(head, axis=0, keepdims=True)
        if singles is not None:
            m = jnp.maximum(m, jnp.max(singles, axis=0, keepdims=True))
        vals.append(m)
        left = count - 1 - k
        if left == 0:
            break
        none = r + (0 if singles is None else singles.shape[0])
        first = jnp.min(jnp.where(head == m, row, none), axis=0, keepdims=True)
        if singles is not None:
            first = jnp.minimum(first, jnp.min(jnp.where(singles == m, row2, none),
                                               axis=0, keepdims=True))
            singles = jnp.where(row2 == first, -jnp.inf, singles)
        hit = row == first
        depth = min(left, len(cols))
        cols = [jnp.where(hit, cols[d + 1] if d + 1 < len(cols) else ninf, cols[d])
                for d in range(depth)]
    return vals


def _top_values(scores, count):
    n = scores.shape[0]
    stacks = [scores[v * _SUBLANES:(v + 1) * _SUBLANES] for v in range(n // _SUBLANES)]
    for i, j in _sorting_network(len(stacks)):
        stacks[i], stacks[j] = jnp.maximum(stacks[i], stacks[j]), jnp.minimum(stacks[i], stacks[j])
    return _pop_columns(stacks, count)


def _stack_rows(rows_1t, n_rows):
    t = rows_1t[0].shape[1]
    row = lax.broadcasted_iota(jnp.int32, (n_rows, t), 0)
    out = jnp.full((n_rows, t), -jnp.inf, _F32)
    for k, v in enumerate(rows_1t):
        out = jnp.where(row == k, v, out)
    return out


def _store_tile_rows(ref, h, x, ni):
    per = _SUBLANES // ni
    for g in range(x.shape[0] // _SUBLANES):
        grp = x[g * _SUBLANES:(g + 1) * _SUBLANES]
        for j in range(per):
            ref[h, g * per + j] = grp if j == 0 else pltpu.roll(grp, _SUBLANES - j * ni, 0)


def _peer_gate_kernel(q_ref, keys_ref, thr_ref, e0_ref, s1_ref, e1z_ref, *, n_heads, topk, ni):
    half = keys_ref.shape[2]
    tt = q_ref.shape[0]
    k0 = keys_ref[0]
    k1 = keys_ref[1]
    nkeep = topk + 1
    rows = _SUBLANES * (-(-nkeep // _SUBLANES))
    row = lax.broadcasted_iota(jnp.int32, (rows, tt), 0)
    for h in range(n_heads):
        s0 = _dot_nt(k0, q_ref[:, (2 * h) * half:(2 * h + 1) * half])
        s1 = _dot_nt(k1, q_ref[:, (2 * h + 1) * half:(2 * h + 2) * half])
        a_vals = _top_values(s0, nkeep)
        b_vals = _top_values(s1, nkeep)
        a_rows = _stack_rows(a_vals, rows)
        stacks = [jnp.where(row[:_SUBLANES] < nkeep // (j + 1), a_rows[:_SUBLANES] + b_vals[j],
                            -jnp.inf) for j in range(nkeep)]
        assert nkeep // 2 <= _SUBLANES
        singles = jnp.where(row[_SUBLANES:] < nkeep, a_rows[_SUBLANES:] + b_vals[0], -jnp.inf)
        top = _pop_columns(stacks, nkeep, singles)
        z = sum(jnp.exp(tv - top[0]) for tv in top[:topk])
        thr = 0.5 * (top[topk - 1] + top[topk])
        _store_tile_rows(thr_ref, h, thr - s0, ni)
        _store_tile_rows(e0_ref, h, jnp.exp(s0 - a_vals[0]), ni)
        s1_ref[h] = s1
        e1z_ref[h] = jnp.exp(s1 - b_vals[0]) / z


def _peer_gate(q, sub_keys, n_heads, ni, tt=128):
    t = q.shape[0]
    nk, half = sub_keys.shape[1:]
    tt = min(tt, t)
    assert _SUBLANES % ni == 0 and nk % _SUBLANES == 0
    n_tiles = nk // ni
    row_spec = pl.BlockSpec((n_heads, n_tiles, _SUBLANES, tt), lambda i: (0, 0, 0, i))
    row_shape = jax.ShapeDtypeStruct((n_heads, n_tiles, _SUBLANES, t), _F32)
    col_spec = pl.BlockSpec((n_heads, nk, tt), lambda i: (0, 0, i))
    col_shape = jax.ShapeDtypeStruct((n_heads, nk, t), _F32)
    return pl.pallas_call(
        functools.partial(_peer_gate_kernel, n_heads=n_heads, topk=_PEER_TOPK, ni=ni),
        grid=(t // tt,),
        in_specs=[pl.BlockSpec((tt, q.shape[1]), lambda i: (i, 0)),
                  pl.BlockSpec((2, nk, half), lambda i: (0, 0, 0))],
        out_specs=[row_spec, row_spec, col_spec, col_spec],
        out_shape=[row_shape, row_shape, col_shape, col_shape],
        compiler_params=_params(("parallel",)),
        name="peer_gate",
    )(q, sub_keys)


def _gelu(x):
    return 0.5 * x * (1.0 + lax.erf(x * (1.0 / math.sqrt(2.0))))


def _peer_kernel(hb_ref, u_ref, vt_ref, thr_ref, e0_ref, s1_ref, e1z_ref, o_ref, a_ref, w_ref,
                 *, ni, n_heads, n_tiles, n_work):
    k = pl.program_id(0)

    @pl.when(k == 0)
    def _():
        a_ref[...] = jnp.zeros_like(a_ref)

    @pl.when(jnp.clip(k - 1, 0, n_work - 1) % n_tiles == 0)
    def _():
        o_ref[...] = jnp.zeros_like(o_ref)

    nk = s1_ref.shape[1]
    tt = hb_ref.shape[0]
    gate_rows = 16
    halves = 2

    def gate_block(ii):
        for tc in range(tt // _LANES):
            cs = slice(tc * _LANES, (tc + 1) * _LANES)
            for j0 in range(0, nk, gate_rows):
                js = slice(j0, j0 + gate_rows)
                gate = jnp.zeros((gate_rows, _LANES), _F32)
                for h in range(n_heads):
                    thr = thr_ref[h, ii:ii + 1, cs]
                    e0 = e0_ref[h, ii:ii + 1, cs]
                    gate = gate + jnp.where(s1_ref[h, js, cs] >= thr, e1z_ref[h, js, cs] * e0, 0.0)
                rs = slice(ii * nk + j0, ii * nk + j0 + gate_rows)
                w_ref[rs, cs] = (_gelu(a_ref[rs, cs]) * gate).astype(_BF16)

    per = ni // halves
    for half in range(halves):
        for ii in range(half * per, (half + 1) * per):
            gate_block(ii)
        es = slice(half * per * nk, (half + 1) * per * nk)
        o_ref[...] += jnp.dot(vt_ref[:, es], w_ref[es, :], preferred_element_type=_F32)
    a_ref[...] = _dot_nt(u_ref[...], hb_ref[...])


def _peer_dense(hb, u, v, thr, e0, s1, e1z, n_heads, tt=512, ni=4):
    t, d = hb.shape
    nk = s1.shape[1]
    tt = min(tt, t)
    et = ni * nk
    n_tiles = nk // ni
    n_work = (t // tt) * n_tiles
    vt = v.reshape(n_tiles, et, d).transpose(0, 2, 1)

    def stage(lag):
        return lambda k: jnp.clip(k - lag, 0, n_work - 1)

    once = pl.Buffered(1)
    gate_spec = pl.BlockSpec((n_heads, nk, tt), lambda k: (0, 0, stage(1)(k) // n_tiles),
                             pipeline_mode=once)
    row_spec = pl.BlockSpec((n_heads, None, _SUBLANES, tt),
                            lambda k: (0, stage(1)(k) % n_tiles, 0, stage(1)(k) // n_tiles))
    return pl.pallas_call(
        functools.partial(_peer_kernel, ni=ni, n_heads=n_heads, n_tiles=n_tiles, n_work=n_work),
        grid=(n_work + 1,),
        in_specs=[pl.BlockSpec((tt, d), lambda k: (stage(0)(k) // n_tiles, 0), pipeline_mode=once),
                  pl.BlockSpec((et, d), lambda k: (stage(0)(k) % n_tiles, 0)),
                  pl.BlockSpec((None, d, et), lambda k: (stage(1)(k) % n_tiles, 0, 0)),
                  row_spec, row_spec, gate_spec, gate_spec],
        out_specs=pl.BlockSpec((d, tt), lambda k: (0, stage(1)(k) // n_tiles)),
        out_shape=jax.ShapeDtypeStruct((d, t), _F32),
        scratch_shapes=[pltpu.VMEM((et, tt), _F32), pltpu.VMEM((et, tt), _BF16)],
        compiler_params=_params(("arbitrary",)),
        name="peer_dense",
    )(hb, u, vt, thr, e0, s1, e1z)


def _peer_ffn(hb, w_q, sub_keys, u, v):
    half = sub_keys.shape[2]
    n_heads = w_q.shape[1] // (2 * half)
    q = _matmul(hb, w_q.astype(_BF16), _BF16)
    ni = 4
    thr, e0, s1, e1z = _peer_gate(q, sub_keys.astype(_BF16), n_heads, ni)
    return _peer_dense(hb, u.astype(_BF16), v.astype(_BF16), thr, e0, s1, e1z, n_heads, ni=ni)


def kernel(x, a_w_in, a_w_out, b_w_in, b_f_bias, b_w_out, peer_w_q, peer_sub_keys, peer_u, peer_v,
           ln_mix_g, ln_mix_b, ln_ffn_g, ln_ffn_b):
    b, s, d = x.shape
    depth = ln_mix_g.shape[0]
    alpha = (2 * depth) ** 0.25
    h = x.reshape(b * s, d)
    hb = h.astype(_BF16)
    for i in range(depth):
        j = i // 2
        if i % 2 == 0:
            mix = _dilated_mixer(h, hb, a_w_in[j], a_w_out[j], b, s)
        else:
            mix = _forgetting_mixer(hb, b_w_in[j], b_f_bias[j], b_w_out[j], b, s)
        h, hb = _ln_residual(h, mix, ln_mix_g[i], ln_mix_b[i], alpha)
        ffn_t = _peer_ffn(hb, peer_w_q[i], peer_sub_keys[i], peer_u[i], peer_v[i])
        h, hb = _ln_residual(h, ffn_t, ln_ffn_g[i], ln_ffn_b[i], alpha, transposed_mix=True)
    return h.reshape(b, s, d)
```

```python
import functools
import math

import jax
import jax.numpy as jnp
from jax import lax
from jax.experimental import pallas as pl
from jax.experimental.pallas import tpu as pltpu

_F32 = jnp.float32
_BF16 = jnp.bfloat16

_HEAD_DIM = 128
_DIL_WINDOWS = (128, 512, 2048)
_DIL_RATES = (1, 4, 16)
_PEER_TOPK = 16
_LN_EPS = 1e-5
_NEG = -1e30
_LANES = 128
_SUBLANES = 8
_LOG2E = math.log2(math.e)
_VMEM_LIMIT = 60 * 1024 * 1024


def _dot_nt(a, b):
    return lax.dot_general(a, b, (((1,), (1,)), ((), ())), preferred_element_type=_F32)


def _params(sem):
    return pltpu.CompilerParams(dimension_semantics=sem, vmem_limit_bytes=_VMEM_LIMIT)


def _mm_kernel(a_ref, w_ref, *rest, scaled):
    if scaled:
        s_ref, o_ref, wb_ref = rest
    else:
        o_ref, wb_ref = rest

    @pl.when(pl.program_id(1) == 0)
    def _():
        w = w_ref[...]
        if scaled:
            w = w * s_ref[...]
        wb_ref[...] = w.astype(wb_ref.dtype)

    o_ref[...] = jnp.dot(a_ref[...], wb_ref[...], preferred_element_type=_F32).astype(o_ref.dtype)


def _matmul(a, w, out_dtype, tm=1024, tn=512, col_start=0, n_cols=None, col_scale=None):
    m, k = a.shape
    n = w.shape[1] if n_cols is None else n_cols
    tm = min(tm, m)
    tn = min(tn, n)
    while n % tn or col_start % tn:
        tn -= _LANES
    assert m % tm == 0 and tn > 0
    first = col_start // tn
    w_spec = pl.BlockSpec((k, tn), lambda j, i: (0, j + first))
    operands = [a, w]
    in_specs = [pl.BlockSpec((tm, k), lambda j, i: (i, 0)), w_spec]
    if col_scale is not None:
        operands.append(col_scale.reshape(1, -1).astype(_F32))
        in_specs.append(pl.BlockSpec((1, tn), lambda j, i: (0, j + first)))
    return pl.pallas_call(
        functools.partial(_mm_kernel, scaled=col_scale is not None),
        grid=(n // tn, m // tm),
        in_specs=in_specs,
        out_specs=pl.BlockSpec((tm, tn), lambda j, i: (i, j)),
        out_shape=jax.ShapeDtypeStruct((m, n), out_dtype),
        scratch_shapes=[pltpu.VMEM((k, tn), _BF16)],
        compiler_params=_params(("parallel", "arbitrary")),
        name="matmul",
    )(*operands)


def _ln_body(y, g_ref, b_ref, o_ref, ob_ref):
    mu = jnp.mean(y, axis=-1, keepdims=True)
    d = y - mu
    var = jnp.mean(d * d, axis=-1, keepdims=True)
    out = d * lax.rsqrt(var + _LN_EPS) * g_ref[...] + b_ref[...]
    o_ref[...] = out
    ob_ref[...] = out.astype(_BF16)


def _ln_kernel(h_ref, m_ref, g_ref, b_ref, o_ref, ob_ref, *, alpha):
    _ln_body(alpha * h_ref[...] + m_ref[...], g_ref, b_ref, o_ref, ob_ref)


def _ln_t_kernel(h_ref, mt_ref, g_ref, b_ref, o_ref, ob_ref, *, alpha):
    _ln_body(alpha * h_ref[...] + mt_ref[...].T, g_ref, b_ref, o_ref, ob_ref)


def _ln_residual(h, mix, g, b, alpha, transposed_mix=False, tr=256):
    t, d = h.shape
    tr = min(tr, t)
    assert t % tr == 0
    if transposed_mix:
        body = _ln_t_kernel
        mix_spec = pl.BlockSpec((d, tr), lambda i: (0, i))
    else:
        body = _ln_kernel
        mix_spec = pl.BlockSpec((tr, d), lambda i: (i, 0))
    row = pl.BlockSpec((tr, d), lambda i: (i, 0))
    vec = pl.BlockSpec((1, d), lambda i: (0, 0))
    return pl.pallas_call(
        functools.partial(body, alpha=alpha),
        grid=(t // tr,),
        in_specs=[row, mix_spec, vec, vec],
        out_specs=[row, row],
        out_shape=[jax.ShapeDtypeStruct((t, d), _F32), jax.ShapeDtypeStruct((t, d), _BF16)],
        compiler_params=_params(("parallel",)),
        name="ln_residual",
    )(h, mix, g.reshape(1, d), b.reshape(1, d))


def _dil_kernel(q_ref, kc_ref, kp_ref, vc_ref, vp_ref, o_ref, lse_ref, *, dil, n_heads):
    n = _LANES
    hd = _HEAD_DIM
    has_prev = pl.program_id(1) > 0
    row = lax.broadcasted_iota(jnp.int32, (n, 2 * n), 0)
    col = lax.broadcasted_iota(jnp.int32, (n, 2 * n), 1)
    dist = n + row - col
    valid = (dist >= 0) & (dist <= n) & ((col >= n) | has_prev)
    key_pos = (lax.broadcasted_iota(jnp.int32, (1, 2 * n), 1) - n).astype(_F32)
    query_pos = lax.broadcasted_iota(jnp.int32, (n, 1), 0).astype(_F32)
    lane = lax.broadcasted_iota(jnp.int32, (n, n), 1)
    ones = jnp.ones((2 * n, hd), _BF16)
    lse_tile = jnp.zeros((n, n), _F32)
    group = 4
    for h0 in range(0, n_heads, group):
        heads = range(h0, min(h0 + group, n_heads))
        slopes = [float(dil) * _LOG2E * 2.0 ** (-8.0 * (h + 1) / n_heads) for h in heads]
        sls = [slice(h * hd, (h + 1) * hd) for h in heads]
        logits = []
        for sl, c in zip(sls, slopes):
            k2 = jnp.concatenate([kp_ref[0, :, sl], kc_ref[0, :, sl]], axis=0)
            logits.append(jnp.where(valid, _dot_nt(q_ref[0, :, sl], k2) + c * key_pos, _NEG))
        probs, maxes = [], []
        for s in logits:
            mx = jnp.max(s, axis=1, keepdims=True)
            maxes.append(mx)
            probs.append(jnp.exp2(s - mx).astype(_BF16))
        for h, sl, c, p, mx in zip(heads, sls, slopes, probs, maxes):
            v2 = jnp.concatenate([vp_ref[0, :, sl], vc_ref[0, :, sl]], axis=0)
            r = jnp.dot(p, jnp.concatenate([v2, ones], axis=1), preferred_element_type=_F32)
            den = r[:, hd:hd + 1]
            o_ref[0, :, sl] = r[:, :hd] / den
            lse = (mx + jnp.log2(den) - c * query_pos) * (1.0 / _LOG2E)
            lse_tile = jnp.where(lane == h, lse, lse_tile)
    lse_ref[0] = lse_tile


def _deinterleave_kernel(x_ref, *o_refs, dils):
    s = x_ref.shape[1]
    for o_ref, dil in zip(o_refs, dils):
        l = s // dil
        for r in range(dil):
            o_ref[0, r * l:(r + 1) * l, :] = x_ref[0, pl.ds(r, l, stride=dil), :].astype(o_ref.dtype)


def _deinterleave(h3, dils, cw=_LANES):
    b, s, d = h3.shape
    cw = min(cw, d)
    spec = pl.BlockSpec((1, s, cw), lambda bi, ci: (bi, 0, ci))
    return pl.pallas_call(
        functools.partial(_deinterleave_kernel, dils=dils),
        grid=(b, d // cw),
        in_specs=[spec],
        out_specs=[spec] * len(dils),
        out_shape=[jax.ShapeDtypeStruct((b, s, d), _BF16)] * len(dils),
        compiler_params=_params(("parallel", "parallel")),
        name="deinterleave",
    )(h3)


def _dilated_group(proj, g, n_heads, b, s):
    w = n_heads * _HEAD_DIM
    dil = _DIL_RATES[g]
    n = _DIL_WINDOWS[g] // dil
    assert n == _LANES and n_heads <= _LANES
    l = s // dil
    nb = l // n
    assert l % n == 0
    view = proj.reshape(b * dil, l, 3 * w)

    def spec(which, prev):
        def index(p, i):
            return (p, jnp.maximum(i - 1, 0) if prev else i, which)
        return pl.BlockSpec((1, n, w), index)

    return pl.pallas_call(
        functools.partial(_dil_kernel, dil=dil, n_heads=n_heads),
        grid=(b * dil, nb),
        in_specs=[spec(0, False), spec(1, False), spec(1, True), spec(2, False), spec(2, True)],
        out_specs=[pl.BlockSpec((1, n, w), lambda p, i: (p, i, 0)),
                   pl.BlockSpec((1, n, _LANES), lambda p, i: (p, i, 0))],
        out_shape=[jax.ShapeDtypeStruct((b * dil, l, w), _F32),
                   jax.ShapeDtypeStruct((b * dil, l, _LANES), _F32)],
        compiler_params=_params(("parallel", "parallel")),
        name=f"dilated_attention_g{g}",
    )(view, view, view, view, view)


def _merge_kernel(*refs, dils, n_heads):
    ng = len(dils)
    o_refs, l_refs, out_ref = refs[:ng], refs[ng:2 * ng], refs[2 * ng]
    scratch = iter(refs[2 * ng + 1:])
    tr = out_ref.shape[1]
    outs, lses = [], []
    for o_ref, l_ref, dil in zip(o_refs, l_refs, dils):
        if dil == 1:
            outs.append(o_ref.at[0, 0])
            lses.append(l_ref[0, 0])
            continue
        o_scr, l_scr = next(scratch), next(scratch)
        rows = tr // dil
        for r in range(dil):
            dst = pl.ds(r, rows, stride=dil)
            l_scr[dst, :] = l_ref[0, r]
            for h in range(n_heads):
                o_scr[h, dst, :] = o_ref[0, r, :, h * _HEAD_DIM:(h + 1) * _HEAD_DIM]
        outs.append(o_scr)
        lses.append(l_scr[...])
    for h in range(n_heads):
        sl = slice(h * _HEAD_DIM, (h + 1) * _HEAD_DIM)
        lh = [x[:, h:h + 1] for x in lses]
        mx = functools.reduce(jnp.maximum, lh)
        ws = [jnp.exp(x - mx) for x in lh]
        num = sum(wg * (o[:, sl] if dil == 1 else o[h]) for wg, o, dil in zip(ws, outs, dils))
        out_ref[0, :, sl] = (num / sum(ws)).astype(out_ref.dtype)


def _merge_groups(outs, lses, dils, n_heads, b, s, tr=256):
    w = outs[0].shape[-1]
    tr = min(tr, s)
    assert all(tr % (dil * _SUBLANES) == 0 for dil in dils)

    def spec(dil, width):
        return pl.BlockSpec((1, dil, tr // dil, width), lambda bi, i: (bi, 0, i, 0))

    view = lambda x, dil: x.reshape(b, dil, s // dil, x.shape[-1])
    scratch = []
    for dil in dils:
        if dil != 1:
            scratch += [pltpu.VMEM((n_heads, tr, _HEAD_DIM), _F32), pltpu.VMEM((tr, _LANES), _F32)]
    return pl.pallas_call(
        functools.partial(_merge_kernel, dils=dils, n_heads=n_heads),
        grid=(b, s // tr),
        in_specs=[spec(dil, w) for dil in dils] + [spec(dil, _LANES) for dil in dils],
        out_specs=pl.BlockSpec((1, tr, w), lambda bi, i: (bi, i, 0)),
        out_shape=jax.ShapeDtypeStruct((b, s, w), _BF16),
        scratch_shapes=scratch,
        compiler_params=_params(("parallel", "parallel")),
        name="dilated_merge",
    )(*[view(o, dil) for o, dil in zip(outs, dils)], *[view(x, dil) for x, dil in zip(lses, dils)])


def _dilated_mixer(h, hb, w_in, w_out, b, s):
    d = hb.shape[1]
    n_heads = d // _HEAD_DIM // 2
    w = n_heads * _HEAD_DIM
    strided = tuple(dil for dil in _DIL_RATES if dil != 1)
    copies = dict(zip(strided, _deinterleave(h.reshape(b, s, d), strided)))
    col_scale = jnp.ones((len(_DIL_RATES), 3, w), _F32).at[:, 0].set(_LOG2E / math.sqrt(_HEAD_DIM))
    col_scale = col_scale.reshape(-1)
    outs, lses = [], []
    for g, dil in enumerate(_DIL_RATES):
        rows = hb if dil == 1 else copies[dil].reshape(b * s, d)
        proj = _matmul(rows, w_in, _BF16, col_start=g * 3 * w, n_cols=3 * w, col_scale=col_scale)
        o, lse = _dilated_group(proj, g, n_heads, b, s)
        outs.append(o)
        lses.append(lse)
    merged = _merge_groups(outs, lses, _DIL_RATES, n_heads, b, s)
    return _matmul(merged.reshape(b * s, w), w_out, _F32)


def _fox_gate_kernel(wf_ref, h_ref, bias_ref, c_ref, carry_ref):
    @pl.when(pl.program_id(1) == 0)
    def _():
        carry_ref[...] = jnp.zeros_like(carry_ref)

    z = _dot_nt(wf_ref[...], h_ref[0]) + bias_ref[...]
    log_f = jnp.minimum(z, 0.0) - jnp.log1p(jnp.exp(-jnp.abs(z)))
    log_f = log_f * _LOG2E
    nh, ts = log_f.shape
    lane = lax.broadcasted_iota(jnp.int32, (nh, _LANES), 1)
    carry = carry_ref[...]
    for c in range(ts // _LANES):
        x = log_f[:, c * _LANES:(c + 1) * _LANES]
        shift = 1
        while shift < _LANES:
            x = x + jnp.where(lane >= shift, pltpu.roll(x, shift, 1), 0.0)
            shift *= 2
        x = x + carry
        c_ref[0, :, c * _LANES:(c + 1) * _LANES] = x
        carry = jnp.broadcast_to(x[:, _LANES - 1:_LANES], (nh, _LANES))
    carry_ref[...] = carry


def _fox_gate(hb3, wf_t, bias, ts=512):
    b, s, d = hb3.shape
    nh = wf_t.shape[0]
    ts = min(ts, s)
    return pl.pallas_call(
        _fox_gate_kernel,
        grid=(b, s // ts),
        in_specs=[pl.BlockSpec((nh, d), lambda bi, si: (0, 0)),
                  pl.BlockSpec((1, ts, d), lambda bi, si: (bi, si, 0)),
                  pl.BlockSpec((nh, 1), lambda bi, si: (0, 0))],
        out_specs=pl.BlockSpec((1, nh, ts), lambda bi, si: (bi, 0, si)),
        out_shape=jax.ShapeDtypeStruct((b, nh, s), _F32),
        scratch_shapes=[pltpu.VMEM((nh, _LANES), _F32)],
        compiler_params=_params(("parallel", "arbitrary")),
        name="fox_gate_cumsum",
    )(wf_t, hb3, bias.reshape(nh, 1).astype(_F32))


def _fox_kernel(q_ref, k_ref, v_ref, c_ref, o_ref, m_ref, acc_ref, *, tq, heads):
    qi = pl.program_id(2)
    hd = _HEAD_DIM
    m_ref[...] = jnp.full_like(m_ref, _NEG)
    acc_ref[...] = jnp.zeros_like(acc_ref)
    ones = jnp.ones((tq, hd), _BF16)

    def step(kb, diagonal):
        k_start = pl.multiple_of(kb * tq, tq)
        sls = [slice(g * hd, (g + 1) * hd) for g in range(heads)]
        logits = []
        for g, sl in enumerate(sls):
            s = _dot_nt(q_ref[0, :, sl], k_ref[0, pl.ds(k_start, tq), sl]) - c_ref[0, g, :, pl.ds(k_start, tq)]
            if diagonal:
                row = lax.broadcasted_iota(jnp.int32, (tq, tq), 0)
                col = lax.broadcasted_iota(jnp.int32, (tq, tq), 1)
                s = jnp.where(col <= row, s, _NEG)
            logits.append(s)
        probs, alphas = [], []
        for g, s in enumerate(logits):
            m_old = m_ref[g]
            m_new = jnp.maximum(m_old, jnp.max(s, axis=1, keepdims=True))
            m_ref[g] = m_new
            alphas.append(jnp.exp2(m_old - m_new))
            probs.append(jnp.exp2(s - jnp.concatenate([m_new] * (tq // hd), axis=1)).astype(_BF16))
        for g, (sl, p, alpha) in enumerate(zip(sls, probs, alphas)):
            v1 = jnp.concatenate([v_ref[0, pl.ds(k_start, tq), sl], ones], axis=1)
            acc_ref[g] = (jnp.concatenate([alpha, alpha], axis=1) * acc_ref[g]
                          + jnp.dot(p, v1, preferred_element_type=_F32))

    lax.fori_loop(0, qi, lambda kb, carry: step(kb, False), None)
    step(qi, True)
    for g in range(heads):
        o_ref[0, :, g * hd:(g + 1) * hd] = (acc_ref[g, :, :hd] / acc_ref[g, :, hd:]).astype(o_ref.dtype)


def _fox_attention(qkv, c, n_heads, tq=512, heads=4):
    b, s, _ = qkv.shape
    tq = min(tq, s)
    heads = min(heads, n_heads)
    assert n_heads % heads == 0 and tq % _HEAD_DIM == 0
    ng = n_heads // heads
    gw = heads * _HEAD_DIM
    return pl.pallas_call(
        functools.partial(_fox_kernel, tq=tq, heads=heads),
        grid=(b, ng, s // tq),
        in_specs=[pl.BlockSpec((1, tq, gw), lambda bi, h, qi: (bi, qi, h)),
                  pl.BlockSpec((1, s, gw), lambda bi, h, qi: (bi, 0, ng + h)),
                  pl.BlockSpec((1, s, gw), lambda bi, h, qi: (bi, 0, 2 * ng + h)),
                  pl.BlockSpec((1, heads, 1, s), lambda bi, h, qi: (bi, h, 0, 0))],
        out_specs=pl.BlockSpec((1, tq, gw), lambda bi, h, qi: (bi, qi, h)),
        out_shape=jax.ShapeDtypeStruct((b, s, n_heads * _HEAD_DIM), _BF16),
        scratch_shapes=[pltpu.VMEM((heads, tq, _HEAD_DIM), _F32),
                        pltpu.VMEM((heads, tq, 2 * _HEAD_DIM), _F32)],
        compiler_params=_params(("parallel", "parallel", "parallel")),
        name="fox_attention",
    )(qkv, qkv, qkv, c.reshape(b, n_heads, 1, s))


def _forgetting_mixer(hb, w_in, f_bias, w_out, b, s):
    d = hb.shape[1]
    n_heads = d // _HEAD_DIM
    width = n_heads * _HEAD_DIM
    q_scale = _LOG2E / math.sqrt(_HEAD_DIM)
    col_scale = jnp.ones((3, width), _F32).at[0].set(q_scale).reshape(-1)
    qkv = _matmul(hb, w_in, _BF16, n_cols=3 * width, col_scale=col_scale).reshape(b, s, 3 * width)
    wf_t = w_in[:, 3 * width:].T.astype(_BF16)
    c = _fox_gate(hb.reshape(b, s, d), wf_t, f_bias)
    o = _fox_attention(qkv, c, n_heads)
    return _matmul(o.reshape(b * s, width), w_out, _F32)


def _sorting_network(n):
    comps = []

    def merge(lo, m, r):
        step = 2 * r
        if step < m:
            merge(lo, m, step)
            merge(lo + r, m, step)
            comps.extend((i, i + r) for i in range(lo + r, lo + m - r, step))
        else:
            comps.append((lo, lo + r))

    def sort(lo, m):
        if m > 1:
            sort(lo, m // 2)
            sort(lo + m // 2, m // 2)
            merge(lo, m, 1)

    sort(0, n)
    return comps


def _pop_columns(cols, count, singles=None):
    r, t = cols[0].shape
    row = lax.broadcasted_iota(jnp.int32, (r, t), 0)
    ninf = jnp.full((r, t), -jnp.inf, _F32)
    cols = list(cols)
    if singles is not None:
        r2 = singles.shape[0]
        row2 = lax.broadcasted_iota(jnp.int32, (r2, t), 0) + r
    vals = []
    for k in range(count):
        head = cols[0]
        m = jnp.max(head, axis=0, keepdims=True)
        if singles is not None:
            m = jnp.maximum(m, jnp.max(singles, axis=0, keepdims=True))
        vals.append(m)
        left = count - 1 - k
        if left == 0:
            break
        none = r + (0 if singles is None else singles.shape[0])
        first = jnp.min(jnp.where(head == m, row, none), axis=0, keepdims=True)
        if singles is not None:
            first = jnp.minimum(first, jnp.min(jnp.where(singles == m, row2, none),
                                               axis=0, keepdims=True))
            singles = jnp.where(row2 == first, -jnp.inf, singles)
        hit = row == first
        depth = min(left, len(cols))
        cols = [jnp.where(hit, cols[d + 1] if d + 1 < len(cols) else ninf, cols[d])
                for d in range(depth)]
    return vals


def _top_values(scores, count):
    n = scores.shape[0]
    stacks = [scores[v * _SUBLANES:(v + 1) * _SUBLANES] for v in range(n // _SUBLANES)]
    for i, j in _sorting_network(len(stacks)):
        stacks[i], stacks[j] = jnp.maximum(stacks[i], stacks[j]), jnp.minimum(stacks[i], stacks[j])
    return _pop_columns(stacks, count)


def _stack_rows(rows_1t, n_rows):
    t = rows_1t[0].shape[1]
    row = lax.broadcasted_iota(jnp.int32, (n_rows, t), 0)
    out = jnp.full((n_rows, t), -jnp.inf, _F32)
    for k, v in enumerate(rows_1t):
        out = jnp.where(row == k, v, out)
    return out


def _store_tile_rows(ref, h, x, ni):
    per = _SUBLANES // ni
    for g in range(x.shape[0] // _SUBLANES):
        grp = x[g * _SUBLANES:(g + 1) * _SUBLANES]
        for j in range(per):
            ref[h, g * per + j] = grp if j == 0 else pltpu.roll(grp, _SUBLANES - j * ni, 0)


def _peer_gate_kernel(q_ref, keys_ref, thr_ref, e0_ref, s1_ref, e1z_ref, *, n_heads, topk, ni):
    half = keys_ref.shape[2]
    tt = q_ref.shape[0]
    k0 = keys_ref[0]
    k1 = keys_ref[1]
    nkeep = topk + 1
    rows = _SUBLANES * (-(-nkeep // _SUBLANES))
    row = lax.broadcasted_iota(jnp.int32, (rows, tt), 0)
    for h in range(n_heads):
        s0 = _dot_nt(k0, q_ref[:, (2 * h) * half:(2 * h + 1) * half])
        s1 = _dot_nt(k1, q_ref[:, (2 * h + 1) * half:(2 * h + 2) * half])
        a_vals = _top_values(s0, nkeep)
        b_vals = _top_values(s1, nkeep)
        a_rows = _stack_rows(a_vals, rows)
        stacks = [jnp.where(row[:_SUBLANES] < nkeep // (j + 1), a_rows[:_SUBLANES] + b_vals[j],
                            -jnp.inf) for j in range(nkeep)]
        assert nkeep // 2 <= _SUBLANES
        singles = jnp.where(row[_SUBLANES:] < nkeep, a_rows[_SUBLANES:] + b_vals[0], -jnp.inf)
        top = _pop_columns(stacks, nkeep, singles)
        z = sum(jnp.exp(tv - top[0]) for tv in top[:topk])
        thr = 0.5 * (top[topk - 1] + top[topk])
        _store_tile_rows(thr_ref, h, thr - s0, ni)
        _store_tile_rows(e0_ref, h, jnp.exp(s0 - a_vals[0]), ni)
        s1_ref[h] = s1
        e1z_ref[h] = jnp.exp(s1 - b_vals[0]) / z


def _peer_gate(q, sub_keys, n_heads, ni, tt=128):
    t = q.shape[0]
    nk, half = sub_keys.shape[1:]
    tt = min(tt, t)
    assert _SUBLANES % ni == 0 and nk % _SUBLANES == 0
    n_tiles = nk // ni
    row_spec = pl.BlockSpec((n_heads, n_tiles, _SUBLANES, tt), lambda i: (0, 0, 0, i))
    row_shape = jax.ShapeDtypeStruct((n_heads, n_tiles, _SUBLANES, t), _F32)
    col_spec = pl.BlockSpec((n_heads, nk, tt), lambda i: (0, 0, i))
    col_shape = jax.ShapeDtypeStruct((n_heads, nk, t), _F32)
    return pl.pallas_call(
        functools.partial(_peer_gate_kernel, n_heads=n_heads, topk=_PEER_TOPK, ni=ni),
        grid=(t // tt,),
        in_specs=[pl.BlockSpec((tt, q.shape[1]), lambda i: (i, 0)),
                  pl.BlockSpec((2, nk, half), lambda i: (0, 0, 0))],
        out_specs=[row_spec, row_spec, col_spec, col_spec],
        out_shape=[row_shape, row_shape, col_shape, col_shape],
        compiler_params=_params(("parallel",)),
        name="peer_gate",
    )(q, sub_keys)


def _gelu(x):
    return 0.5 * x * (1.0 + lax.erf(x * (1.0 / math.sqrt(2.0))))


def _peer_kernel(hb_ref, u_ref, vt_ref, thr_ref, e0_ref, s1_ref, e1z_ref, o_ref, a_ref, w_ref,
                 *, ni, n_heads, n_tiles, n_work):
    k = pl.program_id(0)

    @pl.when(k == 0)
    def _():
        a_ref[...] = jnp.zeros_like(a_ref)

    @pl.when(jnp.clip(k - 1, 0, n_work - 1) % n_tiles == 0)
    def _():
        o_ref[...] = jnp.zeros_like(o_ref)

    nk = s1_ref.shape[1]
    tt = hb_ref.shape[0]
    gate_rows = 16
    halves = 2

    def gate_block(ii):
        for tc in range(tt // _LANES):
            cs = slice(tc * _LANES, (tc + 1) * _LANES)
            for j0 in range(0, nk, gate_rows):
                js = slice(j0, j0 + gate_rows)
                gate = jnp.zeros((gate_rows, _LANES), _F32)
                for h in range(n_heads):
                    thr = thr_ref[h, ii:ii + 1, cs]
                    e0 = e0_ref[h, ii:ii + 1, cs]
                    gate = gate + jnp.where(s1_ref[h, js, cs] >= thr, e1z_ref[h, js, cs] * e0, 0.0)
                rs = slice(ii * nk + j0, ii * nk + j0 + gate_rows)
                w_ref[rs, cs] = (_gelu(a_ref[rs, cs]) * gate).astype(_BF16)

    per = ni // halves
    for half in range(halves):
        for ii in range(half * per, (half + 1) * per):
            gate_block(ii)
        es = slice(half * per * nk, (half + 1) * per * nk)
        o_ref[...] += jnp.dot(vt_ref[:, es], w_ref[es, :], preferred_element_type=_F32)
    a_ref[...] = _dot_nt(u_ref[...], hb_ref[...])


def _peer_dense(hb, u, v, thr, e0, s1, e1z, n_heads, tt=512, ni=4):
    t, d = hb.shape
    nk = s1.shape[1]
    tt = min(tt, t)
    et = ni * nk
    n_tiles = nk // ni
    n_work = (t // tt) * n_tiles
    vt = v.reshape(n_tiles, et, d).transpose(0, 2, 1)

    def stage(lag):
        return lambda k: jnp.clip(k - lag, 0, n_work - 1)

    once = pl.Buffered(1)
    gate_spec = pl.BlockSpec((n_heads, nk, tt), lambda k: (0, 0, stage(1)(k) // n_tiles),
                             pipeline_mode=once)
    row_spec = pl.BlockSpec((n_heads, None, _SUBLANES, tt),
                            lambda k: (0, stage(1)(k) % n_tiles, 0, stage(1)(k) // n_tiles))
    return pl.pallas_call(
        functools.partial(_peer_kernel, ni=ni, n_heads=n_heads, n_tiles=n_tiles, n_work=n_work),
        grid=(n_work + 1,),
        in_specs=[pl.BlockSpec((tt, d), lambda k: (stage(0)(k) // n_tiles, 0), pipeline_mode=once),
                  pl.BlockSpec((et, d), lambda k: (stage(0)(k) % n_tiles, 0)),
                  pl.BlockSpec((None, d, et), lambda k: (stage(1)(k) % n_tiles, 0, 0)),
                  row_spec, row_spec, gate_spec, gate_spec],
        out_specs=pl.BlockSpec((d, tt), lambda k: (0, stage(1)(k) // n_tiles), pipeline_mode=once),
        out_shape=jax.ShapeDtypeStruct((d, t), _F32),
        scratch_shapes=[pltpu.VMEM((et, tt), _F32), pltpu.VMEM((et, tt), _BF16)],
        compiler_params=_params(("arbitrary",)),
        name="peer_dense",
    )(hb, u, vt, thr, e0, s1, e1z)


def _peer_ffn(hb, w_q, sub_keys, u, v):
    half = sub_keys.shape[2]
    n_heads = w_q.shape[1] // (2 * half)
    q = _matmul(hb, w_q, _BF16)
    ni = 8
    thr, e0, s1, e1z = _peer_gate(q, sub_keys.astype(_BF16), n_heads, ni)
    return _peer_dense(hb, u.astype(_BF16), v.astype(_BF16), thr, e0, s1, e1z, n_heads, ni=ni)


def kernel(x, a_w_in, a_w_out, b_w_in, b_f_bias, b_w_out, peer_w_q, peer_sub_keys, peer_u, peer_v,
           ln_mix_g, ln_mix_b, ln_ffn_g, ln_ffn_b):
    b, s, d = x.shape
    depth = ln_mix_g.shape[0]
    alpha = (2 * depth) ** 0.25
    h = x.reshape(b * s, d)
    hb = h.astype(_BF16)
    for i in range(depth):
        j = i // 2
        if i % 2 == 0:
            mix = _dilated_mixer(h, hb, a_w_in[j], a_w_out[j], b, s)
        else:
            mix = _forgetting_mixer(hb, b_w_in[j], b_f_bias[j], b_w_out[j], b, s)
        h, hb = _ln_residual(h, mix, ln_mix_g[i], ln_mix_b[i], alpha)
        ffn_t = _peer_ffn(hb, peer_w_q[i], peer_sub_keys[i], peer_u[i], peer_v[i])
        h, hb = _ln_residual(h, ffn_t, ln_ffn_g[i], ln_ffn_b[i], alpha, transposed_mix=True)
    return h.reshape(b, s, d)
```

```python
import functools
import math

import jax
import jax.numpy as jnp
from jax import lax
from jax.experimental import pallas as pl
from jax.experimental.pallas import tpu as pltpu

_F32 = jnp.float32
_BF16 = jnp.bfloat16

_HEAD_DIM = 128
_DIL_WINDOWS = (128, 512, 2048)
_DIL_RATES = (1, 4, 16)
_PEER_TOPK = 16
_LN_EPS = 1e-5
_NEG = -1e30
_LANES = 128
_SUBLANES = 8
_LOG2E = math.log2(math.e)
_VMEM_LIMIT = 60 * 1024 * 1024


def _dot_nt(a, b):
    return lax.dot_general(a, b, (((1,), (1,)), ((), ())), preferred_element_type=_F32)


def _params(sem):
    return pltpu.CompilerParams(dimension_semantics=sem, vmem_limit_bytes=_VMEM_LIMIT)


def _mm_kernel(a_ref, w_ref, *rest, scaled):
    if scaled:
        s_ref, o_ref, wb_ref = rest
    else:
        o_ref, wb_ref = rest

    @pl.when(pl.program_id(1) == 0)
    def _():
        w = w_ref[...]
        if scaled:
            w = w * s_ref[...]
        wb_ref[...] = w.astype(wb_ref.dtype)

    o_ref[...] = jnp.dot(a_ref[...], wb_ref[...], preferred_element_type=_F32).astype(o_ref.dtype)


def _matmul(a, w, out_dtype, tm=1024, tn=512, col_start=0, n_cols=None, col_scale=None):
    m, k = a.shape
    n = w.shape[1] if n_cols is None else n_cols
    tm = min(tm, m)
    tn = min(tn, n)
    while n % tn or col_start % tn:
        tn -= _LANES
    assert m % tm == 0 and tn > 0
    first = col_start // tn
    w_spec = pl.BlockSpec((k, tn), lambda j, i: (0, j + first))
    operands = [a, w]
    in_specs = [pl.BlockSpec((tm, k), lambda j, i: (i, 0)), w_spec]
    if col_scale is not None:
        operands.append(col_scale.reshape(1, -1).astype(_F32))
        in_specs.append(pl.BlockSpec((1, tn), lambda j, i: (0, j + first)))
    return pl.pallas_call(
        functools.partial(_mm_kernel, scaled=col_scale is not None),
        grid=(n // tn, m // tm),
        in_specs=in_specs,
        out_specs=pl.BlockSpec((tm, tn), lambda j, i: (i, j)),
        out_shape=jax.ShapeDtypeStruct((m, n), out_dtype),
        scratch_shapes=[pltpu.VMEM((k, tn), _BF16)],
        compiler_params=_params(("parallel", "arbitrary")),
        name="matmul",
    )(*operands)


def _ln_body(y, g_ref, b_ref, o_ref, ob_ref):
    mu = jnp.mean(y, axis=-1, keepdims=True)
    d = y - mu
    var = jnp.mean(d * d, axis=-1, keepdims=True)
    out = d * lax.rsqrt(var + _LN_EPS) * g_ref[...] + b_ref[...]
    o_ref[...] = out
    ob_ref[...] = out.astype(_BF16)


def _ln_kernel(h_ref, m_ref, g_ref, b_ref, o_ref, ob_ref, *, alpha):
    _ln_body(alpha * h_ref[...] + m_ref[...], g_ref, b_ref, o_ref, ob_ref)


def _ln_t_kernel(h_ref, mt_ref, g_ref, b_ref, o_ref, ob_ref, *, alpha):
    _ln_body(alpha * h_ref[...] + mt_ref[...].T, g_ref, b_ref, o_ref, ob_ref)


def _ln_residual(h, mix, g, b, alpha, transposed_mix=False, tr=256):
    t, d = h.shape
    tr = min(tr, t)
    assert t % tr == 0
    if transposed_mix:
        body = _ln_t_kernel
        mix_spec = pl.BlockSpec((d, tr), lambda i: (0, i))
    else:
        body = _ln_kernel
        mix_spec = pl.BlockSpec((tr, d), lambda i: (i, 0))
    row = pl.BlockSpec((tr, d), lambda i: (i, 0))
    vec = pl.BlockSpec((1, d), lambda i: (0, 0))
    return pl.pallas_call(
        functools.partial(body, alpha=alpha),
        grid=(t // tr,),
        in_specs=[row, mix_spec, vec, vec],
        out_specs=[row, row],
        out_shape=[jax.ShapeDtypeStruct((t, d), _F32), jax.ShapeDtypeStruct((t, d), _BF16)],
        compiler_params=_params(("parallel",)),
        name="ln_residual",
    )(h, mix, g.reshape(1, d), b.reshape(1, d))


def _dil_kernel(q_ref, kc_ref, kp_ref, vc_ref, vp_ref, o_ref, lse_ref, *, dil, n_heads):
    n = _LANES
    hd = _HEAD_DIM
    has_prev = pl.program_id(1) > 0
    row = lax.broadcasted_iota(jnp.int32, (n, 2 * n), 0)
    col = lax.broadcasted_iota(jnp.int32, (n, 2 * n), 1)
    dist = n + row - col
    valid = (dist >= 0) & (dist <= n) & ((col >= n) | has_prev)
    key_pos = (lax.broadcasted_iota(jnp.int32, (1, 2 * n), 1) - n).astype(_F32)
    query_pos = lax.broadcasted_iota(jnp.int32, (n, 1), 0).astype(_F32)
    lane = lax.broadcasted_iota(jnp.int32, (n, n), 1)
    ones = jnp.ones((2 * n, hd), _BF16)
    lse_tile = jnp.zeros((n, n), _F32)
    group = 4
    for h0 in range(0, n_heads, group):
        heads = range(h0, min(h0 + group, n_heads))
        slopes = [float(dil) * _LOG2E * 2.0 ** (-8.0 * (h + 1) / n_heads) for h in heads]
        sls = [slice(h * hd, (h + 1) * hd) for h in heads]
        logits = []
        for sl, c in zip(sls, slopes):
            k2 = jnp.concatenate([kp_ref[0, :, sl], kc_ref[0, :, sl]], axis=0)
            logits.append(jnp.where(valid, _dot_nt(q_ref[0, :, sl], k2) + c * key_pos, _NEG))
        probs, maxes = [], []
        for s in logits:
            mx = jnp.max(s, axis=1, keepdims=True)
            maxes.append(mx)
            probs.append(jnp.exp2(s - mx).astype(_BF16))
        for h, sl, c, p, mx in zip(heads, sls, slopes, probs, maxes):
            v2 = jnp.concatenate([vp_ref[0, :, sl], vc_ref[0, :, sl]], axis=0)
            r = jnp.dot(p, jnp.concatenate([v2, ones], axis=1), preferred_element_type=_F32)
            den = r[:, hd:hd + 1]
            o_ref[0, :, sl] = r[:, :hd] / den
            lse = (mx + jnp.log2(den) - c * query_pos) * (1.0 / _LOG2E)
            lse_tile = jnp.where(lane == h, lse, lse_tile)
    lse_ref[0] = lse_tile


def _deinterleave_kernel(x_ref, *o_refs, dils):
    s = x_ref.shape[1]
    for o_ref, dil in zip(o_refs, dils):
        l = s // dil
        for r in range(dil):
            o_ref[0, r * l:(r + 1) * l, :] = x_ref[0, pl.ds(r, l, stride=dil), :].astype(o_ref.dtype)


def _deinterleave(h3, dils, cw=_LANES):
    b, s, d = h3.shape
    cw = min(cw, d)
    spec = pl.BlockSpec((1, s, cw), lambda bi, ci: (bi, 0, ci))
    return pl.pallas_call(
        functools.partial(_deinterleave_kernel, dils=dils),
        grid=(b, d // cw),
        in_specs=[spec],
        out_specs=[spec] * len(dils),
        out_shape=[jax.ShapeDtypeStruct((b, s, d), _BF16)] * len(dils),
        compiler_params=_params(("parallel", "parallel")),
        name="deinterleave",
    )(h3)


def _dilated_group(proj, g, n_heads, b, s):
    w = n_heads * _HEAD_DIM
    dil = _DIL_RATES[g]
    n = _DIL_WINDOWS[g] // dil
    assert n == _LANES and n_heads <= _LANES
    l = s // dil
    nb = l // n
    assert l % n == 0
    view = proj.reshape(b * dil, l, 3 * w)

    def spec(which, prev):
        def index(p, i):
            return (p, jnp.maximum(i - 1, 0) if prev else i, which)
        return pl.BlockSpec((1, n, w), index)

    return pl.pallas_call(
        functools.partial(_dil_kernel, dil=dil, n_heads=n_heads),
        grid=(b * dil, nb),
        in_specs=[spec(0, False), spec(1, False), spec(1, True), spec(2, False), spec(2, True)],
        out_specs=[pl.BlockSpec((1, n, w), lambda p, i: (p, i, 0)),
                   pl.BlockSpec((1, n, _LANES), lambda p, i: (p, i, 0))],
        out_shape=[jax.ShapeDtypeStruct((b * dil, l, w), _F32),
                   jax.ShapeDtypeStruct((b * dil, l, _LANES), _F32)],
        compiler_params=_params(("parallel", "parallel")),
        name=f"dilated_attention_g{g}",
    )(view, view, view, view, view)


def _merge_kernel(*refs, dils, n_heads):
    ng = len(dils)
    o_refs, l_refs, out_ref = refs[:ng], refs[ng:2 * ng], refs[2 * ng]
    scratch = iter(refs[2 * ng + 1:])
    tr = out_ref.shape[1]
    outs, lses = [], []
    for o_ref, l_ref, dil in zip(o_refs, l_refs, dils):
        if dil == 1:
            outs.append(o_ref.at[0, 0])
            lses.append(l_ref[0, 0])
            continue
        o_scr, l_scr = next(scratch), next(scratch)
        rows = tr // dil
        for r in range(dil):
            dst = pl.ds(r, rows, stride=dil)
            l_scr[dst, :] = l_ref[0, r]
            for h in range(n_heads):
                o_scr[h, dst, :] = o_ref[0, r, :, h * _HEAD_DIM:(h + 1) * _HEAD_DIM]
        outs.append(o_scr)
        lses.append(l_scr[...])
    for h in range(n_heads):
        sl = slice(h * _HEAD_DIM, (h + 1) * _HEAD_DIM)
        lh = [x[:, h:h + 1] for x in lses]
        mx = functools.reduce(jnp.maximum, lh)
        ws = [jnp.exp(x - mx) for x in lh]
        num = sum(wg * (o[:, sl] if dil == 1 else o[h]) for wg, o, dil in zip(ws, outs, dils))
        out_ref[0, :, sl] = (num / sum(ws)).astype(out_ref.dtype)


def _merge_groups(outs, lses, dils, n_heads, b, s, tr=256):
    w = outs[0].shape[-1]
    tr = min(tr, s)
    assert all(tr % (dil * _SUBLANES) == 0 for dil in dils)

    def spec(dil, width):
        return pl.BlockSpec((1, dil, tr // dil, width), lambda bi, i: (bi, 0, i, 0))

    view = lambda x, dil: x.reshape(b, dil, s // dil, x.shape[-1])
    scratch = []
    for dil in dils:
        if dil != 1:
            scratch += [pltpu.VMEM((n_heads, tr, _HEAD_DIM), _F32), pltpu.VMEM((tr, _LANES), _F32)]
    return pl.pallas_call(
        functools.partial(_merge_kernel, dils=dils, n_heads=n_heads),
        grid=(b, s // tr),
        in_specs=[spec(dil, w) for dil in dils] + [spec(dil, _LANES) for dil in dils],
        out_specs=pl.BlockSpec((1, tr, w), lambda bi, i: (bi, i, 0)),
        out_shape=jax.ShapeDtypeStruct((b, s, w), _BF16),
        scratch_shapes=scratch,
        compiler_params=_params(("parallel", "parallel")),
        name="dilated_merge",
    )(*[view(o, dil) for o, dil in zip(outs, dils)], *[view(x, dil) for x, dil in zip(lses, dils)])


def _dilated_mixer(h, hb, w_in, w_out, b, s):
    d = hb.shape[1]
    n_heads = d // _HEAD_DIM // 2
    w = n_heads * _HEAD_DIM
    strided = tuple(dil for dil in _DIL_RATES if dil != 1)
    copies = dict(zip(strided, _deinterleave(h.reshape(b, s, d), strided)))
    col_scale = jnp.ones((len(_DIL_RATES), 3, w), _F32).at[:, 0].set(_LOG2E / math.sqrt(_HEAD_DIM))
    col_scale = col_scale.reshape(-1)
    outs, lses = [], []
    for g, dil in enumerate(_DIL_RATES):
        rows = hb if dil == 1 else copies[dil].reshape(b * s, d)
        proj = _matmul(rows, w_in, _BF16, col_start=g * 3 * w, n_cols=3 * w, col_scale=col_scale)
        o, lse = _dilated_group(proj, g, n_heads, b, s)
        outs.append(o)
        lses.append(lse)
    merged = _merge_groups(outs, lses, _DIL_RATES, n_heads, b, s)
    return _matmul(merged.reshape(b * s, w), w_out, _F32)


def _fox_gate_kernel(wf_ref, h_ref, bias_ref, c_ref, carry_ref):
    @pl.when(pl.program_id(1) == 0)
    def _():
        carry_ref[...] = jnp.zeros_like(carry_ref)

    z = _dot_nt(wf_ref[...], h_ref[0]) + bias_ref[...]
    log_f = jnp.minimum(z, 0.0) - jnp.log1p(jnp.exp(-jnp.abs(z)))
    log_f = log_f * _LOG2E
    nh, ts = log_f.shape
    lane = lax.broadcasted_iota(jnp.int32, (nh, _LANES), 1)
    carry = carry_ref[...]
    for c in range(ts // _LANES):
        x = log_f[:, c * _LANES:(c + 1) * _LANES]
        shift = 1
        while shift < _LANES:
            x = x + jnp.where(lane >= shift, pltpu.roll(x, shift, 1), 0.0)
            shift *= 2
        x = x + carry
        c_ref[0, :, c * _LANES:(c + 1) * _LANES] = x
        carry = jnp.broadcast_to(x[:, _LANES - 1:_LANES], (nh, _LANES))
    carry_ref[...] = carry


def _fox_gate(hb3, wf_t, bias, ts=512):
    b, s, d = hb3.shape
    nh = wf_t.shape[0]
    ts = min(ts, s)
    return pl.pallas_call(
        _fox_gate_kernel,
        grid=(b, s // ts),
        in_specs=[pl.BlockSpec((nh, d), lambda bi, si: (0, 0)),
                  pl.BlockSpec((1, ts, d), lambda bi, si: (bi, si, 0)),
                  pl.BlockSpec((nh, 1), lambda bi, si: (0, 0))],
        out_specs=pl.BlockSpec((1, nh, ts), lambda bi, si: (bi, 0, si)),
        out_shape=jax.ShapeDtypeStruct((b, nh, s), _F32),
        scratch_shapes=[pltpu.VMEM((nh, _LANES), _F32)],
        compiler_params=_params(("parallel", "arbitrary")),
        name="fox_gate_cumsum",
    )(wf_t, hb3, bias.reshape(nh, 1).astype(_F32))


def _fox_kernel(q_ref, k_ref, v_ref, c_ref, o_ref, m_ref, acc_ref, *, tq, heads):
    qi = pl.program_id(2)
    hd = _HEAD_DIM
    m_ref[...] = jnp.full_like(m_ref, _NEG)
    acc_ref[...] = jnp.zeros_like(acc_ref)
    ones = jnp.ones((tq, hd), _BF16)

    def step(kb, diagonal):
        k_start = pl.multiple_of(kb * tq, tq)
        sls = [slice(g * hd, (g + 1) * hd) for g in range(heads)]
        logits = []
        for g, sl in enumerate(sls):
            s = _dot_nt(q_ref[0, :, sl], k_ref[0, pl.ds(k_start, tq), sl]) - c_ref[0, g, :, pl.ds(k_start, tq)]
            if diagonal:
                row = lax.broadcasted_iota(jnp.int32, (tq, tq), 0)
                col = lax.broadcasted_iota(jnp.int32, (tq, tq), 1)
                s = jnp.where(col <= row, s, _NEG)
            logits.append(s)
        probs, alphas = [], []
        for g, s in enumerate(logits):
            m_old = m_ref[g]
            m_new = jnp.maximum(m_old, jnp.max(s, axis=1, keepdims=True))
            m_ref[g] = m_new
            alphas.append(jnp.exp2(m_old - m_new))
            probs.append(jnp.exp2(s - jnp.concatenate([m_new] * (tq // hd), axis=1)).astype(_BF16))
        for g, (sl, p, alpha) in enumerate(zip(sls, probs, alphas)):
            v1 = jnp.concatenate([v_ref[0, pl.ds(k_start, tq), sl], ones], axis=1)
            acc_ref[g] = (jnp.concatenate([alpha, alpha], axis=1) * acc_ref[g]
                          + jnp.dot(p, v1, preferred_element_type=_F32))

    lax.fori_loop(0, qi, lambda kb, carry: step(kb, False), None)
    step(qi, True)
    for g in range(heads):
        o_ref[0, :, g * hd:(g + 1) * hd] = (acc_ref[g, :, :hd] / acc_ref[g, :, hd:]).astype(o_ref.dtype)


def _fox_attention(qkv, c, n_heads, tq=512, heads=4):
    b, s, _ = qkv.shape
    tq = min(tq, s)
    heads = min(heads, n_heads)
    assert n_heads % heads == 0 and tq % _HEAD_DIM == 0
    ng = n_heads // heads
    gw = heads * _HEAD_DIM
    return pl.pallas_call(
        functools.partial(_fox_kernel, tq=tq, heads=heads),
        grid=(b, ng, s // tq),
        in_specs=[pl.BlockSpec((1, tq, gw), lambda bi, h, qi: (bi, qi, h)),
                  pl.BlockSpec((1, s, gw), lambda bi, h, qi: (bi, 0, ng + h)),
                  pl.BlockSpec((1, s, gw), lambda bi, h, qi: (bi, 0, 2 * ng + h)),
                  pl.BlockSpec((1, heads, 1, s), lambda bi, h, qi: (bi, h, 0, 0))],
        out_specs=pl.BlockSpec((1, tq, gw), lambda bi, h, qi: (bi, qi, h)),
        out_shape=jax.ShapeDtypeStruct((b, s, n_heads * _HEAD_DIM), _BF16),
        scratch_shapes=[pltpu.VMEM((heads, tq, _HEAD_DIM), _F32),
                        pltpu.VMEM((heads, tq, 2 * _HEAD_DIM), _F32)],
        compiler_params=_params(("parallel", "parallel", "parallel")),
        name="fox_attention",
    )(qkv, qkv, qkv, c.reshape(b, n_heads, 1, s))


def _forgetting_mixer(hb, w_in, f_bias, w_out, b, s):
    d = hb.shape[1]
    n_heads = d // _HEAD_DIM
    width = n_heads * _HEAD_DIM
    q_scale = _LOG2E / math.sqrt(_HEAD_DIM)
    col_scale = jnp.ones((3, width), _F32).at[0].set(q_scale).reshape(-1)
    qkv = _matmul(hb, w_in, _BF16, n_cols=3 * width, col_scale=col_scale).reshape(b, s, 3 * width)
    wf_t = w_in[:, 3 * width:].T.astype(_BF16)
    c = _fox_gate(hb.reshape(b, s, d), wf_t, f_bias)
    o = _fox_attention(qkv, c, n_heads)
    return _matmul(o.reshape(b * s, width), w_out, _F32)


def _sorting_network(n):
    comps = []

    def merge(lo, m, r):
        step = 2 * r
        if step < m:
            merge(lo, m, step)
            merge(lo + r, m, step)
            comps.extend((i, i + r) for i in range(lo + r, lo + m - r, step))
        else:
            comps.append((lo, lo + r))

    def sort(lo, m):
        if m > 1:
            sort(lo, m // 2)
            sort(lo + m // 2, m // 2)
            merge(lo, m, 1)

    sort(0, n)
    return comps


def _pop_columns(cols, count, singles=None):
    r, t = cols[0].shape
    row = lax.broadcasted_iota(jnp.int32, (r, t), 0)
    ninf = jnp.full((r, t), -jnp.inf, _F32)
    cols = list(cols)
    if singles is not None:
        r2 = singles.shape[0]
        row2 = lax.broadcasted_iota(jnp.int32, (r2, t), 0) + r
    vals = []
    for k in range(count):
        head = cols[0]
        m = jnp.max(head, axis=0, keepdims=True)
        if singles is not None:
            m = jnp.maximum(m, jnp.max(singles, axis=0, keepdims=True))
        vals.append(m)
        left = count - 1 - k
        if left == 0:
            break
        none = r + (0 if singles is None else singles.shape[0])
        first = jnp.min(jnp.where(head == m, row, none), axis=0, keepdims=True)
        if singles is not None:
            first = jnp.minimum(first, jnp.min(jnp.where(singles == m, row2, none),
                                               axis=0, keepdims=True))
            singles = jnp.where(row2 == first, -jnp.inf, singles)
        hit = row == first
        depth = min(left, len(cols))
        cols = [jnp.where(hit, cols[d + 1] if d + 1 < len(cols) else ninf, cols[d])
                for d in range(depth)]
    return vals


def _top_values(scores, count):
    n = scores.shape[0]
    stacks = [scores[v * _SUBLANES:(v + 1) * _SUBLANES] for v in range(n // _SUBLANES)]
    for i, j in _sorting_network(len(stacks)):
        stacks[i], stacks[j] = jnp.maximum(stacks[i], stacks[j]), jnp.minimum(stacks[i], stacks[j])
    return _pop_columns(stacks, count)


def _stack_rows(rows_1t, n_rows):
    t = rows_1t[0].shape[1]
    row = lax.broadcasted_iota(jnp.int32, (n_rows, t), 0)
    out = jnp.full((n_rows, t), -jnp.inf, _F32)
    for k, v in enumerate(rows_1t):
        out = jnp.where(row == k, v, out)
    return out


def _store_tile_rows(ref, h, x, ni):
    per = _SUBLANES // ni
    for g in range(x.shape[0] // _SUBLANES):
        grp = x[g * _SUBLANES:(g + 1) * _SUBLANES]
        for j in range(per):
            ref[h, g * per + j] = grp if j == 0 else pltpu.roll(grp, _SUBLANES - j * ni, 0)


def _peer_gate_kernel(q_ref, keys_ref, thr_ref, e0_ref, s1_ref, e1z_ref, *, n_heads, topk, ni):
    half = keys_ref.shape[2]
    tt = q_ref.shape[0]
    k0 = keys_ref[0]
    k1 = keys_ref[1]
    nkeep = topk + 1
    rows = _SUBLANES * (-(-nkeep // _SUBLANES))
    row = lax.broadcasted_iota(jnp.int32, (rows, tt), 0)
    for h in range(n_heads):
        s0 = _dot_nt(k0, q_ref[:, (2 * h) * half:(2 * h + 1) * half])
        s1 = _dot_nt(k1, q_ref[:, (2 * h + 1) * half:(2 * h + 2) * half])
        a_vals = _top_values(s0, nkeep)
        b_vals = _top_values(s1, nkeep)
        a_rows = _stack_rows(a_vals, rows)
        stacks = [jnp.where(row[:_SUBLANES] < nkeep // (j + 1), a_rows[:_SUBLANES] + b_vals[j],
                            -jnp.inf) for j in range(nkeep)]
        assert nkeep // 2 <= _SUBLANES
        singles = jnp.where(row[_SUBLANES:] < nkeep, a_rows[_SUBLANES:] + b_vals[0], -jnp.inf)
        top = _pop_columns(stacks, nkeep, singles)
        z = sum(jnp.exp(tv - top[0]) for tv in top[:topk])
        thr = 0.5 * (top[topk - 1] + top[topk])
        _store_tile_rows(thr_ref, h, thr - s0, ni)
        _store_tile_rows(e0_ref, h, jnp.exp(s0 - a_vals[0]), ni)
        s1_ref[h] = s1
        e1z_ref[h] = jnp.exp(s1 - b_vals[0]) / z


def _peer_gate(q, sub_keys, n_heads, ni, tt=128):
    t = q.shape[0]
    nk, half = sub_keys.shape[1:]
    tt = min(tt, t)
    assert _SUBLANES % ni == 0 and nk % _SUBLANES == 0
    n_tiles = nk // ni
    row_spec = pl.BlockSpec((n_heads, n_tiles, _SUBLANES, tt), lambda i: (0, 0, 0, i))
    row_shape = jax.ShapeDtypeStruct((n_heads, n_tiles, _SUBLANES, t), _F32)
    col_spec = pl.BlockSpec((n_heads, nk, tt), lambda i: (0, 0, i))
    col_shape = jax.ShapeDtypeStruct((n_heads, nk, t), _F32)
    return pl.pallas_call(
        functools.partial(_peer_gate_kernel, n_heads=n_heads, topk=_PEER_TOPK, ni=ni),
        grid=(t // tt,),
        in_specs=[pl.BlockSpec((tt, q.shape[1]), lambda i: (i, 0)),
                  pl.BlockSpec((2, nk, half), lambda i: (0, 0, 0))],
        out_specs=[row_spec, row_spec, col_spec, col_spec],
        out_shape=[row_shape, row_shape, col_shape, col_shape],
        compiler_params=_params(("parallel",)),
        name="peer_gate",
    )(q, sub_keys)


def _gelu(x):
    return 0.5 * x * (1.0 + lax.erf(x * (1.0 / math.sqrt(2.0))))


def _peer_kernel(hb_ref, u_ref, vt_ref, thr_ref, e0_ref, s1_ref, e1z_ref, o_ref, a_ref, w_ref,
                 *, ni, n_heads, n_tiles, n_work):
    k = pl.program_id(0)

    @pl.when(k == 0)
    def _():
        a_ref[...] = jnp.zeros_like(a_ref)

    @pl.when(jnp.clip(k - 1, 0, n_work - 1) % n_tiles == 0)
    def _():
        o_ref[...] = jnp.zeros_like(o_ref)

    nk = s1_ref.shape[1]
    tt = hb_ref.shape[0]
    gate_rows = 16
    halves = 4

    def gate_block(ii):
        for tc in range(tt // _LANES):
            cs = slice(tc * _LANES, (tc + 1) * _LANES)
            for j0 in range(0, nk, gate_rows):
                js = slice(j0, j0 + gate_rows)
                gate = jnp.zeros((gate_rows, _LANES), _F32)
                for h in range(n_heads):
                    thr = thr_ref[h, ii:ii + 1, cs]
                    e0 = e0_ref[h, ii:ii + 1, cs]
                    gate = gate + jnp.where(s1_ref[h, js, cs] >= thr, e1z_ref[h, js, cs] * e0, 0.0)
                rs = slice(ii * nk + j0, ii * nk + j0 + gate_rows)
                w_ref[rs, cs] = (_gelu(a_ref[rs, cs]) * gate).astype(_BF16)

    per = ni // halves
    for half in range(halves):
        for ii in range(half * per, (half + 1) * per):
            gate_block(ii)
        es = slice(half * per * nk, (half + 1) * per * nk)
        o_ref[...] += jnp.dot(vt_ref[:, es], w_ref[es, :], preferred_element_type=_F32)
    a_ref[...] = _dot_nt(u_ref[...], hb_ref[...])


def _peer_dense(hb, u, v, thr, e0, s1, e1z, n_heads, tt=512, ni=4):
    t, d = hb.shape
    nk = s1.shape[1]
    tt = min(tt, t)
    et = ni * nk
    n_tiles = nk // ni
    n_work = (t // tt) * n_tiles
    vt = v.reshape(n_tiles, et, d).transpose(0, 2, 1)

    def stage(lag):
        return lambda k: jnp.clip(k - lag, 0, n_work - 1)

    once = pl.Buffered(1)
    gate_spec = pl.BlockSpec((n_heads, nk, tt), lambda k: (0, 0, stage(1)(k) // n_tiles),
                             pipeline_mode=once)
    row_spec = pl.BlockSpec((n_heads, None, _SUBLANES, tt),
                            lambda k: (0, stage(1)(k) % n_tiles, 0, stage(1)(k) // n_tiles))
    return pl.pallas_call(
        functools.partial(_peer_kernel, ni=ni, n_heads=n_heads, n_tiles=n_tiles, n_work=n_work),
        grid=(n_work + 1,),
        in_specs=[pl.BlockSpec((tt, d), lambda k: (stage(0)(k) // n_tiles, 0), pipeline_mode=once),
                  pl.BlockSpec((et, d), lambda k: (stage(0)(k) % n_tiles, 0)),
                  pl.BlockSpec((None, d, et), lambda k: (stage(1)(k) % n_tiles, 0, 0)),
                  row_spec, row_spec, gate_spec, gate_spec],
        out_specs=pl.BlockSpec((d, tt), lambda k: (0, stage(1)(k) // n_tiles), pipeline_mode=once),
        out_shape=jax.ShapeDtypeStruct((d, t), _F32),
        scratch_shapes=[pltpu.VMEM((et, tt), _F32), pltpu.VMEM((et, tt), _BF16)],
        compiler_params=_params(("arbitrary",)),
        name="peer_dense",
    )(hb, u, vt, thr, e0, s1, e1z)


def _peer_ffn(hb, w_q, sub_keys, u, v):
    half = sub_keys.shape[2]
    n_heads = w_q.shape[1] // (2 * half)
    q = _matmul(hb, w_q, _BF16)
    ni = 8
    thr, e0, s1, e1z = _peer_gate(q, sub_keys.astype(_BF16), n_heads, ni)
    return _peer_dense(hb, u.astype(_BF16), v.astype(_BF16), thr, e0, s1, e1z, n_heads, ni=ni)


def kernel(x, a_w_in, a_w_out, b_w_in, b_f_bias, b_w_out, peer_w_q, peer_sub_keys, peer_u, peer_v,
           ln_mix_g, ln_mix_b, ln_ffn_g, ln_ffn_b):
    b, s, d = x.shape
    depth = ln_mix_g.shape[0]
    alpha = (2 * depth) ** 0.25
    h = x.reshape(b * s, d)
    hb = h.astype(_BF16)
    for i in range(depth):
        j = i // 2
        if i % 2 == 0:
            mix = _dilated_mixer(h, hb, a_w_in[j], a_w_out[j], b, s)
        else:
            mix = _forgetting_mixer(hb, b_w_in[j], b_f_bias[j], b_w_out[j], b, s)
        h, hb = _ln_residual(h, mix, ln_mix_g[i], ln_mix_b[i], alpha)
        ffn_t = _peer_ffn(hb, peer_w_q[i], peer_sub_keys[i], peer_u[i], peer_v[i])
        h, hb = _ln_residual(h, ffn_t, ln_ffn_g[i], ln_ffn_b[i], alpha, transposed_mix=True)
    return h.reshape(b, s, d)
```

```python
import functools
import math

import jax
import jax.numpy as jnp
from jax import lax
from jax.experimental import pallas as pl
from jax.experimental.pallas import tpu as pltpu

_F32 = jnp.float32
_BF16 = jnp.bfloat16

_HEAD_DIM = 128
_DIL_WINDOWS = (128, 512, 2048)
_DIL_RATES = (1, 4, 16)
_PEER_TOPK = 16
_LN_EPS = 1e-5
_NEG = -1e30
_LANES = 128
_SUBLANES = 8
_LOG2E = math.log2(math.e)
_VMEM_LIMIT = 60 * 1024 * 1024


def _dot_nt(a, b):
    return lax.dot_general(a, b, (((1,), (1,)), ((), ())), preferred_element_type=_F32)


def _params(sem):
    return pltpu.CompilerParams(dimension_semantics=sem, vmem_limit_bytes=_VMEM_LIMIT)


def _mm_kernel(a_ref, w_ref, *rest, scaled):
    if scaled:
        s_ref, o_ref, wb_ref = rest
    else:
        o_ref, wb_ref = rest

    @pl.when(pl.program_id(1) == 0)
    def _():
        w = w_ref[...]
        if scaled:
            w = w * s_ref[...]
        wb_ref[...] = w.astype(wb_ref.dtype)

    o_ref[...] = jnp.dot(a_ref[...], wb_ref[...], preferred_element_type=_F32).astype(o_ref.dtype)


def _matmul(a, w, layer, out_dtype, tm=1024, tn=512, col_start=0, n_cols=None, col_scale=None):
    m, k = a.shape
    n = w.shape[2] if n_cols is None else n_cols
    tm = min(tm, m)
    tn = min(tn, n)
    while n % tn or col_start % tn:
        tn -= _LANES
    assert m % tm == 0 and tn > 0
    first = col_start // tn
    w_spec = pl.BlockSpec((None, k, tn), lambda j, i: (layer, 0, j + first))
    operands = [a, w]
    in_specs = [pl.BlockSpec((tm, k), lambda j, i: (i, 0)), w_spec]
    if col_scale is not None:
        operands.append(col_scale.reshape(1, -1).astype(_F32))
        in_specs.append(pl.BlockSpec((1, tn), lambda j, i: (0, j + first)))
    return pl.pallas_call(
        functools.partial(_mm_kernel, scaled=col_scale is not None),
        grid=(n // tn, m // tm),
        in_specs=in_specs,
        out_specs=pl.BlockSpec((tm, tn), lambda j, i: (i, j)),
        out_shape=jax.ShapeDtypeStruct((m, n), out_dtype),
        scratch_shapes=[pltpu.VMEM((k, tn), _BF16)],
        compiler_params=_params(("parallel", "arbitrary")),
        name="matmul",
    )(*operands)


def _ln_body(y, g_ref, b_ref, o_ref, ob_ref):
    mu = jnp.mean(y, axis=-1, keepdims=True)
    d = y - mu
    var = jnp.mean(d * d, axis=-1, keepdims=True)
    out = d * lax.rsqrt(var + _LN_EPS) * g_ref[...] + b_ref[...]
    o_ref[...] = out
    ob_ref[...] = out.astype(_BF16)


def _ln_kernel(h_ref, m_ref, g_ref, b_ref, o_ref, ob_ref, *, alpha):
    _ln_body(alpha * h_ref[...] + m_ref[...], g_ref, b_ref, o_ref, ob_ref)


def _ln_t_kernel(h_ref, mt_ref, g_ref, b_ref, o_ref, ob_ref, *, alpha):
    _ln_body(alpha * h_ref[...] + mt_ref[...].T, g_ref, b_ref, o_ref, ob_ref)


def _ln_residual(h, mix, g, b, alpha, transposed_mix=False, tr=256):
    t, d = h.shape
    tr = min(tr, t)
    assert t % tr == 0
    if transposed_mix:
        body = _ln_t_kernel
        mix_spec = pl.BlockSpec((d, tr), lambda i: (0, i))
    else:
        body = _ln_kernel
        mix_spec = pl.BlockSpec((tr, d), lambda i: (i, 0))
    row = pl.BlockSpec((tr, d), lambda i: (i, 0))
    vec = pl.BlockSpec((1, d), lambda i: (0, 0))
    return pl.pallas_call(
        functools.partial(body, alpha=alpha),
        grid=(t // tr,),
        in_specs=[row, mix_spec, vec, vec],
        out_specs=[row, row],
        out_shape=[jax.ShapeDtypeStruct((t, d), _F32), jax.ShapeDtypeStruct((t, d), _BF16)],
        compiler_params=_params(("parallel",)),
        name="ln_residual",
    )(h, mix, g.reshape(1, d), b.reshape(1, d))


def _dil_kernel(q_ref, kc_ref, kp_ref, vc_ref, vp_ref, o_ref, lse_ref, *, dil, n_heads):
    n = _LANES
    hd = _HEAD_DIM
    has_prev = pl.program_id(1) > 0
    row = lax.broadcasted_iota(jnp.int32, (n, 2 * n), 0)
    col = lax.broadcasted_iota(jnp.int32, (n, 2 * n), 1)
    dist = n + row - col
    valid = (dist >= 0) & (dist <= n) & ((col >= n) | has_prev)
    key_pos = (lax.broadcasted_iota(jnp.int32, (1, 2 * n), 1) - n).astype(_F32)
    query_pos = lax.broadcasted_iota(jnp.int32, (n, 1), 0).astype(_F32)
    lane = lax.broadcasted_iota(jnp.int32, (n, n), 1)
    ones = jnp.ones((2 * n, hd), _BF16)
    lse_tile = jnp.zeros((n, n), _F32)
    group = 4
    for h0 in range(0, n_heads, group):
        heads = range(h0, min(h0 + group, n_heads))
        slopes = [float(dil) * _LOG2E * 2.0 ** (-8.0 * (h + 1) / n_heads) for h in heads]
        sls = [slice(h * hd, (h + 1) * hd) for h in heads]
        logits = []
        for sl, c in zip(sls, slopes):
            k2 = jnp.concatenate([kp_ref[0, :, sl], kc_ref[0, :, sl]], axis=0)
            logits.append(jnp.where(valid, _dot_nt(q_ref[0, :, sl], k2) + c * key_pos, _NEG))
        probs, maxes = [], []
        for s in logits:
            mx = jnp.max(s, axis=1, keepdims=True)
            maxes.append(mx)
            probs.append(jnp.exp2(s - mx).astype(_BF16))
        for h, sl, c, p, mx in zip(heads, sls, slopes, probs, maxes):
            v2 = jnp.concatenate([vp_ref[0, :, sl], vc_ref[0, :, sl]], axis=0)
            r = jnp.dot(p, jnp.concatenate([v2, ones], axis=1), preferred_element_type=_F32)
            den = r[:, hd:hd + 1]
            o_ref[0, :, sl] = r[:, :hd] / den
            lse = (mx + jnp.log2(den) - c * query_pos) * (1.0 / _LOG2E)
            lse_tile = jnp.where(lane == h, lse, lse_tile)
    lse_ref[0] = lse_tile


def _deinterleave_kernel(x_ref, *o_refs, dils):
    s = x_ref.shape[1]
    for o_ref, dil in zip(o_refs, dils):
        l = s // dil
        for r in range(dil):
            o_ref[0, r * l:(r + 1) * l, :] = x_ref[0, pl.ds(r, l, stride=dil), :].astype(o_ref.dtype)


def _deinterleave(h3, dils, cw=_LANES):
    b, s, d = h3.shape
    cw = min(cw, d)
    spec = pl.BlockSpec((1, s, cw), lambda bi, ci: (bi, 0, ci))
    return pl.pallas_call(
        functools.partial(_deinterleave_kernel, dils=dils),
        grid=(b, d // cw),
        in_specs=[spec],
        out_specs=[spec] * len(dils),
        out_shape=[jax.ShapeDtypeStruct((b, s, d), _BF16)] * len(dils),
        compiler_params=_params(("parallel", "parallel")),
        name="deinterleave",
    )(h3)


def _dilated_group(proj, g, n_heads, b, s):
    w = n_heads * _HEAD_DIM
    dil = _DIL_RATES[g]
    n = _DIL_WINDOWS[g] // dil
    assert n == _LANES and n_heads <= _LANES
    l = s // dil
    nb = l // n
    assert l % n == 0
    view = proj.reshape(b * dil, l, 3 * w)

    def spec(which, prev):
        def index(p, i):
            return (p, jnp.maximum(i - 1, 0) if prev else i, which)
        return pl.BlockSpec((1, n, w), index)

    return pl.pallas_call(
        functools.partial(_dil_kernel, dil=dil, n_heads=n_heads),
        grid=(b * dil, nb),
        in_specs=[spec(0, False), spec(1, False), spec(1, True), spec(2, False), spec(2, True)],
        out_specs=[pl.BlockSpec((1, n, w), lambda p, i: (p, i, 0)),
                   pl.BlockSpec((1, n, _LANES), lambda p, i: (p, i, 0))],
        out_shape=[jax.ShapeDtypeStruct((b * dil, l, w), _F32),
                   jax.ShapeDtypeStruct((b * dil, l, _LANES), _F32)],
        compiler_params=_params(("parallel", "parallel")),
        name=f"dilated_attention_g{g}",
    )(view, view, view, view, view)


def _merge_kernel(*refs, dils, n_heads):
    ng = len(dils)
    o_refs, l_refs, out_ref = refs[:ng], refs[ng:2 * ng], refs[2 * ng]
    scratch = iter(refs[2 * ng + 1:])
    tr = out_ref.shape[1]
    outs, lses = [], []
    for o_ref, l_ref, dil in zip(o_refs, l_refs, dils):
        if dil == 1:
            outs.append(o_ref.at[0, 0])
            lses.append(l_ref[0, 0])
            continue
        o_scr, l_scr = next(scratch), next(scratch)
        rows = tr // dil
        for r in range(dil):
            dst = pl.ds(r, rows, stride=dil)
            l_scr[dst, :] = l_ref[0, r]
            for h in range(n_heads):
                o_scr[h, dst, :] = o_ref[0, r, :, h * _HEAD_DIM:(h + 1) * _HEAD_DIM]
        outs.append(o_scr)
        lses.append(l_scr[...])
    for h in range(n_heads):
        sl = slice(h * _HEAD_DIM, (h + 1) * _HEAD_DIM)
        lh = [x[:, h:h + 1] for x in lses]
        mx = functools.reduce(jnp.maximum, lh)
        ws = [jnp.exp(x - mx) for x in lh]
        num = sum(wg * (o[:, sl] if dil == 1 else o[h]) for wg, o, dil in zip(ws, outs, dils))
        out_ref[0, :, sl] = (num / sum(ws)).astype(out_ref.dtype)


def _merge_groups(outs, lses, dils, n_heads, b, s, tr=256):
    w = outs[0].shape[-1]
    tr = min(tr, s)
    assert all(tr % (dil * _SUBLANES) == 0 for dil in dils)

    def spec(dil, width):
        return pl.BlockSpec((1, dil, tr // dil, width), lambda bi, i: (bi, 0, i, 0))

    view = lambda x, dil: x.reshape(b, dil, s // dil, x.shape[-1])
    scratch = []
    for dil in dils:
        if dil != 1:
            scratch += [pltpu.VMEM((n_heads, tr, _HEAD_DIM), _F32), pltpu.VMEM((tr, _LANES), _F32)]
    return pl.pallas_call(
        functools.partial(_merge_kernel, dils=dils, n_heads=n_heads),
        grid=(b, s // tr),
        in_specs=[spec(dil, w) for dil in dils] + [spec(dil, _LANES) for dil in dils],
        out_specs=pl.BlockSpec((1, tr, w), lambda bi, i: (bi, i, 0)),
        out_shape=jax.ShapeDtypeStruct((b, s, w), _BF16),
        scratch_shapes=scratch,
        compiler_params=_params(("parallel", "parallel")),
        name="dilated_merge",
    )(*[view(o, dil) for o, dil in zip(outs, dils)], *[view(x, dil) for x, dil in zip(lses, dils)])


def _dilated_mixer(h, hb, w_in, w_out, layer, b, s):
    d = h.shape[1]
    n_heads = d // _HEAD_DIM // 2
    w = n_heads * _HEAD_DIM
    wanted = tuple(dil for dil in _DIL_RATES if dil != 1 or hb is None)
    copies = dict(zip(wanted, _deinterleave(h.reshape(b, s, d), wanted)))
    if hb is None:
        hb = copies[1].reshape(b * s, d)
    col_scale = jnp.ones((len(_DIL_RATES), 3, w), _F32).at[:, 0].set(_LOG2E / math.sqrt(_HEAD_DIM))
    col_scale = col_scale.reshape(-1)
    outs, lses = [], []
    for g, dil in enumerate(_DIL_RATES):
        rows = hb if dil == 1 else copies[dil].reshape(b * s, d)
        proj = _matmul(rows, w_in, layer, _BF16, col_start=g * 3 * w, n_cols=3 * w, col_scale=col_scale)
        o, lse = _dilated_group(proj, g, n_heads, b, s)
        outs.append(o)
        lses.append(lse)
    merged = _merge_groups(outs, lses, _DIL_RATES, n_heads, b, s)
    return _matmul(merged.reshape(b * s, w), w_out, layer, _F32)


def _fox_gate_kernel(wf_ref, h_ref, bias_ref, c_ref, carry_ref):
    @pl.when(pl.program_id(1) == 0)
    def _():
        carry_ref[...] = jnp.zeros_like(carry_ref)

    z = _dot_nt(wf_ref[...], h_ref[0]) + bias_ref[...]
    log_f = jnp.minimum(z, 0.0) - jnp.log1p(jnp.exp(-jnp.abs(z)))
    log_f = log_f * _LOG2E
    nh, ts = log_f.shape
    lane = lax.broadcasted_iota(jnp.int32, (nh, _LANES), 1)
    carry = carry_ref[...]
    for c in range(ts // _LANES):
        x = log_f[:, c * _LANES:(c + 1) * _LANES]
        shift = 1
        while shift < _LANES:
            x = x + jnp.where(lane >= shift, pltpu.roll(x, shift, 1), 0.0)
            shift *= 2
        x = x + carry
        c_ref[0, :, c * _LANES:(c + 1) * _LANES] = x
        carry = jnp.broadcast_to(x[:, _LANES - 1:_LANES], (nh, _LANES))
    carry_ref[...] = carry


def _fox_gate(hb3, wf_t, bias, ts=512):
    b, s, d = hb3.shape
    nh = wf_t.shape[0]
    ts = min(ts, s)
    return pl.pallas_call(
        _fox_gate_kernel,
        grid=(b, s // ts),
        in_specs=[pl.BlockSpec((nh, d), lambda bi, si: (0, 0)),
                  pl.BlockSpec((1, ts, d), lambda bi, si: (bi, si, 0)),
                  pl.BlockSpec((nh, 1), lambda bi, si: (0, 0))],
        out_specs=pl.BlockSpec((1, nh, ts), lambda bi, si: (bi, 0, si)),
        out_shape=jax.ShapeDtypeStruct((b, nh, s), _F32),
        scratch_shapes=[pltpu.VMEM((nh, _LANES), _F32)],
        compiler_params=_params(("parallel", "arbitrary")),
        name="fox_gate_cumsum",
    )(wf_t, hb3, bias.reshape(nh, 1).astype(_F32))


def _fox_kernel(q_ref, k_ref, v_ref, c_ref, o_ref, m_ref, acc_ref, *, tq, heads):
    qi = pl.program_id(2)
    hd = _HEAD_DIM
    m_ref[...] = jnp.full_like(m_ref, _NEG)
    acc_ref[...] = jnp.zeros_like(acc_ref)
    ones = jnp.ones((tq, hd), _BF16)

    def step(kb, diagonal):
        k_start = pl.multiple_of(kb * tq, tq)
        sls = [slice(g * hd, (g + 1) * hd) for g in range(heads)]
        logits = []
        for g, sl in enumerate(sls):
            s = _dot_nt(q_ref[0, :, sl], k_ref[0, pl.ds(k_start, tq), sl]) - c_ref[0, g, :, pl.ds(k_start, tq)]
            if diagonal:
                row = lax.broadcasted_iota(jnp.int32, (tq, tq), 0)
                col = lax.broadcasted_iota(jnp.int32, (tq, tq), 1)
                s = jnp.where(col <= row, s, _NEG)
            logits.append(s)
        probs, alphas = [], []
        for g, s in enumerate(logits):
            m_old = m_ref[g]
            m_new = jnp.maximum(m_old, jnp.max(s, axis=1, keepdims=True))
            m_ref[g] = m_new
            alphas.append(jnp.exp2(m_old - m_new))
            probs.append(jnp.exp2(s - jnp.concatenate([m_new] * (tq // hd), axis=1)).astype(_BF16))
        for g, (sl, p, alpha) in enumerate(zip(sls, probs, alphas)):
            v1 = jnp.concatenate([v_ref[0, pl.ds(k_start, tq), sl], ones], axis=1)
            acc_ref[g] = (jnp.concatenate([alpha, alpha], axis=1) * acc_ref[g]
                          + jnp.dot(p, v1, preferred_element_type=_F32))

    lax.fori_loop(0, qi, lambda kb, carry: step(kb, False), None)
    step(qi, True)
    for g in range(heads):
        o_ref[0, :, g * hd:(g + 1) * hd] = (acc_ref[g, :, :hd] / acc_ref[g, :, hd:]).astype(o_ref.dtype)


def _fox_attention(qkv, c, n_heads, tq=512, heads=4):
    b, s, _ = qkv.shape
    tq = min(tq, s)
    heads = min(heads, n_heads)
    assert n_heads % heads == 0 and tq % _HEAD_DIM == 0
    ng = n_heads // heads
    gw = heads * _HEAD_DIM
    return pl.pallas_call(
        functools.partial(_fox_kernel, tq=tq, heads=heads),
        grid=(b, ng, s // tq),
        in_specs=[pl.BlockSpec((1, tq, gw), lambda bi, h, qi: (bi, qi, h)),
                  pl.BlockSpec((1, s, gw), lambda bi, h, qi: (bi, 0, ng + h)),
                  pl.BlockSpec((1, s, gw), lambda bi, h, qi: (bi, 0, 2 * ng + h)),
                  pl.BlockSpec((1, heads, 1, s), lambda bi, h, qi: (bi, h, 0, 0))],
        out_specs=pl.BlockSpec((1, tq, gw), lambda bi, h, qi: (bi, qi, h)),
        out_shape=jax.ShapeDtypeStruct((b, s, n_heads * _HEAD_DIM), _BF16),
        scratch_shapes=[pltpu.VMEM((heads, tq, _HEAD_DIM), _F32),
                        pltpu.VMEM((heads, tq, 2 * _HEAD_DIM), _F32)],
        compiler_params=_params(("parallel", "parallel", "parallel")),
        name="fox_attention",
    )(qkv, qkv, qkv, c.reshape(b, n_heads, 1, s))


def _forgetting_mixer(hb, w_in, f_bias, w_out, layer, b, s):
    d = hb.shape[1]
    n_heads = d // _HEAD_DIM
    width = n_heads * _HEAD_DIM
    q_scale = _LOG2E / math.sqrt(_HEAD_DIM)
    col_scale = jnp.ones((3, width), _F32).at[0].set(q_scale).reshape(-1)
    qkv = _matmul(hb, w_in, layer, _BF16, n_cols=3 * width, col_scale=col_scale).reshape(b, s, 3 * width)
    wf_t = w_in[layer, :, 3 * width:].T.astype(_BF16)
    c = _fox_gate(hb.reshape(b, s, d), wf_t, f_bias)
    o = _fox_attention(qkv, c, n_heads)
    return _matmul(o.reshape(b * s, width), w_out, layer, _F32)


def _sorting_network(n):
    comps = []

    def merge(lo, m, r):
        step = 2 * r
        if step < m:
            merge(lo, m, step)
            merge(lo + r, m, step)
            comps.extend((i, i + r) for i in range(lo + r, lo + m - r, step))
        else:
            comps.append((lo, lo + r))

    def sort(lo, m):
        if m > 1:
            sort(lo, m // 2)
            sort(lo + m // 2, m // 2)
            merge(lo, m, 1)

    sort(0, n)
    return comps


def _pop_columns(cols, count, singles=None):
    r, t = cols[0].shape
    row = lax.broadcasted_iota(jnp.int32, (r, t), 0)
    ninf = jnp.full((r, t), -jnp.inf, _F32)
    cols = list(cols)
    if singles is not None:
        r2 = singles.shape[0]
        row2 = lax.broadcasted_iota(jnp.int32, (r2, t), 0) + r
    vals = []
    for k in range(count):
        head = cols[0]
        m = jnp.max(head, axis=0, keepdims=True)
        if singles is not None:
            m = jnp.maximum(m, jnp.max(singles, axis=0, keepdims=True))
        vals.append(m)
        left = count - 1 - k
        if left == 0:
            break
        none = r + (0 if singles is None else singles.shape[0])
        first = jnp.min(jnp.where(head == m, row, none), axis=0, keepdims=True)
        if singles is not None:
            first = jnp.minimum(first, jnp.min(jnp.where(singles == m, row2, none),
                                               axis=0, keepdims=True))
            singles = jnp.where(row2 == first, -jnp.inf, singles)
        hit = row == first
        depth = min(left, len(cols))
        cols = [jnp.where(hit, cols[d + 1] if d + 1 < len(cols) else ninf, cols[d])
                for d in range(depth)]
    return vals


def _top_values(scores, count):
    n = scores.shape[0]
    stacks = [scores[v * _SUBLANES:(v + 1) * _SUBLANES] for v in range(n // _SUBLANES)]
    for i, j in _sorting_network(len(stacks)):
        stacks[i], stacks[j] = jnp.maximum(stacks[i], stacks[j]), jnp.minimum(stacks[i], stacks[j])
    return _pop_columns(stacks, count)


def _stack_rows(rows_1t, n_rows):
    t = rows_1t[0].shape[1]
    row = lax.broadcasted_iota(jnp.int32, (n_rows, t), 0)
    out = jnp.full((n_rows, t), -jnp.inf, _F32)
    for k, v in enumerate(rows_1t):
        out = jnp.where(row == k, v, out)
    return out


def _store_tile_rows(ref, h, x, ni):
    per = _SUBLANES // ni
    for g in range(x.shape[0] // _SUBLANES):
        grp = x[g * _SUBLANES:(g + 1) * _SUBLANES]
        for j in range(per):
            ref[h, g * per + j] = grp if j == 0 else pltpu.roll(grp, _SUBLANES - j * ni, 0)


def _peer_gate_kernel(q_ref, keys_ref, thr_ref, e0_ref, s1_ref, e1z_ref, *, n_heads, topk, ni):
    half = keys_ref.shape[2]
    tt = q_ref.shape[0]
    k0 = keys_ref[0]
    k1 = keys_ref[1]
    nkeep = topk + 1
    rows = _SUBLANES * (-(-nkeep // _SUBLANES))
    row = lax.broadcasted_iota(jnp.int32, (rows, tt), 0)
    for h in range(n_heads):
        s0 = _dot_nt(k0, q_ref[:, (2 * h) * half:(2 * h + 1) * half])
        s1 = _dot_nt(k1, q_ref[:, (2 * h + 1) * half:(2 * h + 2) * half])
        a_vals = _top_values(s0, nkeep)
        b_vals = _top_values(s1, nkeep)
        a_rows = _stack_rows(a_vals, rows)
        stacks = [jnp.where(row[:_SUBLANES] < nkeep // (j + 1), a_rows[:_SUBLANES] + b_vals[j],
                            -jnp.inf) for j in range(nkeep)]
        assert nkeep // 2 <= _SUBLANES
        singles = jnp.where(row[_SUBLANES:] < nkeep, a_rows[_SUBLANES:] + b_vals[0], -jnp.inf)
        top = _pop_columns(stacks, nkeep, singles)
        z = sum(jnp.exp(tv - top[0]) for tv in top[:topk])
        thr = 0.5 * (top[topk - 1] + top[topk])
        _store_tile_rows(thr_ref, h, thr - s0, ni)
        _store_tile_rows(e0_ref, h, jnp.exp(s0 - a_vals[0]), ni)
        s1_ref[h] = s1
        e1z_ref[h] = jnp.exp(s1 - b_vals[0]) / z


def _peer_gate(q, sub_keys, n_heads, ni, tt=128):
    t = q.shape[0]
    nk, half = sub_keys.shape[1:]
    tt = min(tt, t)
    assert _SUBLANES % ni == 0 and nk % _SUBLANES == 0
    n_tiles = nk // ni
    row_spec = pl.BlockSpec((n_heads, n_tiles, _SUBLANES, tt), lambda i: (0, 0, 0, i))
    row_shape = jax.ShapeDtypeStruct((n_heads, n_tiles, _SUBLANES, t), _F32)
    col_spec = pl.BlockSpec((n_heads, nk, tt), lambda i: (0, 0, i))
    col_shape = jax.ShapeDtypeStruct((n_heads, nk, t), _F32)
    return pl.pallas_call(
        functools.partial(_peer_gate_kernel, n_heads=n_heads, topk=_PEER_TOPK, ni=ni),
        grid=(t // tt,),
        in_specs=[pl.BlockSpec((tt, q.shape[1]), lambda i: (i, 0)),
                  pl.BlockSpec((2, nk, half), lambda i: (0, 0, 0))],
        out_specs=[row_spec, row_spec, col_spec, col_spec],
        out_shape=[row_shape, row_shape, col_shape, col_shape],
        compiler_params=_params(("parallel",)),
        name="peer_gate",
    )(q, sub_keys)


def _gelu(x):
    return 0.5 * x * (1.0 + lax.erf(x * (1.0 / math.sqrt(2.0))))


def _peer_kernel(hb_ref, u_ref, vt_ref, thr_ref, e0_ref, s1_ref, e1z_ref, o_ref, a_ref, w_ref,
                 *, ni, n_heads, n_tiles, n_work):
    k = pl.program_id(0)

    @pl.when(k == 0)
    def _():
        a_ref[...] = jnp.zeros_like(a_ref)

    @pl.when(jnp.clip(k - 1, 0, n_work - 1) % n_tiles == 0)
    def _():
        o_ref[...] = jnp.zeros_like(o_ref)

    nk = s1_ref.shape[1]
    tt = hb_ref.shape[0]
    gate_rows = 16
    halves = 4

    def gate_block(ii):
        for tc in range(tt // _LANES):
            cs = slice(tc * _LANES, (tc + 1) * _LANES)
            for j0 in range(0, nk, gate_rows):
                js = slice(j0, j0 + gate_rows)
                gate = jnp.zeros((gate_rows, _LANES), _F32)
                for h in range(n_heads):
                    thr = thr_ref[h, ii:ii + 1, cs]
                    e0 = e0_ref[h, ii:ii + 1, cs]
                    gate = gate + jnp.where(s1_ref[h, js, cs] >= thr, e1z_ref[h, js, cs] * e0, 0.0)
                rs = slice(ii * nk + j0, ii * nk + j0 + gate_rows)
                w_ref[rs, cs] = (_gelu(a_ref[rs, cs]) * gate).astype(_BF16)

    per = ni // halves
    for half in range(halves):
        for ii in range(half * per, (half + 1) * per):
            gate_block(ii)
        es = slice(half * per * nk, (half + 1) * per * nk)
        o_ref[...] += jnp.dot(vt_ref[:, es], w_ref[es, :], preferred_element_type=_F32)
    a_ref[...] = _dot_nt(u_ref[...], hb_ref[...])


def _peer_dense(hb, u, v, thr, e0, s1, e1z, n_heads, tt=512, ni=4):
    t, d = hb.shape
    nk = s1.shape[1]
    tt = min(tt, t)
    et = ni * nk
    n_tiles = nk // ni
    n_work = (t // tt) * n_tiles
    vt = v.reshape(n_tiles, et, d).transpose(0, 2, 1)

    def stage(lag):
        return lambda k: jnp.clip(k - lag, 0, n_work - 1)

    once = pl.Buffered(1)
    gate_spec = pl.BlockSpec((n_heads, nk, tt), lambda k: (0, 0, stage(1)(k) // n_tiles),
                             pipeline_mode=once)
    row_spec = pl.BlockSpec((n_heads, None, _SUBLANES, tt),
                            lambda k: (0, stage(1)(k) % n_tiles, 0, stage(1)(k) // n_tiles))
    return pl.pallas_call(
        functools.partial(_peer_kernel, ni=ni, n_heads=n_heads, n_tiles=n_tiles, n_work=n_work),
        grid=(n_work + 1,),
        in_specs=[pl.BlockSpec((tt, d), lambda k: (stage(0)(k) // n_tiles, 0), pipeline_mode=once),
                  pl.BlockSpec((et, d), lambda k: (stage(0)(k) % n_tiles, 0)),
                  pl.BlockSpec((None, d, et), lambda k: (stage(1)(k) % n_tiles, 0, 0)),
                  row_spec, row_spec, gate_spec, gate_spec],
        out_specs=pl.BlockSpec((d, tt), lambda k: (0, stage(1)(k) // n_tiles), pipeline_mode=once),
        out_shape=jax.ShapeDtypeStruct((d, t), _F32),
        scratch_shapes=[pltpu.VMEM((et, tt), _F32), pltpu.VMEM((et, tt), _BF16)],
        compiler_params=_params(("arbitrary",)),
        name="peer_dense",
    )(hb, u, vt, thr, e0, s1, e1z)


def _peer_ffn(hb, w_q, layer, sub_keys, u, v):
    half = sub_keys.shape[2]
    n_heads = w_q.shape[2] // (2 * half)
    q = _matmul(hb, w_q, layer, _BF16)
    ni = 8
    thr, e0, s1, e1z = _peer_gate(q, sub_keys.astype(_BF16), n_heads, ni)
    return _peer_dense(hb, u.astype(_BF16), v.astype(_BF16), thr, e0, s1, e1z, n_heads, ni=ni)


def kernel(x, a_w_in, a_w_out, b_w_in, b_f_bias, b_w_out, peer_w_q, peer_sub_keys, peer_u, peer_v,
           ln_mix_g, ln_mix_b, ln_ffn_g, ln_ffn_b):
    b, s, d = x.shape
    depth = ln_mix_g.shape[0]
    alpha = (2 * depth) ** 0.25
    h = x.reshape(b * s, d)
    hb = None
    for i in range(depth):
        j = i // 2
        if i % 2 == 0:
            mix = _dilated_mixer(h, hb, a_w_in, a_w_out, j, b, s)
        else:
            mix = _forgetting_mixer(hb, b_w_in, b_f_bias[j], b_w_out, j, b, s)
        h, hb = _ln_residual(h, mix, ln_mix_g[i], ln_mix_b[i], alpha)
        ffn_t = _peer_ffn(hb, peer_w_q, i, peer_sub_keys[i], peer_u[i], peer_v[i])
        h, hb = _ln_residual(h, ffn_t, ln_ffn_g[i], ln_ffn_b[i], alpha, transposed_mix=True)
    return h.reshape(b, s, d)
```

```python
import functools
import math

import jax
import jax.numpy as jnp
from jax import lax
from jax.experimental import pallas as pl
from jax.experimental.pallas import tpu as pltpu

_F32 = jnp.float32
_BF16 = jnp.bfloat16

_HEAD_DIM = 128
_DIL_WINDOWS = (128, 512, 2048)
_DIL_RATES = (1, 4, 16)
_PEER_TOPK = 16
_LN_EPS = 1e-5
_NEG = -1e30
_LANES = 128
_SUBLANES = 8
_LOG2E = math.log2(math.e)
_VMEM_LIMIT = 60 * 1024 * 1024


def _dot_nt(a, b):
    return lax.dot_general(a, b, (((1,), (1,)), ((), ())), preferred_element_type=_F32)


def _params(sem):
    return pltpu.CompilerParams(dimension_semantics=sem, vmem_limit_bytes=_VMEM_LIMIT)


def _mm_kernel(a_ref, w_ref, *rest, scaled):
    if scaled:
        s_ref, o_ref, wb_ref = rest
    else:
        o_ref, wb_ref = rest

    @pl.when(pl.program_id(1) == 0)
    def _():
        w = w_ref[...]
        if scaled:
            w = w * s_ref[...]
        wb_ref[...] = w.astype(wb_ref.dtype)

    o_ref[...] = jnp.dot(a_ref[...], wb_ref[...], preferred_element_type=_F32).astype(o_ref.dtype)


def _matmul(a, w, layer, out_dtype, tm=1024, tn=512, col_start=0, n_cols=None, col_scale=None):
    m, k = a.shape
    n = w.shape[2] if n_cols is None else n_cols
    tm = min(tm, m)
    tn = min(tn, n)
    while n % tn or col_start % tn:
        tn -= _LANES
    assert m % tm == 0 and tn > 0
    first = col_start // tn
    w_spec = pl.BlockSpec((None, k, tn), lambda j, i: (layer, 0, j + first))
    operands = [a, w]
    in_specs = [pl.BlockSpec((tm, k), lambda j, i: (i, 0)), w_spec]
    if col_scale is not None:
        operands.append(col_scale.reshape(1, -1).astype(_F32))
        in_specs.append(pl.BlockSpec((1, tn), lambda j, i: (0, j + first)))
    return pl.pallas_call(
        functools.partial(_mm_kernel, scaled=col_scale is not None),
        grid=(n // tn, m // tm),
        in_specs=in_specs,
        out_specs=pl.BlockSpec((tm, tn), lambda j, i: (i, j)),
        out_shape=jax.ShapeDtypeStruct((m, n), out_dtype),
        scratch_shapes=[pltpu.VMEM((k, tn), _BF16)],
        compiler_params=_params(("parallel", "arbitrary")),
        name="matmul",
    )(*operands)


def _ln_body(y, g_ref, b_ref, o_ref, ob_ref):
    mu = jnp.mean(y, axis=-1, keepdims=True)
    d = y - mu
    var = jnp.mean(d * d, axis=-1, keepdims=True)
    out = d * lax.rsqrt(var + _LN_EPS) * g_ref[...] + b_ref[...]
    o_ref[...] = out
    ob_ref[...] = out.astype(_BF16)


def _ln_kernel(h_ref, m_ref, g_ref, b_ref, o_ref, ob_ref, *, alpha):
    _ln_body(alpha * h_ref[...] + m_ref[...], g_ref, b_ref, o_ref, ob_ref)


def _ln_t_kernel(h_ref, mt_ref, g_ref, b_ref, o_ref, ob_ref, *, alpha):
    _ln_body(alpha * h_ref[...] + mt_ref[...].T, g_ref, b_ref, o_ref, ob_ref)


def _ln_residual(h, mix, g, b, alpha, transposed_mix=False, tr=256):
    t, d = h.shape
    tr = min(tr, t)
    assert t % tr == 0
    if transposed_mix:
        body = _ln_t_kernel
        mix_spec = pl.BlockSpec((d, tr), lambda i: (0, i))
    else:
        body = _ln_kernel
        mix_spec = pl.BlockSpec((tr, d), lambda i: (i, 0))
    row = pl.BlockSpec((tr, d), lambda i: (i, 0))
    vec = pl.BlockSpec((1, d), lambda i: (0, 0))
    return pl.pallas_call(
        functools.partial(body, alpha=alpha),
        grid=(t // tr,),
        in_specs=[row, mix_spec, vec, vec],
        out_specs=[row, row],
        out_shape=[jax.ShapeDtypeStruct((t, d), _F32), jax.ShapeDtypeStruct((t, d), _BF16)],
        compiler_params=_params(("parallel",)),
        name="ln_residual",
    )(h, mix, g.reshape(1, d), b.reshape(1, d))


def _dil_kernel(q_ref, kc_ref, kp_ref, vc_ref, vp_ref, o_ref, lse_ref, *, dil, n_heads):
    n = _LANES
    hd = _HEAD_DIM
    has_prev = pl.program_id(1) > 0
    row = lax.broadcasted_iota(jnp.int32, (n, 2 * n), 0)
    col = lax.broadcasted_iota(jnp.int32, (n, 2 * n), 1)
    dist = n + row - col
    valid = (dist >= 0) & (dist <= n) & ((col >= n) | has_prev)
    key_pos = (lax.broadcasted_iota(jnp.int32, (1, 2 * n), 1) - n).astype(_F32)
    query_pos = lax.broadcasted_iota(jnp.int32, (n, 1), 0).astype(_F32)
    lane = lax.broadcasted_iota(jnp.int32, (n, n), 1)
    ones = jnp.ones((2 * n, hd), _BF16)
    lse_tile = jnp.zeros((n, n), _F32)
    group = 4
    for h0 in range(0, n_heads, group):
        heads = range(h0, min(h0 + group, n_heads))
        slopes = [float(dil) * _LOG2E * 2.0 ** (-8.0 * (h + 1) / n_heads) for h in heads]
        sls = [slice(h * hd, (h + 1) * hd) for h in heads]
        logits = []
        for sl, c in zip(sls, slopes):
            k2 = jnp.concatenate([kp_ref[0, :, sl], kc_ref[0, :, sl]], axis=0)
            logits.append(jnp.where(valid, _dot_nt(q_ref[0, :, sl], k2) + c * key_pos, _NEG))
        probs, maxes = [], []
        for s in logits:
            mx = jnp.max(s, axis=1, keepdims=True)
            maxes.append(mx)
            probs.append(jnp.exp2(s - mx).astype(_BF16))
        for h, sl, c, p, mx in zip(heads, sls, slopes, probs, maxes):
            v2 = jnp.concatenate([vp_ref[0, :, sl], vc_ref[0, :, sl]], axis=0)
            r = jnp.dot(p, jnp.concatenate([v2, ones], axis=1), preferred_element_type=_F32)
            den = r[:, hd:hd + 1]
            o_ref[0, :, sl] = r[:, :hd] / den
            lse = (mx + jnp.log2(den) - c * query_pos) * (1.0 / _LOG2E)
            lse_tile = jnp.where(lane == h, lse, lse_tile)
    lse_ref[0] = lse_tile


def _deinterleave_kernel(x_ref, *o_refs, dils):
    s = x_ref.shape[1]
    for o_ref, dil in zip(o_refs, dils):
        l = s // dil
        for r in range(dil):
            o_ref[0, r * l:(r + 1) * l, :] = x_ref[0, pl.ds(r, l, stride=dil), :].astype(o_ref.dtype)


def _deinterleave(h3, dils, cw=_LANES):
    b, s, d = h3.shape
    cw = min(cw, d)
    spec = pl.BlockSpec((1, s, cw), lambda bi, ci: (bi, 0, ci))
    return pl.pallas_call(
        functools.partial(_deinterleave_kernel, dils=dils),
        grid=(b, d // cw),
        in_specs=[spec],
        out_specs=[spec] * len(dils),
        out_shape=[jax.ShapeDtypeStruct((b, s, d), _BF16)] * len(dils),
        compiler_params=_params(("parallel", "parallel")),
        name="deinterleave",
    )(h3)


def _dilated_group(proj, g, n_heads, b, s):
    w = n_heads * _HEAD_DIM
    dil = _DIL_RATES[g]
    n = _DIL_WINDOWS[g] // dil
    assert n == _LANES and n_heads <= _LANES
    l = s // dil
    nb = l // n
    assert l % n == 0
    view = proj.reshape(b * dil, l, 3 * w)

    def spec(which, prev):
        def index(p, i):
            return (p, jnp.maximum(i - 1, 0) if prev else i, which)
        return pl.BlockSpec((1, n, w), index)

    return pl.pallas_call(
        functools.partial(_dil_kernel, dil=dil, n_heads=n_heads),
        grid=(b * dil, nb),
        in_specs=[spec(0, False), spec(1, False), spec(1, True), spec(2, False), spec(2, True)],
        out_specs=[pl.BlockSpec((1, n, w), lambda p, i: (p, i, 0)),
                   pl.BlockSpec((1, n, _LANES), lambda p, i: (p, i, 0))],
        out_shape=[jax.ShapeDtypeStruct((b * dil, l, w), _F32),
                   jax.ShapeDtypeStruct((b * dil, l, _LANES), _F32)],
        compiler_params=_params(("parallel", "parallel")),
        name=f"dilated_attention_g{g}",
    )(view, view, view, view, view)


def _merge_kernel(*refs, dils, n_heads):
    ng = len(dils)
    o_refs, l_refs, out_ref = refs[:ng], refs[ng:2 * ng], refs[2 * ng]
    scratch = iter(refs[2 * ng + 1:])
    tr = out_ref.shape[1]
    outs, lses = [], []
    for o_ref, l_ref, dil in zip(o_refs, l_refs, dils):
        if dil == 1:
            outs.append(o_ref.at[0, 0])
            lses.append(l_ref[0, 0])
            continue
        o_scr, l_scr = next(scratch), next(scratch)
        rows = tr // dil
        for r in range(dil):
            dst = pl.ds(r, rows, stride=dil)
            l_scr[dst, :] = l_ref[0, r]
            for h in range(n_heads):
                o_scr[h, dst, :] = o_ref[0, r, :, h * _HEAD_DIM:(h + 1) * _HEAD_DIM]
        outs.append(o_scr)
        lses.append(l_scr[...])
    for h in range(n_heads):
        sl = slice(h * _HEAD_DIM, (h + 1) * _HEAD_DIM)
        lh = [x[:, h:h + 1] for x in lses]
        mx = functools.reduce(jnp.maximum, lh)
        ws = [jnp.exp(x - mx) for x in lh]
        num = sum(wg * (o[:, sl] if dil == 1 else o[h]) for wg, o, dil in zip(ws, outs, dils))
        out_ref[0, :, sl] = (num / sum(ws)).astype(out_ref.dtype)


def _merge_groups(outs, lses, dils, n_heads, b, s, tr=256):
    w = outs[0].shape[-1]
    tr = min(tr, s)
    assert all(tr % (dil * _SUBLANES) == 0 for dil in dils)

    def spec(dil, width):
        return pl.BlockSpec((1, dil, tr // dil, width), lambda bi, i: (bi, 0, i, 0))

    view = lambda x, dil: x.reshape(b, dil, s // dil, x.shape[-1])
    scratch = []
    for dil in dils:
        if dil != 1:
            scratch += [pltpu.VMEM((n_heads, tr, _HEAD_DIM), _F32), pltpu.VMEM((tr, _LANES), _F32)]
    return pl.pallas_call(
        functools.partial(_merge_kernel, dils=dils, n_heads=n_heads),
        grid=(b, s // tr),
        in_specs=[spec(dil, w) for dil in dils] + [spec(dil, _LANES) for dil in dils],
        out_specs=pl.BlockSpec((1, tr, w), lambda bi, i: (bi, i, 0)),
        out_shape=jax.ShapeDtypeStruct((b, s, w), _BF16),
        scratch_shapes=scratch,
        compiler_params=_params(("parallel", "parallel")),
        name="dilated_merge",
    )(*[view(o, dil) for o, dil in zip(outs, dils)], *[view(x, dil) for x, dil in zip(lses, dils)])


def _dilated_mixer(h, hb, w_in, w_out, layer, b, s):
    d = h.shape[1]
    n_heads = d // _HEAD_DIM // 2
    w = n_heads * _HEAD_DIM
    wanted = tuple(dil for dil in _DIL_RATES if dil != 1 or hb is None)
    copies = dict(zip(wanted, _deinterleave(h.reshape(b, s, d), wanted)))
    if hb is None:
        hb = copies[1].reshape(b * s, d)
    col_scale = jnp.ones((len(_DIL_RATES), 3, w), _F32).at[:, 0].set(_LOG2E / math.sqrt(_HEAD_DIM))
    col_scale = col_scale.reshape(-1)
    outs, lses = [], []
    for g, dil in enumerate(_DIL_RATES):
        rows = hb if dil == 1 else copies[dil].reshape(b * s, d)
        proj = _matmul(rows, w_in, layer, _BF16, col_start=g * 3 * w, n_cols=3 * w, col_scale=col_scale)
        o, lse = _dilated_group(proj, g, n_heads, b, s)
        outs.append(o)
        lses.append(lse)
    merged = _merge_groups(outs, lses, _DIL_RATES, n_heads, b, s)
    return _matmul(merged.reshape(b * s, w), w_out, layer, _F32)


def _fox_gate_kernel(wf_ref, h_ref, bias_ref, c_ref, carry_ref):
    @pl.when(pl.program_id(1) == 0)
    def _():
        carry_ref[...] = jnp.zeros_like(carry_ref)

    z = _dot_nt(wf_ref[...], h_ref[0]) + bias_ref[...]
    log_f = jnp.minimum(z, 0.0) - jnp.log1p(jnp.exp(-jnp.abs(z)))
    log_f = log_f * _LOG2E
    nh, ts = log_f.shape
    lane = lax.broadcasted_iota(jnp.int32, (nh, _LANES), 1)
    carry = carry_ref[...]
    for c in range(ts // _LANES):
        x = log_f[:, c * _LANES:(c + 1) * _LANES]
        shift = 1
        while shift < _LANES:
            x = x + jnp.where(lane >= shift, pltpu.roll(x, shift, 1), 0.0)
            shift *= 2
        x = x + carry
        c_ref[0, :, c * _LANES:(c + 1) * _LANES] = x
        carry = jnp.broadcast_to(x[:, _LANES - 1:_LANES], (nh, _LANES))
    carry_ref[...] = carry


def _fox_gate(hb3, wf_t, bias, ts=512):
    b, s, d = hb3.shape
    nh = wf_t.shape[0]
    ts = min(ts, s)
    return pl.pallas_call(
        _fox_gate_kernel,
        grid=(b, s // ts),
        in_specs=[pl.BlockSpec((nh, d), lambda bi, si: (0, 0)),
                  pl.BlockSpec((1, ts, d), lambda bi, si: (bi, si, 0)),
                  pl.BlockSpec((nh, 1), lambda bi, si: (0, 0))],
        out_specs=pl.BlockSpec((1, nh, ts), lambda bi, si: (bi, 0, si)),
        out_shape=jax.ShapeDtypeStruct((b, nh, s), _F32),
        scratch_shapes=[pltpu.VMEM((nh, _LANES), _F32)],
        compiler_params=_params(("parallel", "arbitrary")),
        name="fox_gate_cumsum",
    )(wf_t, hb3, bias.reshape(nh, 1).astype(_F32))


def _fox_kernel(q_ref, k_ref, v_ref, c_ref, o_ref, m_ref, acc_ref, *, tq, heads):
    qi = pl.program_id(2)
    hd = _HEAD_DIM
    m_ref[...] = jnp.full_like(m_ref, _NEG)
    acc_ref[...] = jnp.zeros_like(acc_ref)
    ones = jnp.ones((tq, hd), _BF16)

    def step(kb, diagonal):
        k_start = pl.multiple_of(kb * tq, tq)
        sls = [slice(g * hd, (g + 1) * hd) for g in range(heads)]
        logits = []
        for g, sl in enumerate(sls):
            s = _dot_nt(q_ref[0, :, sl], k_ref[0, pl.ds(k_start, tq), sl]) - c_ref[0, g, :, pl.ds(k_start, tq)]
            if diagonal:
                row = lax.broadcasted_iota(jnp.int32, (tq, tq), 0)
                col = lax.broadcasted_iota(jnp.int32, (tq, tq), 1)
                s = jnp.where(col <= row, s, _NEG)
            logits.append(s)
        probs, alphas = [], []
        for g, s in enumerate(logits):
            m_old = m_ref[g]
            m_new = jnp.maximum(m_old, jnp.max(s, axis=1, keepdims=True))
            m_ref[g] = m_new
            alphas.append(jnp.exp2(m_old - m_new))
            probs.append(jnp.exp2(s - jnp.concatenate([m_new] * (tq // hd), axis=1)).astype(_BF16))
        for g, (sl, p, alpha) in enumerate(zip(sls, probs, alphas)):
            v1 = jnp.concatenate([v_ref[0, pl.ds(k_start, tq), sl], ones], axis=1)
            acc_ref[g] = (jnp.concatenate([alpha, alpha], axis=1) * acc_ref[g]
                          + jnp.dot(p, v1, preferred_element_type=_F32))

    lax.fori_loop(0, qi, lambda kb, carry: step(kb, False), None)
    step(qi, True)
    for g in range(heads):
        o_ref[0, :, g * hd:(g + 1) * hd] = (acc_ref[g, :, :hd] / acc_ref[g, :, hd:]).astype(o_ref.dtype)


def _fox_attention(qkv, c, n_heads, tq=512, heads=4):
    b, s, _ = qkv.shape
    tq = min(tq, s)
    heads = min(heads, n_heads)
    assert n_heads % heads == 0 and tq % _HEAD_DIM == 0
    ng = n_heads // heads
    gw = heads * _HEAD_DIM
    return pl.pallas_call(
        functools.partial(_fox_kernel, tq=tq, heads=heads),
        grid=(b, ng, s // tq),
        in_specs=[pl.BlockSpec((1, tq, gw), lambda bi, h, qi: (bi, qi, h)),
                  pl.BlockSpec((1, s, gw), lambda bi, h, qi: (bi, 0, ng + h)),
                  pl.BlockSpec((1, s, gw), lambda bi, h, qi: (bi, 0, 2 * ng + h)),
                  pl.BlockSpec((1, heads, 1, s), lambda bi, h, qi: (bi, h, 0, 0))],
        out_specs=pl.BlockSpec((1, tq, gw), lambda bi, h, qi: (bi, qi, h)),
        out_shape=jax.ShapeDtypeStruct((b, s, n_heads * _HEAD_DIM), _BF16),
        scratch_shapes=[pltpu.VMEM((heads, tq, _HEAD_DIM), _F32),
                        pltpu.VMEM((heads, tq, 2 * _HEAD_DIM), _F32)],
        compiler_params=_params(("parallel", "parallel", "parallel")),
        name="fox_attention",
    )(qkv, qkv, qkv, c.reshape(b, n_heads, 1, s))


def _forgetting_mixer(hb, w_in, f_bias, w_out, layer, b, s):
    d = hb.shape[1]
    n_heads = d // _HEAD_DIM
    width = n_heads * _HEAD_DIM
    q_scale = _LOG2E / math.sqrt(_HEAD_DIM)
    col_scale = jnp.ones((3, width), _F32).at[0].set(q_scale).reshape(-1)
    qkv = _matmul(hb, w_in, layer, _BF16, n_cols=3 * width, col_scale=col_scale).reshape(b, s, 3 * width)
    wf_t = w_in[layer, :, 3 * width:].T.astype(_BF16)
    c = _fox_gate(hb.reshape(b, s, d), wf_t, f_bias)
    o = _fox_attention(qkv, c, n_heads)
    return _matmul(o.reshape(b * s, width), w_out, layer, _F32)


def _sorting_network(n):
    comps = []

    def merge(lo, m, r):
        step = 2 * r
        if step < m:
            merge(lo, m, step)
            merge(lo + r, m, step)
            comps.extend((i, i + r) for i in range(lo + r, lo + m - r, step))
        else:
            comps.append((lo, lo + r))

    def sort(lo, m):
        if m > 1:
            sort(lo, m // 2)
            sort(lo + m // 2, m // 2)
            merge(lo, m, 1)

    sort(0, n)
    return comps


def _pop_columns(cols, count, singles=None):
    r, t = cols[0].shape
    row = lax.broadcasted_iota(jnp.int32, (r, t), 0)
    ninf = jnp.full((r, t), -jnp.inf, _F32)
    cols = list(cols)
    if singles is not None:
        r2 = singles.shape[0]
        row2 = lax.broadcasted_iota(jnp.int32, (r2, t), 0) + r
    vals = []
    for k in range(count):
        head = cols[0]
        m = jnp.max(head, axis=0, keepdims=True)
        if singles is not None:
            m = jnp.maximum(m, jnp.max(singles, axis=0, keepdims=True))
        vals.append(m)
        left = count - 1 - k
        if left == 0:
            break
        none = r + (0 if singles is None else singles.shape[0])
        first = jnp.min(jnp.where(head == m, row, none), axis=0, keepdims=True)
        if singles is not None:
            first = jnp.minimum(first, jnp.min(jnp.where(singles == m, row2, none),
                                               axis=0, keepdims=True))
            singles = jnp.where(row2 == first, -jnp.inf, singles)
        hit = row == first
        depth = min(left, len(cols))
        cols = [jnp.where(hit, cols[d + 1] if d + 1 < len(cols) else ninf, cols[d])
                for d in range(depth)]
    return vals


def _top_values(scores, count):
    n = scores.shape[0]
    stacks = [scores[v * _SUBLANES:(v + 1) * _SUBLANES] for v in range(n // _SUBLANES)]
    for i, j in _sorting_network(len(stacks)):
        stacks[i], stacks[j] = jnp.maximum(stacks[i], stacks[j]), jnp.minimum(stacks[i], stacks[j])
    return _pop_columns(stacks, count)


def _stack_rows(rows_1t, n_rows):
    t = rows_1t[0].shape[1]
    row = lax.broadcasted_iota(jnp.int32, (n_rows, t), 0)
    out = jnp.full((n_rows, t), -jnp.inf, _F32)
    for k, v in enumerate(rows_1t):
        out = jnp.where(row == k, v, out)
    return out


def _store_tile_rows(ref, h, x, ni):
    per = _SUBLANES // ni
    for g in range(x.shape[0] // _SUBLANES):
        grp = x[g * _SUBLANES:(g + 1) * _SUBLANES]
        for j in range(per):
            ref[h, g * per + j] = grp if j == 0 else pltpu.roll(grp, _SUBLANES - j * ni, 0)


def _peer_gate_kernel(q_ref, keys_ref, thr_ref, e0_ref, s1_ref, e1z_ref, *, n_heads, topk, ni):
    half = keys_ref.shape[2]
    tt = q_ref.shape[0]
    k0 = keys_ref[0]
    k1 = keys_ref[1]
    nkeep = topk + 1
    rows = _SUBLANES * (-(-nkeep // _SUBLANES))
    row = lax.broadcasted_iota(jnp.int32, (rows, tt), 0)
    for h in range(n_heads):
        s0 = _dot_nt(k0, q_ref[:, (2 * h) * half:(2 * h + 1) * half])
        s1 = _dot_nt(k1, q_ref[:, (2 * h + 1) * half:(2 * h + 2) * half])
        a_vals = _top_values(s0, nkeep)
        b_vals = _top_values(s1, nkeep)
        a_rows = _stack_rows(a_vals, rows)
        stacks = [jnp.where(row[:_SUBLANES] < nkeep // (j + 1), a_rows[:_SUBLANES] + b_vals[j],
                            -jnp.inf) for j in range(nkeep)]
        assert nkeep // 2 <= _SUBLANES
        singles = jnp.where(row[_SUBLANES:] < nkeep, a_rows[_SUBLANES:] + b_vals[0], -jnp.inf)
        top = _pop_columns(stacks, nkeep, singles)
        z = sum(jnp.exp(tv - top[0]) for tv in top[:topk])
        thr = 0.5 * (top[topk - 1] + top[topk])
        _store_tile_rows(thr_ref, h, thr - s0, ni)
        _store_tile_rows(e0_ref, h, jnp.exp(s0 - a_vals[0]), ni)
        s1_ref[h] = s1
        e1z_ref[h] = jnp.exp(s1 - b_vals[0]) / z


def _peer_gate(q, sub_keys, n_heads, ni, tt=128):
    t = q.shape[0]
    nk, half = sub_keys.shape[1:]
    tt = min(tt, t)
    assert _SUBLANES % ni == 0 and nk % _SUBLANES == 0
    n_tiles = nk // ni
    row_spec = pl.BlockSpec((n_heads, n_tiles, _SUBLANES, tt), lambda i: (0, 0, 0, i))
    row_shape = jax.ShapeDtypeStruct((n_heads, n_tiles, _SUBLANES, t), _F32)
    col_spec = pl.BlockSpec((n_heads, nk, tt), lambda i: (0, 0, i))
    col_shape = jax.ShapeDtypeStruct((n_heads, nk, t), _F32)
    return pl.pallas_call(
        functools.partial(_peer_gate_kernel, n_heads=n_heads, topk=_PEER_TOPK, ni=ni),
        grid=(t // tt,),
        in_specs=[pl.BlockSpec((tt, q.shape[1]), lambda i: (i, 0)),
                  pl.BlockSpec((2, nk, half), lambda i: (0, 0, 0))],
        out_specs=[row_spec, row_spec, col_spec, col_spec],
        out_shape=[row_shape, row_shape, col_shape, col_shape],
        compiler_params=_params(("parallel",)),
        name="peer_gate",
    )(q, sub_keys)


def _gelu(x):
    return 0.5 * x * (1.0 + lax.erf(x * (1.0 / math.sqrt(2.0))))


def _peer_kernel(hb_ref, u_ref, vt_ref, thr_ref, e0_ref, s1_ref, e1z_ref, o_ref, a_ref, w_ref,
                 *, ni, n_heads, n_tiles, n_work):
    k = pl.program_id(0)

    @pl.when(k == 0)
    def _():
        a_ref[...] = jnp.zeros_like(a_ref)

    @pl.when(jnp.clip(k - 1, 0, n_work - 1) % n_tiles == 0)
    def _():
        o_ref[...] = jnp.zeros_like(o_ref)

    nk = s1_ref.shape[1]
    tt = hb_ref.shape[0]
    gate_rows = 16
    halves = 4

    def gate_block(ii):
        for tc in range(tt // _LANES):
            cs = slice(tc * _LANES, (tc + 1) * _LANES)
            for j0 in range(0, nk, gate_rows):
                js = slice(j0, j0 + gate_rows)
                gate = jnp.zeros((gate_rows, _LANES), _F32)
                for h in range(n_heads):
                    thr = thr_ref[h, ii:ii + 1, cs]
                    e0 = e0_ref[h, ii:ii + 1, cs]
                    gate = gate + jnp.where(s1_ref[h, js, cs] >= thr, e1z_ref[h, js, cs] * e0, 0.0)
                rs = slice(ii * nk + j0, ii * nk + j0 + gate_rows)
                w_ref[rs, cs] = (_gelu(a_ref[rs, cs]) * gate).astype(_BF16)

    per = ni // halves
    for half in range(halves):
        for ii in range(half * per, (half + 1) * per):
            gate_block(ii)
        es = slice(half * per * nk, (half + 1) * per * nk)
        o_ref[...] += jnp.dot(vt_ref[:, es], w_ref[es, :], preferred_element_type=_F32)
    a_ref[...] = _dot_nt(u_ref[...], hb_ref[...])


def _peer_expert_slabs(u, v, ni, nk):
    layers, _, d = v.shape
    et = ni * nk
    vt = v.astype(_BF16).reshape(layers, -1, et, d).transpose(0, 1, 3, 2)
    return u.astype(_BF16), vt


def _peer_dense(hb, u, vt, layer, thr, e0, s1, e1z, n_heads, tt=512, ni=4):
    t, d = hb.shape
    nk = s1.shape[1]
    tt = min(tt, t)
    et = ni * nk
    n_tiles = nk // ni
    n_work = (t // tt) * n_tiles

    def stage(lag):
        return lambda k: jnp.clip(k - lag, 0, n_work - 1)

    once = pl.Buffered(1)
    gate_spec = pl.BlockSpec((n_heads, nk, tt), lambda k: (0, 0, stage(1)(k) // n_tiles),
                             pipeline_mode=once)
    row_spec = pl.BlockSpec((n_heads, None, _SUBLANES, tt),
                            lambda k: (0, stage(1)(k) % n_tiles, 0, stage(1)(k) // n_tiles))
    return pl.pallas_call(
        functools.partial(_peer_kernel, ni=ni, n_heads=n_heads, n_tiles=n_tiles, n_work=n_work),
        grid=(n_work + 1,),
        in_specs=[pl.BlockSpec((tt, d), lambda k: (stage(0)(k) // n_tiles, 0), pipeline_mode=once),
                  pl.BlockSpec((None, et, d), lambda k: (layer, stage(0)(k) % n_tiles, 0)),
                  pl.BlockSpec((None, None, d, et), lambda k: (layer, stage(1)(k) % n_tiles, 0, 0)),
                  row_spec, row_spec, gate_spec, gate_spec],
        out_specs=pl.BlockSpec((d, tt), lambda k: (0, stage(1)(k) // n_tiles), pipeline_mode=once),
        out_shape=jax.ShapeDtypeStruct((d, t), _F32),
        scratch_shapes=[pltpu.VMEM((et, tt), _F32), pltpu.VMEM((et, tt), _BF16)],
        compiler_params=_params(("arbitrary",)),
        name="peer_dense",
    )(hb, u, vt, thr, e0, s1, e1z)


_PEER_TILE_KEYS = 8


def _peer_ffn(hb, w_q, layer, sub_keys, u, vt):
    half = sub_keys.shape[2]
    n_heads = w_q.shape[2] // (2 * half)
    q = _matmul(hb, w_q, layer, _BF16)
    ni = _PEER_TILE_KEYS
    thr, e0, s1, e1z = _peer_gate(q, sub_keys.astype(_BF16), n_heads, ni)
    return _peer_dense(hb, u, vt, layer, thr, e0, s1, e1z, n_heads, ni=ni)


def kernel(x, a_w_in, a_w_out, b_w_in, b_f_bias, b_w_out, peer_w_q, peer_sub_keys, peer_u, peer_v,
           ln_mix_g, ln_mix_b, ln_ffn_g, ln_ffn_b):
    b, s, d = x.shape
    depth = ln_mix_g.shape[0]
    alpha = (2 * depth) ** 0.25
    h = x.reshape(b * s, d)
    hb = None
    u_b, vt_b = _peer_expert_slabs(peer_u, peer_v, _PEER_TILE_KEYS, peer_sub_keys.shape[2])
    for i in range(depth):
        j = i // 2
        if i % 2 == 0:
            mix = _dilated_mixer(h, hb, a_w_in, a_w_out, j, b, s)
        else:
            mix = _forgetting_mixer(hb, b_w_in, b_f_bias[j], b_w_out, j, b, s)
        h, hb = _ln_residual(h, mix, ln_mix_g[i], ln_mix_b[i], alpha)
        ffn_t = _peer_ffn(hb, peer_w_q, i, peer_sub_keys[i], u_b, vt_b)
        h, hb = _ln_residual(h, ffn_t, ln_ffn_g[i], ln_ffn_b[i], alpha, transposed_mix=True)
    return h.reshape(b, s, d)
```

```python
import functools
import math

import jax
import jax.numpy as jnp
from jax import lax
from jax.experimental import pallas as pl
from jax.experimental.pallas import tpu as pltpu

_F32 = jnp.float32
_BF16 = jnp.bfloat16

_HEAD_DIM = 128
_DIL_WINDOWS = (128, 512, 2048)
_DIL_RATES = (1, 4, 16)
_PEER_TOPK = 16
_LN_EPS = 1e-5
_NEG = -1e30
_LANES = 128
_SUBLANES = 8
_LOG2E = math.log2(math.e)
_VMEM_LIMIT = 60 * 1024 * 1024


def _dot_nt(a, b):
    return lax.dot_general(a, b, (((1,), (1,)), ((), ())), preferred_element_type=_F32)


def _params(sem):
    return pltpu.CompilerParams(dimension_semantics=sem, vmem_limit_bytes=_VMEM_LIMIT)


def _mm_kernel(a_ref, w_ref, *rest, scaled):
    if scaled:
        s_ref, o_ref, wb_ref = rest
    else:
        o_ref, wb_ref = rest

    @pl.when(pl.program_id(1) == 0)
    def _():
        w = w_ref[...]
        if scaled:
            w = w * s_ref[...]
        wb_ref[...] = w.astype(wb_ref.dtype)

    o_ref[...] = jnp.dot(a_ref[...], wb_ref[...], preferred_element_type=_F32).astype(o_ref.dtype)


def _matmul(a, w, layer, out_dtype, tm=1024, tn=512, col_start=0, n_cols=None, col_scale=None):
    m, k = a.shape
    n = w.shape[2] if n_cols is None else n_cols
    tm = min(tm, m)
    tn = min(tn, n)
    while n % tn or col_start % tn:
        tn -= _LANES
    assert m % tm == 0 and tn > 0
    first = col_start // tn
    w_spec = pl.BlockSpec((None, k, tn), lambda j, i: (layer, 0, j + first))
    operands = [a, w]
    in_specs = [pl.BlockSpec((tm, k), lambda j, i: (i, 0)), w_spec]
    if col_scale is not None:
        operands.append(col_scale.reshape(1, -1).astype(_F32))
        in_specs.append(pl.BlockSpec((1, tn), lambda j, i: (0, j + first)))
    return pl.pallas_call(
        functools.partial(_mm_kernel, scaled=col_scale is not None),
        grid=(n // tn, m // tm),
        in_specs=in_specs,
        out_specs=pl.BlockSpec((tm, tn), lambda j, i: (i, j)),
        out_shape=jax.ShapeDtypeStruct((m, n), out_dtype),
        scratch_shapes=[pltpu.VMEM((k, tn), _BF16)],
        compiler_params=_params(("parallel", "arbitrary")),
        name="matmul",
    )(*operands)


def _ln_body(y, g_ref, b_ref, o_ref, ob_ref):
    mu = jnp.mean(y, axis=-1, keepdims=True)
    d = y - mu
    var = jnp.mean(d * d, axis=-1, keepdims=True)
    out = d * lax.rsqrt(var + _LN_EPS) * g_ref[...] + b_ref[...]
    o_ref[...] = out
    ob_ref[...] = out.astype(_BF16)


def _ln_kernel(h_ref, m_ref, g_ref, b_ref, o_ref, ob_ref, *, alpha):
    _ln_body(alpha * h_ref[...] + m_ref[...], g_ref, b_ref, o_ref, ob_ref)


def _ln_t_kernel(h_ref, mt_ref, g_ref, b_ref, o_ref, ob_ref, *, alpha):
    _ln_body(alpha * h_ref[...] + mt_ref[...].T, g_ref, b_ref, o_ref, ob_ref)


def _ln_residual(h, mix, g, b, alpha, transposed_mix=False, tr=256):
    t, d = h.shape
    tr = min(tr, t)
    assert t % tr == 0
    if transposed_mix:
        body = _ln_t_kernel
        mix_spec = pl.BlockSpec((d, tr), lambda i: (0, i))
    else:
        body = _ln_kernel
        mix_spec = pl.BlockSpec((tr, d), lambda i: (i, 0))
    row = pl.BlockSpec((tr, d), lambda i: (i, 0))
    vec = pl.BlockSpec((1, d), lambda i: (0, 0))
    return pl.pallas_call(
        functools.partial(body, alpha=alpha),
        grid=(t // tr,),
        in_specs=[row, mix_spec, vec, vec],
        out_specs=[row, row],
        out_shape=[jax.ShapeDtypeStruct((t, d), _F32), jax.ShapeDtypeStruct((t, d), _BF16)],
        compiler_params=_params(("parallel",)),
        name="ln_residual",
    )(h, mix, g.reshape(1, d), b.reshape(1, d))


def _dil_kernel(*refs, dil, n_heads, with_prev):
    if with_prev:
        q_ref, kc_ref, kp_ref, vc_ref, vp_ref, o_ref, lse_ref = refs
    else:
        q_ref, kc_ref, vc_ref, o_ref, lse_ref = refs
    n = _LANES
    hd = _HEAD_DIM
    nkeys = 2 * n if with_prev else n
    row = lax.broadcasted_iota(jnp.int32, (n, nkeys), 0)
    col = lax.broadcasted_iota(jnp.int32, (n, nkeys), 1)
    dist = (nkeys - n) + row - col
    valid = (dist >= 0) & (dist <= n)
    if with_prev:
        valid = valid & ((col >= n) | (pl.program_id(1) > 0))
    key_pos = (lax.broadcasted_iota(jnp.int32, (1, nkeys), 1) - (nkeys - n)).astype(_F32)
    query_pos = lax.broadcasted_iota(jnp.int32, (n, 1), 0).astype(_F32)
    lane = lax.broadcasted_iota(jnp.int32, (n, n), 1)
    ones = jnp.ones((nkeys, hd), _BF16)

    def keys_of(cur_ref, prev_ref, sl):
        if with_prev:
            return jnp.concatenate([prev_ref[0, :, sl], cur_ref[0, :, sl]], axis=0)
        return cur_ref[0, :, sl]

    lse_tile = jnp.zeros((n, n), _F32)
    group = 4
    for h0 in range(0, n_heads, group):
        heads = range(h0, min(h0 + group, n_heads))
        slopes = [float(dil) * _LOG2E * 2.0 ** (-8.0 * (h + 1) / n_heads) for h in heads]
        sls = [slice(h * hd, (h + 1) * hd) for h in heads]
        logits = []
        for sl, c in zip(sls, slopes):
            k2 = keys_of(kc_ref, kp_ref if with_prev else None, sl)
            logits.append(jnp.where(valid, _dot_nt(q_ref[0, :, sl], k2) + c * key_pos, _NEG))
        probs, maxes = [], []
        for s in logits:
            mx = jnp.max(s, axis=1, keepdims=True)
            maxes.append(mx)
            probs.append(jnp.exp2(s - mx).astype(_BF16))
        for h, sl, c, p, mx in zip(heads, sls, slopes, probs, maxes):
            v2 = keys_of(vc_ref, vp_ref if with_prev else None, sl)
            r = jnp.dot(p, jnp.concatenate([v2, ones], axis=1), preferred_element_type=_F32)
            den = r[:, hd:hd + 1]
            o_ref[0, :, sl] = r[:, :hd] / den
            lse = (mx + jnp.log2(den) - c * query_pos) * (1.0 / _LOG2E)
            lse_tile = jnp.where(lane == h, lse, lse_tile)
    lse_ref[0] = lse_tile


def _deinterleave_kernel(*refs, dils, n_in):
    x_refs, o_refs = refs[:n_in], refs[n_in:]
    s = x_refs[0].shape[1]
    for o_ref, dil in zip(o_refs, dils):
        l = s // dil
        for j, x_ref in enumerate(x_refs):
            cs = slice(j * _LANES, (j + 1) * _LANES)
            for r in range(dil):
                o_ref[0, r * l:(r + 1) * l, cs] = x_ref[0, pl.ds(r, l, stride=dil), :].astype(o_ref.dtype)


def _deinterleave(h3, dils, slabs=4):
    b, s, d = h3.shape
    slabs = min(slabs, d // _LANES)
    assert d % (slabs * _LANES) == 0

    def in_spec(j):
        return pl.BlockSpec((1, s, _LANES), lambda bi, ci: (bi, 0, ci * slabs + j))

    out_spec = pl.BlockSpec((1, s, slabs * _LANES), lambda bi, ci: (bi, 0, ci))
    return pl.pallas_call(
        functools.partial(_deinterleave_kernel, dils=dils, n_in=slabs),
        grid=(b, d // (slabs * _LANES)),
        in_specs=[in_spec(j) for j in range(slabs)],
        out_specs=[out_spec] * len(dils),
        out_shape=[jax.ShapeDtypeStruct((b, s, d), _BF16)] * len(dils),
        compiler_params=_params(("parallel", "parallel")),
        name="deinterleave",
    )(*[h3] * slabs)


def _dilated_group(proj, g, n_heads, b, s):
    w = n_heads * _HEAD_DIM
    dil = _DIL_RATES[g]
    n = _DIL_WINDOWS[g] // dil
    assert n == _LANES and n_heads <= _LANES
    l = s // dil
    nb = l // n
    assert l % n == 0
    view = proj.reshape(b * dil, l, 3 * w)

    def spec(which, prev):
        def index(p, i):
            return (p, jnp.maximum(i - 1, 0) if prev else i, which)
        return pl.BlockSpec((1, n, w), index)

    with_prev = nb > 1
    in_specs = [spec(0, False), spec(1, False), spec(1, True), spec(2, False), spec(2, True)]
    if not with_prev:
        in_specs = [spec(0, False), spec(1, False), spec(2, False)]
    return pl.pallas_call(
        functools.partial(_dil_kernel, dil=dil, n_heads=n_heads, with_prev=with_prev),
        grid=(b * dil, nb),
        in_specs=in_specs,
        out_specs=[pl.BlockSpec((1, n, w), lambda p, i: (p, i, 0)),
                   pl.BlockSpec((1, n, _LANES), lambda p, i: (p, i, 0))],
        out_shape=[jax.ShapeDtypeStruct((b * dil, l, w), _F32),
                   jax.ShapeDtypeStruct((b * dil, l, _LANES), _F32)],
        compiler_params=_params(("parallel", "parallel")),
        name=f"dilated_attention_g{g}",
    )(*[view] * len(in_specs))


def _merge_kernel(*refs, dils, n_heads):
    ng = len(dils)
    o_refs, l_refs, out_ref = refs[:ng], refs[ng:2 * ng], refs[2 * ng]
    scratch = iter(refs[2 * ng + 1:])
    tr = out_ref.shape[1]
    outs, lses = [], []
    for o_ref, l_ref, dil in zip(o_refs, l_refs, dils):
        if dil == 1:
            outs.append(o_ref.at[0, 0])
            lses.append(l_ref[0, 0])
            continue
        o_scr, l_scr = next(scratch), next(scratch)
        rows = tr // dil
        for r in range(dil):
            dst = pl.ds(r, rows, stride=dil)
            l_scr[dst, :] = l_ref[0, r]
            for h in range(n_heads):
                o_scr[h, dst, :] = o_ref[0, r, :, h * _HEAD_DIM:(h + 1) * _HEAD_DIM]
        outs.append(o_scr)
        lses.append(l_scr[...])
    for h in range(n_heads):
        sl = slice(h * _HEAD_DIM, (h + 1) * _HEAD_DIM)
        lh = [x[:, h:h + 1] for x in lses]
        mx = functools.reduce(jnp.maximum, lh)
        ws = [jnp.exp(x - mx) for x in lh]
        num = sum(wg * (o[:, sl] if dil == 1 else o[h]) for wg, o, dil in zip(ws, outs, dils))
        out_ref[0, :, sl] = (num / sum(ws)).astype(out_ref.dtype)


def _merge_groups(outs, lses, dils, n_heads, b, s, tr=256):
    w = outs[0].shape[-1]
    tr = min(tr, s)
    assert all(tr % (dil * _SUBLANES) == 0 for dil in dils)

    def spec(dil, width):
        return pl.BlockSpec((1, dil, tr // dil, width), lambda bi, i: (bi, 0, i, 0))

    view = lambda x, dil: x.reshape(b, dil, s // dil, x.shape[-1])
    scratch = []
    for dil in dils:
        if dil != 1:
            scratch += [pltpu.VMEM((n_heads, tr, _HEAD_DIM), _F32), pltpu.VMEM((tr, _LANES), _F32)]
    return pl.pallas_call(
        functools.partial(_merge_kernel, dils=dils, n_heads=n_heads),
        grid=(b, s // tr),
        in_specs=[spec(dil, w) for dil in dils] + [spec(dil, _LANES) for dil in dils],
        out_specs=pl.BlockSpec((1, tr, w), lambda bi, i: (bi, i, 0)),
        out_shape=jax.ShapeDtypeStruct((b, s, w), _BF16),
        scratch_shapes=scratch,
        compiler_params=_params(("parallel", "parallel")),
        name="dilated_merge",
    )(*[view(o, dil) for o, dil in zip(outs, dils)], *[view(x, dil) for x, dil in zip(lses, dils)])


def _dilated_mixer(h, hb, w_in, w_out, layer, b, s):
    d = h.shape[1]
    n_heads = d // _HEAD_DIM // 2
    w = n_heads * _HEAD_DIM
    wanted = tuple(dil for dil in _DIL_RATES if dil != 1 or hb is None)
    copies = dict(zip(wanted, _deinterleave(h.reshape(b, s, d), wanted)))
    if hb is None:
        hb = copies[1].reshape(b * s, d)
    col_scale = jnp.ones((len(_DIL_RATES), 3, w), _F32).at[:, 0].set(_LOG2E / math.sqrt(_HEAD_DIM))
    col_scale = col_scale.reshape(-1)
    outs, lses = [], []
    for g, dil in enumerate(_DIL_RATES):
        rows = hb if dil == 1 else copies[dil].reshape(b * s, d)
        proj = _matmul(rows, w_in, layer, _BF16, col_start=g * 3 * w, n_cols=3 * w, col_scale=col_scale)
        o, lse = _dilated_group(proj, g, n_heads, b, s)
        outs.append(o)
        lses.append(lse)
    merged = _merge_groups(outs, lses, _DIL_RATES, n_heads, b, s)
    return _matmul(merged.reshape(b * s, w), w_out, layer, _F32)


def _fox_gate_kernel(wf_ref, h_ref, bias_ref, c_ref, carry_ref):
    @pl.when(pl.program_id(1) == 0)
    def _():
        carry_ref[...] = jnp.zeros_like(carry_ref)

    z = _dot_nt(wf_ref[...], h_ref[0]) + bias_ref[...]
    log_f = jnp.minimum(z, 0.0) - jnp.log1p(jnp.exp(-jnp.abs(z)))
    log_f = log_f * _LOG2E
    nh, ts = log_f.shape
    lane = lax.broadcasted_iota(jnp.int32, (nh, _LANES), 1)
    carry = carry_ref[...]
    for c in range(ts // _LANES):
        x = log_f[:, c * _LANES:(c + 1) * _LANES]
        shift = 1
        while shift < _LANES:
            x = x + jnp.where(lane >= shift, pltpu.roll(x, shift, 1), 0.0)
            shift *= 2
        x = x + carry
        c_ref[0, :, c * _LANES:(c + 1) * _LANES] = x
        carry = jnp.broadcast_to(x[:, _LANES - 1:_LANES], (nh, _LANES))
    carry_ref[...] = carry


def _fox_gate(hb3, wf_t, bias, ts=512):
    b, s, d = hb3.shape
    nh = wf_t.shape[0]
    ts = min(ts, s)
    return pl.pallas_call(
        _fox_gate_kernel,
        grid=(b, s // ts),
        in_specs=[pl.BlockSpec((nh, d), lambda bi, si: (0, 0)),
                  pl.BlockSpec((1, ts, d), lambda bi, si: (bi, si, 0)),
                  pl.BlockSpec((nh, 1), lambda bi, si: (0, 0))],
        out_specs=pl.BlockSpec((1, nh, ts), lambda bi, si: (bi, 0, si)),
        out_shape=jax.ShapeDtypeStruct((b, nh, s), _F32),
        scratch_shapes=[pltpu.VMEM((nh, _LANES), _F32)],
        compiler_params=_params(("parallel", "arbitrary")),
        name="fox_gate_cumsum",
    )(wf_t, hb3, bias.reshape(nh, 1).astype(_F32))


def _fox_kernel(q_ref, k_ref, v_ref, c_ref, o_ref, m_ref, acc_ref, *, tq, heads):
    qi = pl.program_id(2)
    hd = _HEAD_DIM
    m_ref[...] = jnp.full_like(m_ref, _NEG)
    acc_ref[...] = jnp.zeros_like(acc_ref)
    ones = jnp.ones((tq, hd), _BF16)

    def step(kb, diagonal):
        k_start = pl.multiple_of(kb * tq, tq)
        sls = [slice(g * hd, (g + 1) * hd) for g in range(heads)]
        logits = []
        for g, sl in enumerate(sls):
            s = _dot_nt(q_ref[0, :, sl], k_ref[0, pl.ds(k_start, tq), sl]) - c_ref[0, g, :, pl.ds(k_start, tq)]
            if diagonal:
                row = lax.broadcasted_iota(jnp.int32, (tq, tq), 0)
                col = lax.broadcasted_iota(jnp.int32, (tq, tq), 1)
                s = jnp.where(col <= row, s, _NEG)
            logits.append(s)
        probs, alphas = [], []
        for g, s in enumerate(logits):
            m_old = m_ref[g]
            m_new = jnp.maximum(m_old, jnp.max(s, axis=1, keepdims=True))
            m_ref[g] = m_new
            alphas.append(jnp.exp2(m_old - m_new))
            probs.append(jnp.exp2(s - jnp.concatenate([m_new] * (tq // hd), axis=1)).astype(_BF16))
        for g, (sl, p, alpha) in enumerate(zip(sls, probs, alphas)):
            v1 = jnp.concatenate([v_ref[0, pl.ds(k_start, tq), sl], ones], axis=1)
            acc_ref[g] = (jnp.concatenate([alpha, alpha], axis=1) * acc_ref[g]
                          + jnp.dot(p, v1, preferred_element_type=_F32))

    lax.fori_loop(0, qi, lambda kb, carry: step(kb, False), None)
    step(qi, True)
    for g in range(heads):
        o_ref[0, :, g * hd:(g + 1) * hd] = (acc_ref[g, :, :hd] / acc_ref[g, :, hd:]).astype(o_ref.dtype)


def _fox_attention(qkv, c, n_heads, tq=512, heads=4):
    b, s, _ = qkv.shape
    tq = min(tq, s)
    heads = min(heads, n_heads)
    assert n_heads % heads == 0 and tq % _HEAD_DIM == 0
    ng = n_heads // heads
    gw = heads * _HEAD_DIM
    return pl.pallas_call(
        functools.partial(_fox_kernel, tq=tq, heads=heads),
        grid=(b, ng, s // tq),
        in_specs=[pl.BlockSpec((1, tq, gw), lambda bi, h, qi: (bi, qi, h)),
                  pl.BlockSpec((1, s, gw), lambda bi, h, qi: (bi, 0, ng + h)),
                  pl.BlockSpec((1, s, gw), lambda bi, h, qi: (bi, 0, 2 * ng + h)),
                  pl.BlockSpec((1, heads, 1, s), lambda bi, h, qi: (bi, h, 0, 0))],
        out_specs=pl.BlockSpec((1, tq, gw), lambda bi, h, qi: (bi, qi, h)),
        out_shape=jax.ShapeDtypeStruct((b, s, n_heads * _HEAD_DIM), _BF16),
        scratch_shapes=[pltpu.VMEM((heads, tq, _HEAD_DIM), _F32),
                        pltpu.VMEM((heads, tq, 2 * _HEAD_DIM), _F32)],
        compiler_params=_params(("parallel", "parallel", "parallel")),
        name="fox_attention",
    )(qkv, qkv, qkv, c.reshape(b, n_heads, 1, s))


def _forgetting_mixer(hb, w_in, f_bias, w_out, layer, b, s):
    d = hb.shape[1]
    n_heads = d // _HEAD_DIM
    width = n_heads * _HEAD_DIM
    q_scale = _LOG2E / math.sqrt(_HEAD_DIM)
    col_scale = jnp.ones((3, width), _F32).at[0].set(q_scale).reshape(-1)
    qkv = _matmul(hb, w_in, layer, _BF16, n_cols=3 * width, col_scale=col_scale).reshape(b, s, 3 * width)
    wf_t = w_in[layer, :, 3 * width:].T.astype(_BF16)
    c = _fox_gate(hb.reshape(b, s, d), wf_t, f_bias)
    o = _fox_attention(qkv, c, n_heads)
    return _matmul(o.reshape(b * s, width), w_out, layer, _F32)


def _sorting_network(n):
    comps = []

    def merge(lo, m, r):
        step = 2 * r
        if step < m:
            merge(lo, m, step)
            merge(lo + r, m, step)
            comps.extend((i, i + r) for i in range(lo + r, lo + m - r, step))
        else:
            comps.append((lo, lo + r))

    def sort(lo, m):
        if m > 1:
            sort(lo, m // 2)
            sort(lo + m // 2, m // 2)
            merge(lo, m, 1)

    sort(0, n)
    return comps


def _pop_columns(cols, count, singles=None):
    r, t = cols[0].shape
    row = lax.broadcasted_iota(jnp.int32, (r, t), 0)
    ninf = jnp.full((r, t), -jnp.inf, _F32)
    cols = list(cols)
    if singles is not None:
        r2 = singles.shape[0]
        row2 = lax.broadcasted_iota(jnp.int32, (r2, t), 0) + r
    vals = []
    for k in range(count):
        head = cols[0]
        m = jnp.max(head, axis=0, keepdims=True)
        if singles is not None:
            m = jnp.maximum(m, jnp.max(singles, axis=0, keepdims=True))
        vals.append(m)
        left = count - 1 - k
        if left == 0:
            break
        none = r + (0 if singles is None else singles.shape[0])
        first = jnp.min(jnp.where(head == m, row, none), axis=0, keepdims=True)
        if singles is not None:
            first = jnp.minimum(first, jnp.min(jnp.where(singles == m, row2, none),
                                               axis=0, keepdims=True))
            singles = jnp.where(row2 == first, -jnp.inf, singles)
        hit = row == first
        depth = min(left, len(cols))
        cols = [jnp.where(hit, cols[d + 1] if d + 1 < len(cols) else ninf, cols[d])
                for d in range(depth)]
    return vals


def _top_values(scores, count):
    n = scores.shape[0]
    stacks = [scores[v * _SUBLANES:(v + 1) * _SUBLANES] for v in range(n // _SUBLANES)]
    for i, j in _sorting_network(len(stacks)):
        stacks[i], stacks[j] = jnp.maximum(stacks[i], stacks[j]), jnp.minimum(stacks[i], stacks[j])
    return _pop_columns(stacks, count)


def _stack_rows(rows_1t, n_rows):
    t = rows_1t[0].shape[1]
    row = lax.broadcasted_iota(jnp.int32, (n_rows, t), 0)
    out = jnp.full((n_rows, t), -jnp.inf, _F32)
    for k, v in enumerate(rows_1t):
        out = jnp.where(row == k, v, out)
    return out


def _store_tile_rows(ref, h, x, ni):
    per = _SUBLANES // ni
    for g in range(x.shape[0] // _SUBLANES):
        grp = x[g * _SUBLANES:(g + 1) * _SUBLANES]
        for j in range(per):
            ref[h, g * per + j] = grp if j == 0 else pltpu.roll(grp, _SUBLANES - j * ni, 0)


def _peer_gate_kernel(q_ref, keys_ref, thr_ref, e0_ref, s1_ref, e1z_ref, *, n_heads, topk, ni):
    half = keys_ref.shape[2]
    tt = q_ref.shape[0]
    k0 = keys_ref[0]
    k1 = keys_ref[1]
    nkeep = topk + 1
    rows = _SUBLANES * (-(-nkeep // _SUBLANES))
    row = lax.broadcasted_iota(jnp.int32, (rows, tt), 0)
    for h in range(n_heads):
        s0 = _dot_nt(k0, q_ref[:, (2 * h) * half:(2 * h + 1) * half])
        s1 = _dot_nt(k1, q_ref[:, (2 * h + 1) * half:(2 * h + 2) * half])
        a_vals = _top_values(s0, nkeep)
        b_vals = _top_values(s1, nkeep)
        a_rows = _stack_rows(a_vals, rows)
        stacks = [jnp.where(row[:_SUBLANES] < nkeep // (j + 1), a_rows[:_SUBLANES] + b_vals[j],
                            -jnp.inf) for j in range(nkeep)]
        assert nkeep // 2 <= _SUBLANES
        singles = jnp.where(row[_SUBLANES:] < nkeep, a_rows[_SUBLANES:] + b_vals[0], -jnp.inf)
        top = _pop_columns(stacks, nkeep, singles)
        z = sum(jnp.exp(tv - top[0]) for tv in top[:topk])
        thr = 0.5 * (top[topk - 1] + top[topk])
        _store_tile_rows(thr_ref, h, thr - s0, ni)
        _store_tile_rows(e0_ref, h, jnp.exp(s0 - a_vals[0]), ni)
        s1_ref[h] = s1
        e1z_ref[h] = jnp.exp(s1 - b_vals[0]) / z


def _peer_gate(q, sub_keys, n_heads, ni, tt=128):
    t = q.shape[0]
    nk, half = sub_keys.shape[1:]
    tt = min(tt, t)
    assert _SUBLANES % ni == 0 and nk % _SUBLANES == 0
    n_tiles = nk // ni
    row_spec = pl.BlockSpec((n_heads, n_tiles, _SUBLANES, tt), lambda i: (0, 0, 0, i))
    row_shape = jax.ShapeDtypeStruct((n_heads, n_tiles, _SUBLANES, t), _F32)
    col_spec = pl.BlockSpec((n_heads, nk, tt), lambda i: (0, 0, i))
    col_shape = jax.ShapeDtypeStruct((n_heads, nk, t), _F32)
    return pl.pallas_call(
        functools.partial(_peer_gate_kernel, n_heads=n_heads, topk=_PEER_TOPK, ni=ni),
        grid=(t // tt,),
        in_specs=[pl.BlockSpec((tt, q.shape[1]), lambda i: (i, 0)),
                  pl.BlockSpec((2, nk, half), lambda i: (0, 0, 0))],
        out_specs=[row_spec, row_spec, col_spec, col_spec],
        out_shape=[row_shape, row_shape, col_shape, col_shape],
        compiler_params=_params(("parallel",)),
        name="peer_gate",
    )(q, sub_keys)


def _gelu(x):
    return 0.5 * x * (1.0 + lax.erf(x * (1.0 / math.sqrt(2.0))))


def _peer_kernel(hb_ref, u_ref, vt_ref, thr_ref, e0_ref, s1_ref, e1z_ref, o_ref, a_ref, w_ref,
                 *, ni, n_heads, n_tiles, n_work):
    k = pl.program_id(0)

    @pl.when(k == 0)
    def _():
        a_ref[...] = jnp.zeros_like(a_ref)

    @pl.when(jnp.clip(k - 1, 0, n_work - 1) % n_tiles == 0)
    def _():
        o_ref[...] = jnp.zeros_like(o_ref)

    nk = s1_ref.shape[1]
    tt = hb_ref.shape[0]
    gate_rows = 16
    halves = 4

    def gate_block(ii):
        for tc in range(tt // _LANES):
            cs = slice(tc * _LANES, (tc + 1) * _LANES)
            for j0 in range(0, nk, gate_rows):
                js = slice(j0, j0 + gate_rows)
                gate = jnp.zeros((gate_rows, _LANES), _F32)
                for h in range(n_heads):
                    thr = thr_ref[h, ii:ii + 1, cs]
                    e0 = e0_ref[h, ii:ii + 1, cs]
                    gate = gate + jnp.where(s1_ref[h, js, cs] >= thr, e1z_ref[h, js, cs] * e0, 0.0)
                rs = slice(ii * nk + j0, ii * nk + j0 + gate_rows)
                w_ref[rs, cs] = (_gelu(a_ref[rs, cs]) * gate).astype(_BF16)

    per = ni // halves
    for half in range(halves):
        for ii in range(half * per, (half + 1) * per):
            gate_block(ii)
        es = slice(half * per * nk, (half + 1) * per * nk)
        o_ref[...] += jnp.dot(vt_ref[:, es], w_ref[es, :], preferred_element_type=_F32)
    a_ref[...] = _dot_nt(u_ref[...], hb_ref[...])


def _peer_expert_slabs(u, v, ni, nk):
    layers, _, d = v.shape
    et = ni * nk
    vt = v.astype(_BF16).reshape(layers, -1, et, d).transpose(0, 1, 3, 2)
    return u.astype(_BF16), vt


def _peer_dense(hb, u, vt, layer, thr, e0, s1, e1z, n_heads, tt=512, ni=4):
    t, d = hb.shape
    nk = s1.shape[1]
    tt = min(tt, t)
    et = ni * nk
    n_tiles = nk // ni
    n_work = (t // tt) * n_tiles

    def stage(lag):
        return lambda k: jnp.clip(k - lag, 0, n_work - 1)

    once = pl.Buffered(1)
    gate_spec = pl.BlockSpec((n_heads, nk, tt), lambda k: (0, 0, stage(1)(k) // n_tiles),
                             pipeline_mode=once)
    row_spec = pl.BlockSpec((n_heads, None, _SUBLANES, tt),
                            lambda k: (0, stage(1)(k) % n_tiles, 0, stage(1)(k) // n_tiles))
    return pl.pallas_call(
        functools.partial(_peer_kernel, ni=ni, n_heads=n_heads, n_tiles=n_tiles, n_work=n_work),
        grid=(n_work + 1,),
        in_specs=[pl.BlockSpec((tt, d), lambda k: (stage(0)(k) // n_tiles, 0), pipeline_mode=once),
                  pl.BlockSpec((None, et, d), lambda k: (layer, stage(0)(k) % n_tiles, 0)),
                  pl.BlockSpec((None, None, d, et), lambda k: (layer, stage(1)(k) % n_tiles, 0, 0)),
                  row_spec, row_spec, gate_spec, gate_spec],
        out_specs=pl.BlockSpec((d, tt), lambda k: (0, stage(1)(k) // n_tiles), pipeline_mode=once),
        out_shape=jax.ShapeDtypeStruct((d, t), _F32),
        scratch_shapes=[pltpu.VMEM((et, tt), _F32), pltpu.VMEM((et, tt), _BF16)],
        compiler_params=_params(("arbitrary",)),
        name="peer_dense",
    )(hb, u, vt, thr, e0, s1, e1z)


_PEER_TILE_KEYS = 8


def _peer_ffn(hb, w_q, layer, sub_keys, u, vt):
    half = sub_keys.shape[2]
    n_heads = w_q.shape[2] // (2 * half)
    q = _matmul(hb, w_q, layer, _BF16)
    ni = _PEER_TILE_KEYS
    thr, e0, s1, e1z = _peer_gate(q, sub_keys.astype(_BF16), n_heads, ni)
    return _peer_dense(hb, u, vt, layer, thr, e0, s1, e1z, n_heads, ni=ni)


def kernel(x, a_w_in, a_w_out, b_w_in, b_f_bias, b_w_out, peer_w_q, peer_sub_keys, peer_u, peer_v,
           ln_mix_g, ln_mix_b, ln_ffn_g, ln_ffn_b):
    b, s, d = x.shape
    depth = ln_mix_g.shape[0]
    alpha = (2 * depth) ** 0.25
    h = x.reshape(b * s, d)
    hb = None
    u_b, vt_b = _peer_expert_slabs(peer_u, peer_v, _PEER_TILE_KEYS, peer_sub_keys.shape[2])
    for i in range(depth):
        j = i // 2
        if i % 2 == 0:
            mix = _dilated_mixer(h, hb, a_w_in, a_w_out, j, b, s)
        else:
            mix = _forgetting_mixer(hb, b_w_in, b_f_bias[j], b_w_out, j, b, s)
        h, hb = _ln_residual(h, mix, ln_mix_g[i], ln_mix_b[i], alpha)
        ffn_t = _peer_ffn(hb, peer_w_q, i, peer_sub_keys[i], u_b, vt_b)
        h, hb = _ln_residual(h, ffn_t, ln_ffn_g[i], ln_ffn_b[i], alpha, transposed_mix=True)
    return h.reshape(b, s, d)
```

```python
import functools
import math

import jax
import jax.numpy as jnp
from jax import lax
from jax.experimental import pallas as pl
from jax.experimental.pallas import tpu as pltpu

_F32 = jnp.float32
_BF16 = jnp.bfloat16

_HEAD_DIM = 128
_DIL_WINDOWS = (128, 512, 2048)
_DIL_RATES = (1, 4, 16)
_PEER_TOPK = 16
_LN_EPS = 1e-5
_NEG = -1e30
_LANES = 128
_SUBLANES = 8
_LOG2E = math.log2(math.e)
_VMEM_LIMIT = 60 * 1024 * 1024


def _dot_nt(a, b):
    return lax.dot_general(a, b, (((1,), (1,)), ((), ())), preferred_element_type=_F32)


def _params(sem):
    return pltpu.CompilerParams(dimension_semantics=sem, vmem_limit_bytes=_VMEM_LIMIT)


def _mm_kernel(a_ref, w_ref, *rest, scaled):
    if scaled:
        s_ref, o_ref, wb_ref = rest
    else:
        o_ref, wb_ref = rest

    @pl.when(pl.program_id(1) == 0)
    def _():
        w = w_ref[...]
        if scaled:
            w = w * s_ref[...]
        wb_ref[...] = w.astype(wb_ref.dtype)

    o_ref[...] = jnp.dot(a_ref[...], wb_ref[...], preferred_element_type=_F32).astype(o_ref.dtype)


def _matmul(a, w, layer, out_dtype, tm=1024, tn=512, col_start=0, n_cols=None, col_scale=None):
    m, k = a.shape
    n = w.shape[2] if n_cols is None else n_cols
    tm = min(tm, m)
    tn = min(tn, n)
    while n % tn or col_start % tn:
        tn -= _LANES
    assert m % tm == 0 and tn > 0
    first = col_start // tn
    w_spec = pl.BlockSpec((None, k, tn), lambda j, i: (layer, 0, j + first))
    operands = [a, w]
    in_specs = [pl.BlockSpec((tm, k), lambda j, i: (i, 0)), w_spec]
    if col_scale is not None:
        operands.append(col_scale.reshape(1, -1).astype(_F32))
        in_specs.append(pl.BlockSpec((1, tn), lambda j, i: (0, j + first)))
    return pl.pallas_call(
        functools.partial(_mm_kernel, scaled=col_scale is not None),
        grid=(n // tn, m // tm),
        in_specs=in_specs,
        out_specs=pl.BlockSpec((tm, tn), lambda j, i: (i, j)),
        out_shape=jax.ShapeDtypeStruct((m, n), out_dtype),
        scratch_shapes=[pltpu.VMEM((k, tn), _BF16)],
        compiler_params=_params(("parallel", "arbitrary")),
        name="matmul",
    )(*operands)


def _ln_body(y, g_ref, b_ref, o_ref, ob_ref):
    mu = jnp.mean(y, axis=-1, keepdims=True)
    d = y - mu
    var = jnp.mean(d * d, axis=-1, keepdims=True)
    out = d * lax.rsqrt(var + _LN_EPS) * g_ref[...] + b_ref[...]
    o_ref[...] = out
    ob_ref[...] = out.astype(_BF16)


def _ln_kernel(h_ref, m_ref, g_ref, b_ref, o_ref, ob_ref, *, alpha):
    _ln_body(alpha * h_ref[...] + m_ref[...], g_ref, b_ref, o_ref, ob_ref)


def _ln_t_kernel(h_ref, mt_ref, g_ref, b_ref, o_ref, ob_ref, *, alpha):
    _ln_body(alpha * h_ref[...] + mt_ref[...].T, g_ref, b_ref, o_ref, ob_ref)


def _ln_residual(h, mix, g, b, alpha, transposed_mix=False, tr=256):
    t, d = h.shape
    tr = min(tr, t)
    assert t % tr == 0
    if transposed_mix:
        body = _ln_t_kernel
        mix_spec = pl.BlockSpec((d, tr), lambda i: (0, i))
    else:
        body = _ln_kernel
        mix_spec = pl.BlockSpec((tr, d), lambda i: (i, 0))
    row = pl.BlockSpec((tr, d), lambda i: (i, 0))
    vec = pl.BlockSpec((1, d), lambda i: (0, 0))
    return pl.pallas_call(
        functools.partial(body, alpha=alpha),
        grid=(t // tr,),
        in_specs=[row, mix_spec, vec, vec],
        out_specs=[row, row],
        out_shape=[jax.ShapeDtypeStruct((t, d), _F32), jax.ShapeDtypeStruct((t, d), _BF16)],
        compiler_params=_params(("parallel",)),
        name="ln_residual",
    )(h, mix, g.reshape(1, d), b.reshape(1, d))


def _dil_kernel(*refs, dil, n_heads, with_prev):
    if with_prev:
        q_ref, kc_ref, kp_ref, vc_ref, vp_ref, o_ref, lse_ref = refs
    else:
        q_ref, kc_ref, vc_ref, o_ref, lse_ref = refs
    n = _LANES
    hd = _HEAD_DIM
    nkeys = 2 * n if with_prev else n
    row = lax.broadcasted_iota(jnp.int32, (n, nkeys), 0)
    col = lax.broadcasted_iota(jnp.int32, (n, nkeys), 1)
    dist = (nkeys - n) + row - col
    valid = (dist >= 0) & (dist <= n)
    if with_prev:
        valid = valid & ((col >= n) | (pl.program_id(1) > 0))
    key_pos = (lax.broadcasted_iota(jnp.int32, (1, nkeys), 1) - (nkeys - n)).astype(_F32)
    query_pos = lax.broadcasted_iota(jnp.int32, (n, 1), 0).astype(_F32)
    lane = lax.broadcasted_iota(jnp.int32, (n, n), 1)
    ones = jnp.ones((nkeys, hd), _BF16)

    def keys_of(cur_ref, prev_ref, sl):
        if with_prev:
            return jnp.concatenate([prev_ref[0, :, sl], cur_ref[0, :, sl]], axis=0)
        return cur_ref[0, :, sl]

    lse_tile = jnp.zeros((n, n), _F32)
    group = 4
    for h0 in range(0, n_heads, group):
        heads = range(h0, min(h0 + group, n_heads))
        slopes = [float(dil) * _LOG2E * 2.0 ** (-8.0 * (h + 1) / n_heads) for h in heads]
        sls = [slice(h * hd, (h + 1) * hd) for h in heads]
        logits = []
        for sl, c in zip(sls, slopes):
            k2 = keys_of(kc_ref, kp_ref if with_prev else None, sl)
            logits.append(jnp.where(valid, _dot_nt(q_ref[0, :, sl], k2) + c * key_pos, _NEG))
        probs, maxes = [], []
        for s in logits:
            mx = jnp.max(s, axis=1, keepdims=True)
            maxes.append(mx)
            probs.append(jnp.exp2(s - mx).astype(_BF16))
        for h, sl, c, p, mx in zip(heads, sls, slopes, probs, maxes):
            v2 = keys_of(vc_ref, vp_ref if with_prev else None, sl)
            r = jnp.dot(p, jnp.concatenate([v2, ones], axis=1), preferred_element_type=_F32)
            den = r[:, hd:hd + 1]
            o_ref[0, :, sl] = r[:, :hd] / den
            lse = (mx + jnp.log2(den) - c * query_pos) * (1.0 / _LOG2E)
            lse_tile = jnp.where(lane == h, lse, lse_tile)
    lse_ref[0] = lse_tile


def _deinterleave_kernel(*refs, dils, n_in):
    x_refs, o_refs = refs[:n_in], refs[n_in:]
    s = x_refs[0].shape[1]
    for o_ref, dil in zip(o_refs, dils):
        l = s // dil
        for j, x_ref in enumerate(x_refs):
            cs = slice(j * _LANES, (j + 1) * _LANES)
            for r in range(dil):
                o_ref[0, r * l:(r + 1) * l, cs] = x_ref[0, pl.ds(r, l, stride=dil), :].astype(o_ref.dtype)


def _deinterleave(h3, dils, slabs=4):
    b, s, d = h3.shape
    slabs = min(slabs, d // _LANES)
    assert d % (slabs * _LANES) == 0

    def in_spec(j):
        return pl.BlockSpec((1, s, _LANES), lambda bi, ci: (bi, 0, ci * slabs + j))

    out_spec = pl.BlockSpec((1, s, slabs * _LANES), lambda bi, ci: (bi, 0, ci))
    return pl.pallas_call(
        functools.partial(_deinterleave_kernel, dils=dils, n_in=slabs),
        grid=(b, d // (slabs * _LANES)),
        in_specs=[in_spec(j) for j in range(slabs)],
        out_specs=[out_spec] * len(dils),
        out_shape=[jax.ShapeDtypeStruct((b, s, d), _BF16)] * len(dils),
        compiler_params=_params(("parallel", "parallel")),
        name="deinterleave",
    )(*[h3] * slabs)


def _dilated_group(proj, g, n_heads, b, s):
    w = n_heads * _HEAD_DIM
    dil = _DIL_RATES[g]
    n = _DIL_WINDOWS[g] // dil
    assert n == _LANES and n_heads <= _LANES
    l = s // dil
    nb = l // n
    assert l % n == 0
    view = proj.reshape(b * dil, l, 3 * w)

    def spec(which, prev):
        def index(p, i):
            return (p, jnp.maximum(i - 1, 0) if prev else i, which)
        return pl.BlockSpec((1, n, w), index)

    with_prev = nb > 1
    in_specs = [spec(0, False), spec(1, False), spec(1, True), spec(2, False), spec(2, True)]
    if not with_prev:
        in_specs = [spec(0, False), spec(1, False), spec(2, False)]
    return pl.pallas_call(
        functools.partial(_dil_kernel, dil=dil, n_heads=n_heads, with_prev=with_prev),
        grid=(b * dil, nb),
        in_specs=in_specs,
        out_specs=[pl.BlockSpec((1, n, w), lambda p, i: (p, i, 0)),
                   pl.BlockSpec((1, n, _LANES), lambda p, i: (p, i, 0))],
        out_shape=[jax.ShapeDtypeStruct((b * dil, l, w), _F32),
                   jax.ShapeDtypeStruct((b * dil, l, _LANES), _F32)],
        compiler_params=_params(("parallel", "parallel")),
        name=f"dilated_attention_g{g}",
    )(*[view] * len(in_specs))


def _merge_kernel(*refs, dils, n_heads):
    ng = len(dils)
    o_refs, l_refs, out_ref = refs[:ng], refs[ng:2 * ng], refs[2 * ng]
    scratch = iter(refs[2 * ng + 1:])
    tr = out_ref.shape[1]
    outs, lses = [], []
    for o_ref, l_ref, dil in zip(o_refs, l_refs, dils):
        if dil == 1:
            outs.append(o_ref.at[0, 0])
            lses.append(l_ref[0, 0])
            continue
        o_scr, l_scr = next(scratch), next(scratch)
        rows = tr // dil
        for r in range(dil):
            dst = pl.ds(r, rows, stride=dil)
            l_scr[dst, :] = l_ref[0, r]
            for h in range(n_heads):
                o_scr[h, dst, :] = o_ref[0, r, :, h * _HEAD_DIM:(h + 1) * _HEAD_DIM]
        outs.append(o_scr)
        lses.append(l_scr[...])
    for h in range(n_heads):
        sl = slice(h * _HEAD_DIM, (h + 1) * _HEAD_DIM)
        lh = [x[:, h:h + 1] for x in lses]
        mx = functools.reduce(jnp.maximum, lh)
        ws = [jnp.exp(x - mx) for x in lh]
        num = sum(wg * (o[:, sl] if dil == 1 else o[h]) for wg, o, dil in zip(ws, outs, dils))
        out_ref[0, :, sl] = (num / sum(ws)).astype(out_ref.dtype)


def _merge_groups(outs, lses, dils, n_heads, b, s, tr=256):
    w = outs[0].shape[-1]
    tr = min(tr, s)
    assert all(tr % (dil * _SUBLANES) == 0 for dil in dils)

    def spec(dil, width):
        return pl.BlockSpec((1, dil, tr // dil, width), lambda bi, i: (bi, 0, i, 0))

    view = lambda x, dil: x.reshape(b, dil, s // dil, x.shape[-1])
    scratch = []
    for dil in dils:
        if dil != 1:
            scratch += [pltpu.VMEM((n_heads, tr, _HEAD_DIM), _F32), pltpu.VMEM((tr, _LANES), _F32)]
    return pl.pallas_call(
        functools.partial(_merge_kernel, dils=dils, n_heads=n_heads),
        grid=(b, s // tr),
        in_specs=[spec(dil, w) for dil in dils] + [spec(dil, _LANES) for dil in dils],
        out_specs=pl.BlockSpec((1, tr, w), lambda bi, i: (bi, i, 0)),
        out_shape=jax.ShapeDtypeStruct((b, s, w), _BF16),
        scratch_shapes=scratch,
        compiler_params=_params(("parallel", "parallel")),
        name="dilated_merge",
    )(*[view(o, dil) for o, dil in zip(outs, dils)], *[view(x, dil) for x, dil in zip(lses, dils)])


def _dilated_mixer(h, hb, w_in, w_out, layer, b, s):
    d = h.shape[1]
    n_heads = d // _HEAD_DIM // 2
    w = n_heads * _HEAD_DIM
    wanted = tuple(dil for dil in _DIL_RATES if dil != 1 or hb is None)
    copies = dict(zip(wanted, _deinterleave(h.reshape(b, s, d), wanted)))
    if hb is None:
        hb = copies[1].reshape(b * s, d)
    col_scale = jnp.ones((len(_DIL_RATES), 3, w), _F32).at[:, 0].set(_LOG2E / math.sqrt(_HEAD_DIM))
    col_scale = col_scale.reshape(-1)
    outs, lses = [], []
    for g, dil in enumerate(_DIL_RATES):
        rows = hb if dil == 1 else copies[dil].reshape(b * s, d)
        proj = _matmul(rows, w_in, layer, _BF16, col_start=g * 3 * w, n_cols=3 * w, col_scale=col_scale)
        o, lse = _dilated_group(proj, g, n_heads, b, s)
        outs.append(o)
        lses.append(lse)
    merged = _merge_groups(outs, lses, _DIL_RATES, n_heads, b, s)
    return _matmul(merged.reshape(b * s, w), w_out, layer, _F32)


def _fox_gate_kernel(wf_ref, h_ref, bias_ref, c_ref, carry_ref):
    @pl.when(pl.program_id(1) == 0)
    def _():
        carry_ref[...] = jnp.zeros_like(carry_ref)

    z = _dot_nt(wf_ref[...], h_ref[0]) + bias_ref[...]
    log_f = jnp.minimum(z, 0.0) - jnp.log1p(jnp.exp(-jnp.abs(z)))
    log_f = log_f * _LOG2E
    nh, ts = log_f.shape
    lane = lax.broadcasted_iota(jnp.int32, (nh, _LANES), 1)
    carry = carry_ref[...]
    for c in range(ts // _LANES):
        x = log_f[:, c * _LANES:(c + 1) * _LANES]
        shift = 1
        while shift < _LANES:
            x = x + jnp.where(lane >= shift, pltpu.roll(x, shift, 1), 0.0)
            shift *= 2
        x = x + carry
        c_ref[0, :, c * _LANES:(c + 1) * _LANES] = x
        carry = jnp.broadcast_to(x[:, _LANES - 1:_LANES], (nh, _LANES))
    carry_ref[...] = carry


def _fox_gate(hb3, wf_t, bias, ts=512):
    b, s, d = hb3.shape
    nh = wf_t.shape[0]
    ts = min(ts, s)
    return pl.pallas_call(
        _fox_gate_kernel,
        grid=(b, s // ts),
        in_specs=[pl.BlockSpec((nh, d), lambda bi, si: (0, 0)),
                  pl.BlockSpec((1, ts, d), lambda bi, si: (bi, si, 0)),
                  pl.BlockSpec((nh, 1), lambda bi, si: (0, 0))],
        out_specs=pl.BlockSpec((1, nh, ts), lambda bi, si: (bi, 0, si)),
        out_shape=jax.ShapeDtypeStruct((b, nh, s), _F32),
        scratch_shapes=[pltpu.VMEM((nh, _LANES), _F32)],
        compiler_params=_params(("parallel", "arbitrary")),
        name="fox_gate_cumsum",
    )(wf_t, hb3, bias.reshape(nh, 1).astype(_F32))


def _fox_kernel(q_ref, k_ref, v_ref, c_ref, o_ref, m_ref, acc_ref, *, tq, heads):
    qi = pl.program_id(2)
    hd = _HEAD_DIM
    m_ref[...] = jnp.full_like(m_ref, _NEG)
    acc_ref[...] = jnp.zeros_like(acc_ref)
    ones = jnp.ones((tq, hd), _BF16)

    def step(kb, diagonal):
        k_start = pl.multiple_of(kb * tq, tq)
        sls = [slice(g * hd, (g + 1) * hd) for g in range(heads)]
        logits = []
        for g, sl in enumerate(sls):
            s = _dot_nt(q_ref[0, :, sl], k_ref[0, pl.ds(k_start, tq), sl]) - c_ref[0, g, :, pl.ds(k_start, tq)]
            if diagonal:
                row = lax.broadcasted_iota(jnp.int32, (tq, tq), 0)
                col = lax.broadcasted_iota(jnp.int32, (tq, tq), 1)
                s = jnp.where(col <= row, s, _NEG)
            logits.append(s)
        probs, alphas = [], []
        for g, s in enumerate(logits):
            m_old = m_ref[g]
            m_new = jnp.maximum(m_old, jnp.max(s, axis=1, keepdims=True))
            m_ref[g] = m_new
            alphas.append(jnp.exp2(m_old - m_new))
            probs.append(jnp.exp2(s - jnp.concatenate([m_new] * (tq // hd), axis=1)).astype(_BF16))
        for g, (sl, p, alpha) in enumerate(zip(sls, probs, alphas)):
            v1 = jnp.concatenate([v_ref[0, pl.ds(k_start, tq), sl], ones], axis=1)
            acc_ref[g] = (jnp.concatenate([alpha, alpha], axis=1) * acc_ref[g]
                          + jnp.dot(p, v1, preferred_element_type=_F32))

    lax.fori_loop(0, qi, lambda kb, carry: step(kb, False), None)
    step(qi, True)
    for g in range(heads):
        o_ref[0, :, g * hd:(g + 1) * hd] = (acc_ref[g, :, :hd] / acc_ref[g, :, hd:]).astype(o_ref.dtype)


def _fox_attention(qkv, c, n_heads, tq=512, heads=4):
    b, s, _ = qkv.shape
    tq = min(tq, s)
    heads = min(heads, n_heads)
    assert n_heads % heads == 0 and tq % _HEAD_DIM == 0
    ng = n_heads // heads
    gw = heads * _HEAD_DIM
    return pl.pallas_call(
        functools.partial(_fox_kernel, tq=tq, heads=heads),
        grid=(b, ng, s // tq),
        in_specs=[pl.BlockSpec((1, tq, gw), lambda bi, h, qi: (bi, qi, h)),
                  pl.BlockSpec((1, s, gw), lambda bi, h, qi: (bi, 0, ng + h)),
                  pl.BlockSpec((1, s, gw), lambda bi, h, qi: (bi, 0, 2 * ng + h)),
                  pl.BlockSpec((1, heads, 1, s), lambda bi, h, qi: (bi, h, 0, 0))],
        out_specs=pl.BlockSpec((1, tq, gw), lambda bi, h, qi: (bi, qi, h)),
        out_shape=jax.ShapeDtypeStruct((b, s, n_heads * _HEAD_DIM), _BF16),
        scratch_shapes=[pltpu.VMEM((heads, tq, _HEAD_DIM), _F32),
                        pltpu.VMEM((heads, tq, 2 * _HEAD_DIM), _F32)],
        compiler_params=_params(("parallel", "parallel", "parallel")),
        name="fox_attention",
    )(qkv, qkv, qkv, c.reshape(b, n_heads, 1, s))


def _forgetting_mixer(hb, w_in, f_bias, w_out, layer, b, s):
    d = hb.shape[1]
    n_heads = d // _HEAD_DIM
    width = n_heads * _HEAD_DIM
    q_scale = _LOG2E / math.sqrt(_HEAD_DIM)
    col_scale = jnp.ones((3, width), _F32).at[0].set(q_scale).reshape(-1)
    qkv = _matmul(hb, w_in, layer, _BF16, n_cols=3 * width, col_scale=col_scale).reshape(b, s, 3 * width)
    wf_t = w_in[layer, :, 3 * width:].T.astype(_BF16)
    c = _fox_gate(hb.reshape(b, s, d), wf_t, f_bias)
    o = _fox_attention(qkv, c, n_heads)
    return _matmul(o.reshape(b * s, width), w_out, layer, _F32)


def _sorting_network(n):
    comps = []

    def merge(lo, m, r):
        step = 2 * r
        if step < m:
            merge(lo, m, step)
            merge(lo + r, m, step)
            comps.extend((i, i + r) for i in range(lo + r, lo + m - r, step))
        else:
            comps.append((lo, lo + r))

    def sort(lo, m):
        if m > 1:
            sort(lo, m // 2)
            sort(lo + m // 2, m // 2)
            merge(lo, m, 1)

    sort(0, n)
    return comps


def _pop_columns(cols, count, singles=None):
    r, t = cols[0].shape
    row = lax.broadcasted_iota(jnp.int32, (r, t), 0)
    ninf = jnp.full((r, t), -jnp.inf, _F32)
    cols = list(cols)
    if singles is not None:
        r2 = singles.shape[0]
        row2 = lax.broadcasted_iota(jnp.int32, (r2, t), 0) + r
    vals = []
    for k in range(count):
        head = cols[0]
        m = jnp.max(head, axis=0, keepdims=True)
        if singles is not None:
            m = jnp.maximum(m, jnp.max(singles, axis=0, keepdims=True))
        vals.append(m)
        left = count - 1 - k
        if left == 0:
            break
        none = r + (0 if singles is None else singles.shape[0])
        first = jnp.min(jnp.where(head == m, row, none), axis=0, keepdims=True)
        if singles is not None:
            first = jnp.minimum(first, jnp.min(jnp.where(singles == m, row2, none),
                                               axis=0, keepdims=True))
            singles = jnp.where(row2 == first, -jnp.inf, singles)
        hit = row == first
        depth = min(left, len(cols))
        cols = [jnp.where(hit, cols[d + 1] if d + 1 < len(cols) else ninf, cols[d])
                for d in range(depth)]
    return vals


def _top_values(scores, count):
    n = scores.shape[0]
    stacks = [scores[v * _SUBLANES:(v + 1) * _SUBLANES] for v in range(n // _SUBLANES)]
    for i, j in _sorting_network(len(stacks)):
        stacks[i], stacks[j] = jnp.maximum(stacks[i], stacks[j]), jnp.minimum(stacks[i], stacks[j])
    return _pop_columns(stacks, count)


def _stack_rows(rows_1t, n_rows):
    t = rows_1t[0].shape[1]
    row = lax.broadcasted_iota(jnp.int32, (n_rows, t), 0)
    out = jnp.full((n_rows, t), -jnp.inf, _F32)
    for k, v in enumerate(rows_1t):
        out = jnp.where(row == k, v, out)
    return out


def _store_tile_rows(ref, h, x, ni):
    per = _SUBLANES // ni
    for g in range(x.shape[0] // _SUBLANES):
        grp = x[g * _SUBLANES:(g + 1) * _SUBLANES]
        for j in range(per):
            ref[h, g * per + j] = grp if j == 0 else pltpu.roll(grp, _SUBLANES - j * ni, 0)


def _peer_gate_kernel(q_ref, keys_ref, thr_ref, e0_ref, s1_ref, e1z_ref, *, n_heads, topk, ni):
    half = keys_ref.shape[2]
    tt = q_ref.shape[0]
    k0 = keys_ref[0]
    k1 = keys_ref[1]
    nkeep = topk + 1
    rows = _SUBLANES * (-(-nkeep // _SUBLANES))
    row = lax.broadcasted_iota(jnp.int32, (rows, tt), 0)
    for h in range(n_heads):
        s0 = _dot_nt(k0, q_ref[:, (2 * h) * half:(2 * h + 1) * half])
        s1 = _dot_nt(k1, q_ref[:, (2 * h + 1) * half:(2 * h + 2) * half])
        a_vals = _top_values(s0, nkeep)
        b_vals = _top_values(s1, nkeep)
        a_rows = _stack_rows(a_vals, rows)
        stacks = [jnp.where(row[:_SUBLANES] < nkeep // (j + 1), a_rows[:_SUBLANES] + b_vals[j],
                            -jnp.inf) for j in range(nkeep)]
        assert nkeep // 2 <= _SUBLANES
        singles = jnp.where(row[_SUBLANES:] < nkeep, a_rows[_SUBLANES:] + b_vals[0], -jnp.inf)
        top = _pop_columns(stacks, nkeep, singles)
        z = sum(jnp.exp(tv - top[0]) for tv in top[:topk])
        thr = 0.5 * (top[topk - 1] + top[topk])
        _store_tile_rows(thr_ref, h, thr - s0, ni)
        _store_tile_rows(e0_ref, h, jnp.exp(s0 - a_vals[0]), ni)
        s1_ref[h] = s1
        e1z_ref[h] = jnp.exp(s1 - b_vals[0]) / z


def _peer_gate(q, sub_keys, n_heads, ni, tt=128):
    t = q.shape[0]
    nk, half = sub_keys.shape[1:]
    tt = min(tt, t)
    assert _SUBLANES % ni == 0 and nk % _SUBLANES == 0
    n_tiles = nk // ni
    row_spec = pl.BlockSpec((n_heads, n_tiles, _SUBLANES, tt), lambda i: (0, 0, 0, i))
    row_shape = jax.ShapeDtypeStruct((n_heads, n_tiles, _SUBLANES, t), _F32)
    col_spec = pl.BlockSpec((n_heads, nk, tt), lambda i: (0, 0, i))
    col_shape = jax.ShapeDtypeStruct((n_heads, nk, t), _F32)
    return pl.pallas_call(
        functools.partial(_peer_gate_kernel, n_heads=n_heads, topk=_PEER_TOPK, ni=ni),
        grid=(t // tt,),
        in_specs=[pl.BlockSpec((tt, q.shape[1]), lambda i: (i, 0)),
                  pl.BlockSpec((2, nk, half), lambda i: (0, 0, 0))],
        out_specs=[row_spec, row_spec, col_spec, col_spec],
        out_shape=[row_shape, row_shape, col_shape, col_shape],
        compiler_params=_params(("parallel",)),
        name="peer_gate",
    )(q, sub_keys)


def _gelu(x):
    return 0.5 * x * (1.0 + lax.erf(x * (1.0 / math.sqrt(2.0))))


def _peer_kernel(hb_ref, u_ref, vt_ref, thr_ref, e0_ref, s1_ref, e1z_ref, o_ref, a_ref, w_ref,
                 *, ni, n_heads, n_tiles, n_work):
    k = pl.program_id(0)

    @pl.when(k == 0)
    def _():
        a_ref[...] = jnp.zeros_like(a_ref)

    @pl.when(jnp.clip(k - 1, 0, n_work - 1) % n_tiles == 0)
    def _():
        o_ref[...] = jnp.zeros_like(o_ref)

    nk = s1_ref.shape[1]
    tt = hb_ref.shape[0]
    gate_rows = 16
    groups = 4

    def gate_block(ii):
        for tc in range(tt // _LANES):
            cs = slice(tc * _LANES, (tc + 1) * _LANES)
            for j0 in range(0, nk, gate_rows):
                js = slice(j0, j0 + gate_rows)
                gate = jnp.zeros((gate_rows, _LANES), _F32)
                for h in range(n_heads):
                    thr = thr_ref[h, ii:ii + 1, cs]
                    e0 = e0_ref[h, ii:ii + 1, cs]
                    gate = gate + jnp.where(s1_ref[h, js, cs] >= thr, e1z_ref[h, js, cs] * e0, 0.0)
                rs = slice(ii * nk + j0, ii * nk + j0 + gate_rows)
                w_ref[rs, cs] = (_gelu(a_ref[rs, cs]) * gate).astype(_BF16)

    per = ni // groups
    for g in range(groups):
        for ii in range(g * per, (g + 1) * per):
            gate_block(ii)
        es = slice(g * per * nk, (g + 1) * per * nk)
        o_ref[...] += jnp.dot(vt_ref[:, es], w_ref[es, :], preferred_element_type=_F32)
    a_ref[...] = _dot_nt(u_ref[...], hb_ref[...])


def _peer_expert_slabs(u, v, ni, nk):
    layers, _, d = v.shape
    et = ni * nk
    vt = v.astype(_BF16).reshape(layers, -1, et, d).transpose(0, 1, 3, 2)
    return u.astype(_BF16), vt


def _peer_dense(hb, u, vt, layer, thr, e0, s1, e1z, n_heads, tt=512, ni=4):
    t, d = hb.shape
    nk = s1.shape[1]
    tt = min(tt, t)
    et = ni * nk
    n_tiles = nk // ni
    n_work = (t // tt) * n_tiles

    def stage(lag):
        return lambda k: jnp.clip(k - lag, 0, n_work - 1)

    once = pl.Buffered(1)
    gate_spec = pl.BlockSpec((n_heads, nk, tt), lambda k: (0, 0, stage(1)(k) // n_tiles),
                             pipeline_mode=once)
    row_spec = pl.BlockSpec((n_heads, None, _SUBLANES, tt),
                            lambda k: (0, stage(1)(k) % n_tiles, 0, stage(1)(k) // n_tiles))
    return pl.pallas_call(
        functools.partial(_peer_kernel, ni=ni, n_heads=n_heads, n_tiles=n_tiles, n_work=n_work),
        grid=(n_work + 1,),
        in_specs=[pl.BlockSpec((tt, d), lambda k: (stage(0)(k) // n_tiles, 0), pipeline_mode=once),
                  pl.BlockSpec((None, et, d), lambda k: (layer, stage(0)(k) % n_tiles, 0)),
                  pl.BlockSpec((None, None, d, et), lambda k: (layer, stage(1)(k) % n_tiles, 0, 0)),
                  row_spec, row_spec, gate_spec, gate_spec],
        out_specs=pl.BlockSpec((d, tt), lambda k: (0, stage(1)(k) // n_tiles), pipeline_mode=once),
        out_shape=jax.ShapeDtypeStruct((d, t), _F32),
        scratch_shapes=[pltpu.VMEM((et, tt), _F32), pltpu.VMEM((et, tt), _BF16)],
        compiler_params=_params(("arbitrary",)),
        name="peer_dense",
    )(hb, u, vt, thr, e0, s1, e1z)


_PEER_TILE_KEYS = 8


def _peer_ffn(hb, w_q, layer, sub_keys, u, vt):
    half = sub_keys.shape[2]
    n_heads = w_q.shape[2] // (2 * half)
    q = _matmul(hb, w_q, layer, _BF16)
    ni = _PEER_TILE_KEYS
    thr, e0, s1, e1z = _peer_gate(q, sub_keys.astype(_BF16), n_heads, ni)
    return _peer_dense(hb, u, vt, layer, thr, e0, s1, e1z, n_heads, ni=ni)


def kernel(x, a_w_in, a_w_out, b_w_in, b_f_bias, b_w_out, peer_w_q, peer_sub_keys, peer_u, peer_v,
           ln_mix_g, ln_mix_b, ln_ffn_g, ln_ffn_b):
    b, s, d = x.shape
    depth = ln_mix_g.shape[0]
    alpha = (2 * depth) ** 0.25
    h = x.reshape(b * s, d)
    hb = None
    u_b, vt_b = _peer_expert_slabs(peer_u, peer_v, _PEER_TILE_KEYS, peer_sub_keys.shape[2])
    for i in range(depth):
        j = i // 2
        if i % 2 == 0:
            mix = _dilated_mixer(h, hb, a_w_in, a_w_out, j, b, s)
        else:
            mix = _forgetting_mixer(hb, b_w_in, b_f_bias[j], b_w_out, j, b, s)
        h, hb = _ln_residual(h, mix, ln_mix_g[i], ln_mix_b[i], alpha)
        ffn_t = _peer_ffn(hb, peer_w_q, i, peer_sub_keys[i], u_b, vt_b)
        h, hb = _ln_residual(h, ffn_t, ln_ffn_g[i], ln_ffn_b[i], alpha, transposed_mix=True)
    return h.reshape(b, s, d)
```

```python
import functools
import math

import jax
import jax.numpy as jnp
from jax import lax
from jax.experimental import pallas as pl
from jax.experimental.pallas import tpu as pltpu

_F32 = jnp.float32
_BF16 = jnp.bfloat16

_HEAD_DIM = 128
_DIL_WINDOWS = (128, 512, 2048)
_DIL_RATES = (1, 4, 16)
_PEER_TOPK = 16
_LN_EPS = 1e-5
_NEG = -1e30
_LANES = 128
_SUBLANES = 8
_LOG2E = math.log2(math.e)
_VMEM_LIMIT = 60 * 1024 * 1024


def _dot_nt(a, b):
    return lax.dot_general(a, b, (((1,), (1,)), ((), ())), preferred_element_type=_F32)


def _params(sem):
    return pltpu.CompilerParams(dimension_semantics=sem, vmem_limit_bytes=_VMEM_LIMIT)


def _mm_kernel(a_ref, w_ref, *rest, scaled):
    if scaled:
        s_ref, o_ref, wb_ref = rest
    else:
        o_ref, wb_ref = rest

    @pl.when(pl.program_id(1) == 0)
    def _():
        w = w_ref[...]
        if scaled:
            w = w * s_ref[...]
        wb_ref[...] = w.astype(wb_ref.dtype)

    o_ref[...] = jnp.dot(a_ref[...], wb_ref[...], preferred_element_type=_F32).astype(o_ref.dtype)


def _matmul(a, w, layer, out_dtype, tm=1024, tn=512, col_start=0, n_cols=None, col_scale=None):
    m, k = a.shape
    n = w.shape[2] if n_cols is None else n_cols
    tm = min(tm, m)
    tn = min(tn, n)
    while n % tn or col_start % tn:
        tn -= _LANES
    assert m % tm == 0 and tn > 0
    first = col_start // tn
    w_spec = pl.BlockSpec((None, k, tn), lambda j, i: (layer, 0, j + first))
    operands = [a, w]
    in_specs = [pl.BlockSpec((tm, k), lambda j, i: (i, 0)), w_spec]
    if col_scale is not None:
        operands.append(col_scale.reshape(1, -1).astype(_F32))
        in_specs.append(pl.BlockSpec((1, tn), lambda j, i: (0, j + first)))
    return pl.pallas_call(
        functools.partial(_mm_kernel, scaled=col_scale is not None),
        grid=(n // tn, m // tm),
        in_specs=in_specs,
        out_specs=pl.BlockSpec((tm, tn), lambda j, i: (i, j)),
        out_shape=jax.ShapeDtypeStruct((m, n), out_dtype),
        scratch_shapes=[pltpu.VMEM((k, tn), _BF16)],
        compiler_params=_params(("parallel", "arbitrary")),
        name="matmul",
    )(*operands)


def _ln_body(y, g_ref, b_ref, o_ref, ob_ref):
    mu = jnp.mean(y, axis=-1, keepdims=True)
    d = y - mu
    var = jnp.mean(d * d, axis=-1, keepdims=True)
    out = d * lax.rsqrt(var + _LN_EPS) * g_ref[...] + b_ref[...]
    o_ref[...] = out
    ob_ref[...] = out.astype(_BF16)


def _ln_kernel(h_ref, m_ref, g_ref, b_ref, o_ref, ob_ref, *, alpha):
    _ln_body(alpha * h_ref[...] + m_ref[...], g_ref, b_ref, o_ref, ob_ref)


def _ln_t_kernel(h_ref, mt_ref, g_ref, b_ref, o_ref, ob_ref, *, alpha):
    _ln_body(alpha * h_ref[...] + mt_ref[...].T, g_ref, b_ref, o_ref, ob_ref)


def _ln_residual(h, mix, g, b, alpha, transposed_mix=False, tr=256):
    t, d = h.shape
    tr = min(tr, t)
    assert t % tr == 0
    if transposed_mix:
        body = _ln_t_kernel
        mix_spec = pl.BlockSpec((d, tr), lambda i: (0, i))
    else:
        body = _ln_kernel
        mix_spec = pl.BlockSpec((tr, d), lambda i: (i, 0))
    row = pl.BlockSpec((tr, d), lambda i: (i, 0))
    vec = pl.BlockSpec((1, d), lambda i: (0, 0))
    return pl.pallas_call(
        functools.partial(body, alpha=alpha),
        grid=(t // tr,),
        in_specs=[row, mix_spec, vec, vec],
        out_specs=[row, row],
        out_shape=[jax.ShapeDtypeStruct((t, d), _F32), jax.ShapeDtypeStruct((t, d), _BF16)],
        compiler_params=_params(("parallel",)),
        name="ln_residual",
    )(h, mix, g.reshape(1, d), b.reshape(1, d))


def _dil_kernel(*refs, dil, n_heads, with_prev):
    if with_prev:
        q_ref, kc_ref, kp_ref, vc_ref, vp_ref, o_ref, lse_ref = refs
    else:
        q_ref, kc_ref, vc_ref, o_ref, lse_ref = refs
    n = _LANES
    hd = _HEAD_DIM
    nkeys = 2 * n if with_prev else n
    row = lax.broadcasted_iota(jnp.int32, (n, nkeys), 0)
    col = lax.broadcasted_iota(jnp.int32, (n, nkeys), 1)
    dist = (nkeys - n) + row - col
    valid = (dist >= 0) & (dist <= n)
    if with_prev:
        valid = valid & ((col >= n) | (pl.program_id(1) > 0))
    key_pos = (lax.broadcasted_iota(jnp.int32, (1, nkeys), 1) - (nkeys - n)).astype(_F32)
    query_pos = lax.broadcasted_iota(jnp.int32, (n, 1), 0).astype(_F32)
    lane = lax.broadcasted_iota(jnp.int32, (n, n), 1)
    ones = jnp.ones((nkeys, hd), _BF16)

    def keys_of(cur_ref, prev_ref, sl):
        if with_prev:
            return jnp.concatenate([prev_ref[0, :, sl], cur_ref[0, :, sl]], axis=0)
        return cur_ref[0, :, sl]

    lse_tile = jnp.zeros((n, n), _F32)
    group = 4
    for h0 in range(0, n_heads, group):
        heads = range(h0, min(h0 + group, n_heads))
        slopes = [float(dil) * _LOG2E * 2.0 ** (-8.0 * (h + 1) / n_heads) for h in heads]
        sls = [slice(h * hd, (h + 1) * hd) for h in heads]
        logits = []
        for sl, c in zip(sls, slopes):
            k2 = keys_of(kc_ref, kp_ref if with_prev else None, sl)
            logits.append(jnp.where(valid, _dot_nt(q_ref[0, :, sl], k2) + c * key_pos, _NEG))
        probs, maxes = [], []
        for s in logits:
            mx = jnp.max(s, axis=1, keepdims=True)
            maxes.append(mx)
            probs.append(jnp.exp2(s - mx).astype(_BF16))
        for h, sl, c, p, mx in zip(heads, sls, slopes, probs, maxes):
            v2 = keys_of(vc_ref, vp_ref if with_prev else None, sl)
            r = jnp.dot(p, jnp.concatenate([v2, ones], axis=1), preferred_element_type=_F32)
            den = r[:, hd:hd + 1]
            o_ref[0, :, sl] = r[:, :hd] / den
            lse = (mx + jnp.log2(den) - c * query_pos) * (1.0 / _LOG2E)
            lse_tile = jnp.where(lane == h, lse, lse_tile)
    lse_ref[0] = lse_tile


def _deinterleave_kernel(*refs, dils, n_in):
    x_refs, o_refs = refs[:n_in], refs[n_in:]
    s = x_refs[0].shape[1]
    for o_ref, dil in zip(o_refs, dils):
        l = s // dil
        for j, x_ref in enumerate(x_refs):
            cs = slice(j * _LANES, (j + 1) * _LANES)
            for r in range(dil):
                o_ref[0, r * l:(r + 1) * l, cs] = x_ref[0, pl.ds(r, l, stride=dil), :].astype(o_ref.dtype)


def _deinterleave(h3, dils, slabs=4):
    b, s, d = h3.shape
    slabs = min(slabs, d // _LANES)
    assert d % (slabs * _LANES) == 0

    def in_spec(j):
        return pl.BlockSpec((1, s, _LANES), lambda bi, ci: (bi, 0, ci * slabs + j))

    out_spec = pl.BlockSpec((1, s, slabs * _LANES), lambda bi, ci: (bi, 0, ci))
    return pl.pallas_call(
        functools.partial(_deinterleave_kernel, dils=dils, n_in=slabs),
        grid=(b, d // (slabs * _LANES)),
        in_specs=[in_spec(j) for j in range(slabs)],
        out_specs=[out_spec] * len(dils),
        out_shape=[jax.ShapeDtypeStruct((b, s, d), _BF16)] * len(dils),
        compiler_params=_params(("parallel", "parallel")),
        name="deinterleave",
    )(*[h3] * slabs)


def _dilated_group(proj, g, n_heads, b, s):
    w = n_heads * _HEAD_DIM
    dil = _DIL_RATES[g]
    n = _DIL_WINDOWS[g] // dil
    assert n == _LANES and n_heads <= _LANES
    l = s // dil
    nb = l // n
    assert l % n == 0
    view = proj.reshape(b * dil, l, 3 * w)

    def spec(which, prev):
        def index(p, i):
            return (p, jnp.maximum(i - 1, 0) if prev else i, which)
        return pl.BlockSpec((1, n, w), index)

    with_prev = nb > 1
    in_specs = [spec(0, False), spec(1, False), spec(1, True), spec(2, False), spec(2, True)]
    if not with_prev:
        in_specs = [spec(0, False), spec(1, False), spec(2, False)]
    return pl.pallas_call(
        functools.partial(_dil_kernel, dil=dil, n_heads=n_heads, with_prev=with_prev),
        grid=(b * dil, nb),
        in_specs=in_specs,
        out_specs=[pl.BlockSpec((1, n, w), lambda p, i: (p, i, 0)),
                   pl.BlockSpec((1, n, _LANES), lambda p, i: (p, i, 0))],
        out_shape=[jax.ShapeDtypeStruct((b * dil, l, w), _F32),
                   jax.ShapeDtypeStruct((b * dil, l, _LANES), _F32)],
        compiler_params=_params(("parallel", "parallel")),
        name=f"dilated_attention_g{g}",
    )(*[view] * len(in_specs))


def _merge_kernel(*refs, dils, n_heads):
    ng = len(dils)
    o_refs, l_refs, out_ref = refs[:ng], refs[ng:2 * ng], refs[2 * ng]
    scratch = iter(refs[2 * ng + 1:])
    tr = out_ref.shape[1]
    outs, lses = [], []
    for o_ref, l_ref, dil in zip(o_refs, l_refs, dils):
        if dil == 1:
            outs.append(o_ref.at[0, 0])
            lses.append(l_ref[0, 0])
            continue
        o_scr, l_scr = next(scratch), next(scratch)
        rows = tr // dil
        for r in range(dil):
            dst = pl.ds(r, rows, stride=dil)
            l_scr[dst, :] = l_ref[0, r]
            for h in range(n_heads):
                o_scr[h, dst, :] = o_ref[0, r, :, h * _HEAD_DIM:(h + 1) * _HEAD_DIM]
        outs.append(o_scr)
        lses.append(l_scr[...])
    for h in range(n_heads):
        sl = slice(h * _HEAD_DIM, (h + 1) * _HEAD_DIM)
        lh = [x[:, h:h + 1] for x in lses]
        mx = functools.reduce(jnp.maximum, lh)
        ws = [jnp.exp(x - mx) for x in lh]
        num = sum(wg * (o[:, sl] if dil == 1 else o[h]) for wg, o, dil in zip(ws, outs, dils))
        out_ref[0, :, sl] = (num / sum(ws)).astype(out_ref.dtype)


def _merge_groups(outs, lses, dils, n_heads, b, s, tr=256):
    w = outs[0].shape[-1]
    tr = min(tr, s)
    assert all(tr % (dil * _SUBLANES) == 0 for dil in dils)

    def spec(dil, width):
        return pl.BlockSpec((1, dil, tr // dil, width), lambda bi, i: (bi, 0, i, 0))

    view = lambda x, dil: x.reshape(b, dil, s // dil, x.shape[-1])
    scratch = []
    for dil in dils:
        if dil != 1:
            scratch += [pltpu.VMEM((n_heads, tr, _HEAD_DIM), _F32), pltpu.VMEM((tr, _LANES), _F32)]
    return pl.pallas_call(
        functools.partial(_merge_kernel, dils=dils, n_heads=n_heads),
        grid=(b, s // tr),
        in_specs=[spec(dil, w) for dil in dils] + [spec(dil, _LANES) for dil in dils],
        out_specs=pl.BlockSpec((1, tr, w), lambda bi, i: (bi, i, 0)),
        out_shape=jax.ShapeDtypeStruct((b, s, w), _BF16),
        scratch_shapes=scratch,
        compiler_params=_params(("parallel", "parallel")),
        name="dilated_merge",
    )(*[view(o, dil) for o, dil in zip(outs, dils)], *[view(x, dil) for x, dil in zip(lses, dils)])


def _dilated_mixer(h, hb, w_in, w_out, layer, b, s):
    d = h.shape[1]
    n_heads = d // _HEAD_DIM // 2
    w = n_heads * _HEAD_DIM
    wanted = tuple(dil for dil in _DIL_RATES if dil != 1 or hb is None)
    copies = dict(zip(wanted, _deinterleave(h.reshape(b, s, d), wanted)))
    if hb is None:
        hb = copies[1].reshape(b * s, d)
    col_scale = jnp.ones((len(_DIL_RATES), 3, w), _F32).at[:, 0].set(_LOG2E / math.sqrt(_HEAD_DIM))
    col_scale = col_scale.reshape(-1)
    outs, lses = [], []
    for g, dil in enumerate(_DIL_RATES):
        rows = hb if dil == 1 else copies[dil].reshape(b * s, d)
        proj = _matmul(rows, w_in, layer, _BF16, col_start=g * 3 * w, n_cols=3 * w, col_scale=col_scale)
        o, lse = _dilated_group(proj, g, n_heads, b, s)
        outs.append(o)
        lses.append(lse)
    merged = _merge_groups(outs, lses, _DIL_RATES, n_heads, b, s)
    return _matmul(merged.reshape(b * s, w), w_out, layer, _F32)


def _fox_gate_kernel(wf_ref, h_ref, bias_ref, c_ref, carry_ref):
    @pl.when(pl.program_id(1) == 0)
    def _():
        carry_ref[...] = jnp.zeros_like(carry_ref)

    z = _dot_nt(wf_ref[...], h_ref[0]) + bias_ref[...]
    log_f = jnp.minimum(z, 0.0) - jnp.log1p(jnp.exp(-jnp.abs(z)))
    log_f = log_f * _LOG2E
    nh, ts = log_f.shape
    lane = lax.broadcasted_iota(jnp.int32, (nh, _LANES), 1)
    carry = carry_ref[...]
    for c in range(ts // _LANES):
        x = log_f[:, c * _LANES:(c + 1) * _LANES]
        shift = 1
        while shift < _LANES:
            x = x + jnp.where(lane >= shift, pltpu.roll(x, shift, 1), 0.0)
            shift *= 2
        x = x + carry
        c_ref[0, :, c * _LANES:(c + 1) * _LANES] = x
        carry = jnp.broadcast_to(x[:, _LANES - 1:_LANES], (nh, _LANES))
    carry_ref[...] = carry


def _fox_gate(hb3, wf_t, bias, ts=512):
    b, s, d = hb3.shape
    nh = wf_t.shape[0]
    ts = min(ts, s)
    return pl.pallas_call(
        _fox_gate_kernel,
        grid=(b, s // ts),
        in_specs=[pl.BlockSpec((nh, d), lambda bi, si: (0, 0)),
                  pl.BlockSpec((1, ts, d), lambda bi, si: (bi, si, 0)),
                  pl.BlockSpec((nh, 1), lambda bi, si: (0, 0))],
        out_specs=pl.BlockSpec((1, nh, ts), lambda bi, si: (bi, 0, si)),
        out_shape=jax.ShapeDtypeStruct((b, nh, s), _F32),
        scratch_shapes=[pltpu.VMEM((nh, _LANES), _F32)],
        compiler_params=_params(("parallel", "arbitrary")),
        name="fox_gate_cumsum",
    )(wf_t, hb3, bias.reshape(nh, 1).astype(_F32))


def _fox_kernel(q_ref, k_ref, v_ref, c_ref, o_ref, m_ref, acc_ref, *, tq, heads):
    qi = pl.program_id(2)
    hd = _HEAD_DIM
    m_ref[...] = jnp.full_like(m_ref, _NEG)
    acc_ref[...] = jnp.zeros_like(acc_ref)
    ones = jnp.ones((tq, hd), _BF16)

    def step(kb, diagonal):
        k_start = pl.multiple_of(kb * tq, tq)
        sls = [slice(g * hd, (g + 1) * hd) for g in range(heads)]
        logits = []
        for g, sl in enumerate(sls):
            s = _dot_nt(q_ref[0, :, sl], k_ref[0, pl.ds(k_start, tq), sl]) - c_ref[0, g, :, pl.ds(k_start, tq)]
            if diagonal:
                row = lax.broadcasted_iota(jnp.int32, (tq, tq), 0)
                col = lax.broadcasted_iota(jnp.int32, (tq, tq), 1)
                s = jnp.where(col <= row, s, _NEG)
            logits.append(s)
        probs, alphas = [], []
        for g, s in enumerate(logits):
            m_old = m_ref[g]
            m_new = jnp.maximum(m_old, jnp.max(s, axis=1, keepdims=True))
            m_ref[g] = m_new
            alphas.append(jnp.exp2(m_old - m_new))
            probs.append(jnp.exp2(s - jnp.concatenate([m_new] * (tq // hd), axis=1)).astype(_BF16))
        for g, (sl, p, alpha) in enumerate(zip(sls, probs, alphas)):
            v1 = jnp.concatenate([v_ref[0, pl.ds(k_start, tq), sl], ones], axis=1)
            acc_ref[g] = (jnp.concatenate([alpha, alpha], axis=1) * acc_ref[g]
                          + jnp.dot(p, v1, preferred_element_type=_F32))

    lax.fori_loop(0, qi, lambda kb, carry: step(kb, False), None)
    step(qi, True)
    for g in range(heads):
        o_ref[0, :, g * hd:(g + 1) * hd] = (acc_ref[g, :, :hd] / acc_ref[g, :, hd:]).astype(o_ref.dtype)


def _fox_attention(qkv, c, n_heads, tq=512, heads=4):
    b, s, _ = qkv.shape
    tq = min(tq, s)
    heads = min(heads, n_heads)
    assert n_heads % heads == 0 and tq % _HEAD_DIM == 0
    ng = n_heads // heads
    gw = heads * _HEAD_DIM
    return pl.pallas_call(
        functools.partial(_fox_kernel, tq=tq, heads=heads),
        grid=(b, ng, s // tq),
        in_specs=[pl.BlockSpec((1, tq, gw), lambda bi, h, qi: (bi, qi, h)),
                  pl.BlockSpec((1, s, gw), lambda bi, h, qi: (bi, 0, ng + h)),
                  pl.BlockSpec((1, s, gw), lambda bi, h, qi: (bi, 0, 2 * ng + h)),
                  pl.BlockSpec((1, heads, 1, s), lambda bi, h, qi: (bi, h, 0, 0))],
        out_specs=pl.BlockSpec((1, tq, gw), lambda bi, h, qi: (bi, qi, h)),
        out_shape=jax.ShapeDtypeStruct((b, s, n_heads * _HEAD_DIM), _BF16),
        scratch_shapes=[pltpu.VMEM((heads, tq, _HEAD_DIM), _F32),
                        pltpu.VMEM((heads, tq, 2 * _HEAD_DIM), _F32)],
        compiler_params=_params(("parallel", "parallel", "parallel")),
        name="fox_attention",
    )(qkv, qkv, qkv, c.reshape(b, n_heads, 1, s))


def _forgetting_mixer(hb, w_in, f_bias, w_out, layer, b, s):
    d = hb.shape[1]
    n_heads = d // _HEAD_DIM
    width = n_heads * _HEAD_DIM
    q_scale = _LOG2E / math.sqrt(_HEAD_DIM)
    col_scale = jnp.ones((3, width), _F32).at[0].set(q_scale).reshape(-1)
    qkv = _matmul(hb, w_in, layer, _BF16, n_cols=3 * width, col_scale=col_scale).reshape(b, s, 3 * width)
    wf_t = w_in[layer, :, 3 * width:].T.astype(_BF16)
    c = _fox_gate(hb.reshape(b, s, d), wf_t, f_bias)
    o = _fox_attention(qkv, c, n_heads)
    return _matmul(o.reshape(b * s, width), w_out, layer, _F32)


def _sorting_network(n):
    comps = []

    def merge(lo, m, r):
        step = 2 * r
        if step < m:
            merge(lo, m, step)
            merge(lo + r, m, step)
            comps.extend((i, i + r) for i in range(lo + r, lo + m - r, step))
        else:
            comps.append((lo, lo + r))

    def sort(lo, m):
        if m > 1:
            sort(lo, m // 2)
            sort(lo + m // 2, m // 2)
            merge(lo, m, 1)

    sort(0, n)
    return comps


def _pop_columns(cols, count, singles=None):
    r, t = cols[0].shape
    row = lax.broadcasted_iota(jnp.int32, (r, t), 0)
    ninf = jnp.full((r, t), -jnp.inf, _F32)
    cols = list(cols)
    if singles is not None:
        r2 = singles.shape[0]
        row2 = lax.broadcasted_iota(jnp.int32, (r2, t), 0) + r
    vals = []
    for k in range(count):
        head = cols[0]
        m = jnp.max(head, axis=0, keepdims=True)
        if singles is not None:
            m = jnp.maximum(m, jnp.max(singles, axis=0, keepdims=True))
        vals.append(m)
        left = count - 1 - k
        if left == 0:
            break
        none = r + (0 if singles is None else singles.shape[0])
        first = jnp.min(jnp.where(head == m, row, none), axis=0, keepdims=True)
        if singles is not None:
            first = jnp.minimum(first, jnp.min(jnp.where(singles == m, row2, none),
                                               axis=0, keepdims=True))
            singles = jnp.where(row2 == first, -jnp.inf, singles)
        hit = row == first
        depth = min(left, len(cols))
        cols = [jnp.where(hit, cols[d + 1] if d + 1 < len(cols) else ninf, cols[d])
                for d in range(depth)]
    return vals


def _top_values(scores, count):
    n = scores.shape[0]
    stacks = [scores[v * _SUBLANES:(v + 1) * _SUBLANES] for v in range(n // _SUBLANES)]
    for i, j in _sorting_network(len(stacks)):
        stacks[i], stacks[j] = jnp.maximum(stacks[i], stacks[j]), jnp.minimum(stacks[i], stacks[j])
    return _pop_columns(stacks, count)


def _stack_rows(rows_1t, n_rows):
    t = rows_1t[0].shape[1]
    row = lax.broadcasted_iota(jnp.int32, (n_rows, t), 0)
    out = jnp.full((n_rows, t), -jnp.inf, _F32)
    for k, v in enumerate(rows_1t):
        out = jnp.where(row == k, v, out)
    return out


def _store_tile_rows(ref, h, x, ni):
    per = _SUBLANES // ni
    for g in range(x.shape[0] // _SUBLANES):
        grp = x[g * _SUBLANES:(g + 1) * _SUBLANES]
        for j in range(per):
            ref[h, g * per + j] = grp if j == 0 else pltpu.roll(grp, _SUBLANES - j * ni, 0)


def _peer_gate_kernel(q_ref, keys_ref, thr_ref, e0_ref, s1_ref, e1z_ref, *, n_heads, topk, ni):
    half = keys_ref.shape[2]
    tt = q_ref.shape[0]
    k0 = keys_ref[0]
    k1 = keys_ref[1]
    nkeep = topk + 1
    rows = _SUBLANES * (-(-nkeep // _SUBLANES))
    row = lax.broadcasted_iota(jnp.int32, (rows, tt), 0)
    for h in range(n_heads):
        s0 = _dot_nt(k0, q_ref[:, (2 * h) * half:(2 * h + 1) * half])
        s1 = _dot_nt(k1, q_ref[:, (2 * h + 1) * half:(2 * h + 2) * half])
        a_vals = _top_values(s0, nkeep)
        b_vals = _top_values(s1, nkeep)
        a_rows = _stack_rows(a_vals, rows)
        stacks = [jnp.where(row[:_SUBLANES] < nkeep // (j + 1), a_rows[:_SUBLANES] + b_vals[j],
                            -jnp.inf) for j in range(nkeep)]
        assert nkeep // 2 <= _SUBLANES
        singles = jnp.where(row[_SUBLANES:] < nkeep, a_rows[_SUBLANES:] + b_vals[0], -jnp.inf)
        top = _pop_columns(stacks, nkeep, singles)
        z = sum(jnp.exp(tv - top[0]) for tv in top[:topk])
        thr = 0.5 * (top[topk - 1] + top[topk])
        _store_tile_rows(thr_ref, h, thr - s0, ni)
        _store_tile_rows(e0_ref, h, jnp.exp(s0 - a_vals[0]), ni)
        s1_ref[h] = s1
        e1z_ref[h] = jnp.exp(s1 - b_vals[0]) / z


def _peer_gate(q, sub_keys, n_heads, ni, tt=128):
    t = q.shape[0]
    nk, half = sub_keys.shape[1:]
    tt = min(tt, t)
    assert _SUBLANES % ni == 0 and nk % _SUBLANES == 0
    n_tiles = nk // ni
    row_spec = pl.BlockSpec((n_heads, n_tiles, _SUBLANES, tt), lambda i: (0, 0, 0, i))
    row_shape = jax.ShapeDtypeStruct((n_heads, n_tiles, _SUBLANES, t), _F32)
    col_spec = pl.BlockSpec((n_heads, nk, tt), lambda i: (0, 0, i))
    col_shape = jax.ShapeDtypeStruct((n_heads, nk, t), _F32)
    return pl.pallas_call(
        functools.partial(_peer_gate_kernel, n_heads=n_heads, topk=_PEER_TOPK, ni=ni),
        grid=(t // tt,),
        in_specs=[pl.BlockSpec((tt, q.shape[1]), lambda i: (i, 0)),
                  pl.BlockSpec((2, nk, half), lambda i: (0, 0, 0))],
        out_specs=[row_spec, row_spec, col_spec, col_spec],
        out_shape=[row_shape, row_shape, col_shape, col_shape],
        compiler_params=_params(("parallel",)),
        name="peer_gate",
    )(q, sub_keys)


def _gelu(x):
    return 0.5 * x * (1.0 + lax.erf(x * (1.0 / math.sqrt(2.0))))


def _peer_kernel(hb_ref, u_ref, vt_ref, thr_ref, e0_ref, s1_ref, e1z_ref, o_ref, a_ref, w_ref,
                 *, ni, n_heads, n_tiles, n_work):
    k = pl.program_id(0)

    @pl.when(k == 0)
    def _():
        a_ref[...] = jnp.zeros_like(a_ref)

    @pl.when(jnp.clip(k - 1, 0, n_work - 1) % n_tiles == 0)
    def _():
        o_ref[...] = jnp.zeros_like(o_ref)

    nk = s1_ref.shape[1]
    tt = hb_ref.shape[0]
    gate_rows = 16
    group_sizes = (2, ni - 2) if ni > 2 else (ni,)

    def gate_block(ii):
        for tc in range(tt // _LANES):
            cs = slice(tc * _LANES, (tc + 1) * _LANES)
            for j0 in range(0, nk, gate_rows):
                js = slice(j0, j0 + gate_rows)
                gate = jnp.zeros((gate_rows, _LANES), _F32)
                for h in range(n_heads):
                    thr = thr_ref[h, ii:ii + 1, cs]
                    e0 = e0_ref[h, ii:ii + 1, cs]
                    gate = gate + jnp.where(s1_ref[h, js, cs] >= thr, e1z_ref[h, js, cs] * e0, 0.0)
                rs = slice(ii * nk + j0, ii * nk + j0 + gate_rows)
                w_ref[rs, cs] = (_gelu(a_ref[rs, cs]) * gate).astype(_BF16)

    first = 0
    for size in group_sizes:
        for ii in range(first, first + size):
            gate_block(ii)
        es = slice(first * nk, (first + size) * nk)
        o_ref[...] += jnp.dot(vt_ref[:, es], w_ref[es, :], preferred_element_type=_F32)
        first += size
    a_ref[...] = _dot_nt(u_ref[...], hb_ref[...])


def _peer_expert_slabs(u, v, ni, nk):
    layers, _, d = v.shape
    et = ni * nk
    vt = v.astype(_BF16).reshape(layers, -1, et, d).transpose(0, 1, 3, 2)
    return u.astype(_BF16), vt


def _peer_dense(hb, u, vt, layer, thr, e0, s1, e1z, n_heads, tt=512, ni=4):
    t, d = hb.shape
    nk = s1.shape[1]
    tt = min(tt, t)
    et = ni * nk
    n_tiles = nk // ni
    n_work = (t // tt) * n_tiles

    def stage(lag):
        return lambda k: jnp.clip(k - lag, 0, n_work - 1)

    once = pl.Buffered(1)
    gate_spec = pl.BlockSpec((n_heads, nk, tt), lambda k: (0, 0, stage(1)(k) // n_tiles),
                             pipeline_mode=once)
    row_spec = pl.BlockSpec((n_heads, None, _SUBLANES, tt),
                            lambda k: (0, stage(1)(k) % n_tiles, 0, stage(1)(k) // n_tiles))
    return pl.pallas_call(
        functools.partial(_peer_kernel, ni=ni, n_heads=n_heads, n_tiles=n_tiles, n_work=n_work),
        grid=(n_work + 1,),
        in_specs=[pl.BlockSpec((tt, d), lambda k: (stage(0)(k) // n_tiles, 0), pipeline_mode=once),
                  pl.BlockSpec((None, et, d), lambda k: (layer, stage(0)(k) % n_tiles, 0)),
                  pl.BlockSpec((None, None, d, et), lambda k: (layer, stage(1)(k) % n_tiles, 0, 0)),
                  row_spec, row_spec, gate_spec, gate_spec],
        out_specs=pl.BlockSpec((d, tt), lambda k: (0, stage(1)(k) // n_tiles), pipeline_mode=once),
        out_shape=jax.ShapeDtypeStruct((d, t), _F32),
        scratch_shapes=[pltpu.VMEM((et, tt), _F32), pltpu.VMEM((et, tt), _BF16)],
        compiler_params=_params(("arbitrary",)),
        name="peer_dense",
    )(hb, u, vt, thr, e0, s1, e1z)


_PEER_TILE_KEYS = 8


def _peer_ffn(hb, w_q, layer, sub_keys, u, vt):
    half = sub_keys.shape[2]
    n_heads = w_q.shape[2] // (2 * half)
    q = _matmul(hb, w_q, layer, _BF16)
    ni = _PEER_TILE_KEYS
    thr, e0, s1, e1z = _peer_gate(q, sub_keys.astype(_BF16), n_heads, ni)
    return _peer_dense(hb, u, vt, layer, thr, e0, s1, e1z, n_heads, ni=ni)


def kernel(x, a_w_in, a_w_out, b_w_in, b_f_bias, b_w_out, peer_w_q, peer_sub_keys, peer_u, peer_v,
           ln_mix_g, ln_mix_b, ln_ffn_g, ln_ffn_b):
    b, s, d = x.shape
    depth = ln_mix_g.shape[0]
    alpha = (2 * depth) ** 0.25
    h = x.reshape(b * s, d)
    hb = None
    u_b, vt_b = _peer_expert_slabs(peer_u, peer_v, _PEER_TILE_KEYS, peer_sub_keys.shape[2])
    for i in range(depth):
        j = i // 2
        if i % 2 == 0:
            mix = _dilated_mixer(h, hb, a_w_in, a_w_out, j, b, s)
        else:
            mix = _forgetting_mixer(hb, b_w_in, b_f_bias[j], b_w_out, j, b, s)
        h, hb = _ln_residual(h, mix, ln_mix_g[i], ln_mix_b[i], alpha)
        ffn_t = _peer_ffn(hb, peer_w_q, i, peer_sub_keys[i], u_b, vt_b)
        h, hb = _ln_residual(h, ffn_t, ln_ffn_g[i], ln_ffn_b[i], alpha, transposed_mix=True)
    return h.reshape(b, s, d)
```

```python
import functools
import math

import jax
import jax.numpy as jnp
from jax import lax
from jax.experimental import pallas as pl
from jax.experimental.pallas import tpu as pltpu

_F32 = jnp.float32
_BF16 = jnp.bfloat16

_HEAD_DIM = 128
_DIL_WINDOWS = (128, 512, 2048)
_DIL_RATES = (1, 4, 16)
_PEER_TOPK = 16
_LN_EPS = 1e-5
_NEG = -1e30
_LANES = 128
_SUBLANES = 8
_LOG2E = math.log2(math.e)
_VMEM_LIMIT = 60 * 1024 * 1024


def _dot_nt(a, b):
    return lax.dot_general(a, b, (((1,), (1,)), ((), ())), preferred_element_type=_F32)


def _params(sem):
    return pltpu.CompilerParams(dimension_semantics=sem, vmem_limit_bytes=_VMEM_LIMIT)


def _mm_kernel(a_ref, w_ref, *rest, scaled):
    if scaled:
        s_ref, o_ref, wb_ref = rest
    else:
        o_ref, wb_ref = rest

    @pl.when(pl.program_id(1) == 0)
    def _():
        w = w_ref[...]
        if scaled:
            w = w * s_ref[...]
        wb_ref[...] = w.astype(wb_ref.dtype)

    o_ref[...] = jnp.dot(a_ref[...], wb_ref[...], preferred_element_type=_F32).astype(o_ref.dtype)


def _matmul(a, w, layer, out_dtype, tm=1024, tn=512, col_start=0, n_cols=None, col_scale=None):
    m, k = a.shape
    n = w.shape[2] if n_cols is None else n_cols
    tm = min(tm, m)
    tn = min(tn, n)
    while n % tn or col_start % tn:
        tn -= _LANES
    assert m % tm == 0 and tn > 0
    first = col_start // tn
    w_spec = pl.BlockSpec((None, k, tn), lambda j, i: (layer, 0, j + first))
    operands = [a, w]
    in_specs = [pl.BlockSpec((tm, k), lambda j, i: (i, 0)), w_spec]
    if col_scale is not None:
        operands.append(col_scale.reshape(1, -1).astype(_F32))
        in_specs.append(pl.BlockSpec((1, tn), lambda j, i: (0, j + first)))
    return pl.pallas_call(
        functools.partial(_mm_kernel, scaled=col_scale is not None),
        grid=(n // tn, m // tm),
        in_specs=in_specs,
        out_specs=pl.BlockSpec((tm, tn), lambda j, i: (i, j)),
        out_shape=jax.ShapeDtypeStruct((m, n), out_dtype),
        scratch_shapes=[pltpu.VMEM((k, tn), _BF16)],
        compiler_params=_params(("parallel", "arbitrary")),
        name="matmul",
    )(*operands)


def _ln_body(y, g_ref, b_ref, o_ref, ob_ref):
    mu = jnp.mean(y, axis=-1, keepdims=True)
    d = y - mu
    var = jnp.mean(d * d, axis=-1, keepdims=True)
    out = d * lax.rsqrt(var + _LN_EPS) * g_ref[...] + b_ref[...]
    o_ref[...] = out
    ob_ref[...] = out.astype(_BF16)


def _ln_kernel(h_ref, m_ref, g_ref, b_ref, o_ref, ob_ref, *, alpha):
    _ln_body(alpha * h_ref[...] + m_ref[...], g_ref, b_ref, o_ref, ob_ref)


def _ln_t_kernel(h_ref, mt_ref, g_ref, b_ref, o_ref, ob_ref, *, alpha):
    _ln_body(alpha * h_ref[...] + mt_ref[...].T, g_ref, b_ref, o_ref, ob_ref)


def _ln_residual(h, mix, g, b, alpha, transposed_mix=False, tr=256):
    t, d = h.shape
    tr = min(tr, t)
    assert t % tr == 0
    if transposed_mix:
        body = _ln_t_kernel
        mix_spec = pl.BlockSpec((d, tr), lambda i: (0, i))
    else:
        body = _ln_kernel
        mix_spec = pl.BlockSpec((tr, d), lambda i: (i, 0))
    row = pl.BlockSpec((tr, d), lambda i: (i, 0))
    vec = pl.BlockSpec((1, d), lambda i: (0, 0))
    return pl.pallas_call(
        functools.partial(body, alpha=alpha),
        grid=(t // tr,),
        in_specs=[row, mix_spec, vec, vec],
        out_specs=[row, row],
        out_shape=[jax.ShapeDtypeStruct((t, d), _F32), jax.ShapeDtypeStruct((t, d), _BF16)],
        compiler_params=_params(("parallel",)),
        name="ln_residual",
    )(h, mix, g.reshape(1, d), b.reshape(1, d))


def _dil_kernel(*refs, dil, n_heads, with_prev):
    if with_prev:
        q_ref, kc_ref, kp_ref, vc_ref, vp_ref, o_ref, lse_ref = refs
    else:
        q_ref, kc_ref, vc_ref, o_ref, lse_ref = refs
    n = _LANES
    hd = _HEAD_DIM
    nkeys = 2 * n if with_prev else n
    row = lax.broadcasted_iota(jnp.int32, (n, nkeys), 0)
    col = lax.broadcasted_iota(jnp.int32, (n, nkeys), 1)
    dist = (nkeys - n) + row - col
    valid = (dist >= 0) & (dist <= n)
    if with_prev:
        valid = valid & ((col >= n) | (pl.program_id(1) > 0))
    key_pos = (lax.broadcasted_iota(jnp.int32, (1, nkeys), 1) - (nkeys - n)).astype(_F32)
    query_pos = lax.broadcasted_iota(jnp.int32, (n, 1), 0).astype(_F32)
    lane = lax.broadcasted_iota(jnp.int32, (n, n), 1)
    ones = jnp.ones((nkeys, hd), _BF16)

    def keys_of(cur_ref, prev_ref, sl):
        if with_prev:
            return jnp.concatenate([prev_ref[0, :, sl], cur_ref[0, :, sl]], axis=0)
        return cur_ref[0, :, sl]

    lse_tile = jnp.zeros((n, n), _F32)
    group = 4
    for h0 in range(0, n_heads, group):
        heads = range(h0, min(h0 + group, n_heads))
        slopes = [float(dil) * _LOG2E * 2.0 ** (-8.0 * (h + 1) / n_heads) for h in heads]
        sls = [slice(h * hd, (h + 1) * hd) for h in heads]
        logits = []
        for sl, c in zip(sls, slopes):
            k2 = keys_of(kc_ref, kp_ref if with_prev else None, sl)
            logits.append(jnp.where(valid, _dot_nt(q_ref[0, :, sl], k2) + c * key_pos, _NEG))
        probs, maxes = [], []
        for s in logits:
            mx = jnp.max(s, axis=1, keepdims=True)
            maxes.append(mx)
            probs.append(jnp.exp2(s - mx).astype(_BF16))
        for h, sl, c, p, mx in zip(heads, sls, slopes, probs, maxes):
            v2 = keys_of(vc_ref, vp_ref if with_prev else None, sl)
            r = jnp.dot(p, jnp.concatenate([v2, ones], axis=1), preferred_element_type=_F32)
            den = r[:, hd:hd + 1]
            o_ref[0, :, sl] = r[:, :hd] / den
            lse = (mx + jnp.log2(den) - c * query_pos) * (1.0 / _LOG2E)
            lse_tile = jnp.where(lane == h, lse, lse_tile)
    lse_ref[0] = lse_tile


def _deinterleave_kernel(*refs, dils, n_in):
    x_refs, o_refs = refs[:n_in], refs[n_in:]
    s = x_refs[0].shape[1]
    for o_ref, dil in zip(o_refs, dils):
        l = s // dil
        for j, x_ref in enumerate(x_refs):
            cs = slice(j * _LANES, (j + 1) * _LANES)
            for r in range(dil):
                o_ref[0, r * l:(r + 1) * l, cs] = x_ref[0, pl.ds(r, l, stride=dil), :].astype(o_ref.dtype)


def _deinterleave(h3, dils, slabs=4):
    b, s, d = h3.shape
    slabs = min(slabs, d // _LANES)
    assert d % (slabs * _LANES) == 0

    def in_spec(j):
        return pl.BlockSpec((1, s, _LANES), lambda bi, ci: (bi, 0, ci * slabs + j))

    out_spec = pl.BlockSpec((1, s, slabs * _LANES), lambda bi, ci: (bi, 0, ci))
    return pl.pallas_call(
        functools.partial(_deinterleave_kernel, dils=dils, n_in=slabs),
        grid=(b, d // (slabs * _LANES)),
        in_specs=[in_spec(j) for j in range(slabs)],
        out_specs=[out_spec] * len(dils),
        out_shape=[jax.ShapeDtypeStruct((b, s, d), _BF16)] * len(dils),
        compiler_params=_params(("parallel", "parallel")),
        name="deinterleave",
    )(*[h3] * slabs)


def _dilated_group(proj, g, n_heads, b, s):
    w = n_heads * _HEAD_DIM
    dil = _DIL_RATES[g]
    n = _DIL_WINDOWS[g] // dil
    assert n == _LANES and n_heads <= _LANES
    l = s // dil
    nb = l // n
    assert l % n == 0
    view = proj.reshape(b * dil, l, 3 * w)

    def spec(which, prev):
        def index(p, i):
            return (p, jnp.maximum(i - 1, 0) if prev else i, which)
        return pl.BlockSpec((1, n, w), index)

    with_prev = nb > 1
    in_specs = [spec(0, False), spec(1, False), spec(1, True), spec(2, False), spec(2, True)]
    if not with_prev:
        in_specs = [spec(0, False), spec(1, False), spec(2, False)]
    return pl.pallas_call(
        functools.partial(_dil_kernel, dil=dil, n_heads=n_heads, with_prev=with_prev),
        grid=(b * dil, nb),
        in_specs=in_specs,
        out_specs=[pl.BlockSpec((1, n, w), lambda p, i: (p, i, 0)),
                   pl.BlockSpec((1, n, _LANES), lambda p, i: (p, i, 0))],
        out_shape=[jax.ShapeDtypeStruct((b * dil, l, w), _F32),
                   jax.ShapeDtypeStruct((b * dil, l, _LANES), _F32)],
        compiler_params=_params(("parallel", "parallel")),
        name=f"dilated_attention_g{g}",
    )(*[view] * len(in_specs))


def _merge_kernel(*refs, dils, n_heads):
    ng = len(dils)
    o_refs, l_refs, out_ref = refs[:ng], refs[ng:2 * ng], refs[2 * ng]
    scratch = iter(refs[2 * ng + 1:])
    tr = out_ref.shape[1]
    outs, lses = [], []
    for o_ref, l_ref, dil in zip(o_refs, l_refs, dils):
        if dil == 1:
            outs.append(o_ref.at[0, 0])
            lses.append(l_ref[0, 0])
            continue
        o_scr, l_scr = next(scratch), next(scratch)
        rows = tr // dil
        for r in range(dil):
            dst = pl.ds(r, rows, stride=dil)
            l_scr[dst, :] = l_ref[0, r]
            for h in range(n_heads):
                o_scr[h, dst, :] = o_ref[0, r, :, h * _HEAD_DIM:(h + 1) * _HEAD_DIM]
        outs.append(o_scr)
        lses.append(l_scr[...])
    for h in range(n_heads):
        sl = slice(h * _HEAD_DIM, (h + 1) * _HEAD_DIM)
        lh = [x[:, h:h + 1] for x in lses]
        mx = functools.reduce(jnp.maximum, lh)
        ws = [jnp.exp(x - mx) for x in lh]
        num = sum(wg * (o[:, sl] if dil == 1 else o[h]) for wg, o, dil in zip(ws, outs, dils))
        out_ref[0, :, sl] = (num / sum(ws)).astype(out_ref.dtype)


def _merge_groups(outs, lses, dils, n_heads, b, s, tr=256):
    w = outs[0].shape[-1]
    tr = min(tr, s)
    assert all(tr % (dil * _SUBLANES) == 0 for dil in dils)

    def spec(dil, width):
        return pl.BlockSpec((1, dil, tr // dil, width), lambda bi, i: (bi, 0, i, 0))

    view = lambda x, dil: x.reshape(b, dil, s // dil, x.shape[-1])
    scratch = []
    for dil in dils:
        if dil != 1:
            scratch += [pltpu.VMEM((n_heads, tr, _HEAD_DIM), _F32), pltpu.VMEM((tr, _LANES), _F32)]
    return pl.pallas_call(
        functools.partial(_merge_kernel, dils=dils, n_heads=n_heads),
        grid=(b, s // tr),
        in_specs=[spec(dil, w) for dil in dils] + [spec(dil, _LANES) for dil in dils],
        out_specs=pl.BlockSpec((1, tr, w), lambda bi, i: (bi, i, 0)),
        out_shape=jax.ShapeDtypeStruct((b, s, w), _BF16),
        scratch_shapes=scratch,
        compiler_params=_params(("parallel", "parallel")),
        name="dilated_merge",
    )(*[view(o, dil) for o, dil in zip(outs, dils)], *[view(x, dil) for x, dil in zip(lses, dils)])


def _dilated_mixer(h, hb, w_in, w_out, layer, b, s):
    d = h.shape[1]
    n_heads = d // _HEAD_DIM // 2
    w = n_heads * _HEAD_DIM
    wanted = tuple(dil for dil in _DIL_RATES if dil != 1 or hb is None)
    copies = dict(zip(wanted, _deinterleave(h.reshape(b, s, d), wanted)))
    if hb is None:
        hb = copies[1].reshape(b * s, d)
    col_scale = jnp.ones((len(_DIL_RATES), 3, w), _F32).at[:, 0].set(_LOG2E / math.sqrt(_HEAD_DIM))
    col_scale = col_scale.reshape(-1)
    outs, lses = [], []
    for g, dil in enumerate(_DIL_RATES):
        rows = hb if dil == 1 else copies[dil].reshape(b * s, d)
        proj = _matmul(rows, w_in, layer, _BF16, col_start=g * 3 * w, n_cols=3 * w, col_scale=col_scale)
        o, lse = _dilated_group(proj, g, n_heads, b, s)
        outs.append(o)
        lses.append(lse)
    merged = _merge_groups(outs, lses, _DIL_RATES, n_heads, b, s)
    return _matmul(merged.reshape(b * s, w), w_out, layer, _F32)


def _fox_gate_kernel(wf_ref, h_ref, bias_ref, c_ref, carry_ref):
    @pl.when(pl.program_id(1) == 0)
    def _():
        carry_ref[...] = jnp.zeros_like(carry_ref)

    z = _dot_nt(wf_ref[...], h_ref[0]) + bias_ref[...]
    log_f = jnp.minimum(z, 0.0) - jnp.log1p(jnp.exp(-jnp.abs(z)))
    log_f = log_f * _LOG2E
    nh, ts = log_f.shape
    lane = lax.broadcasted_iota(jnp.int32, (nh, _LANES), 1)
    carry = carry_ref[...]
    for c in range(ts // _LANES):
        x = log_f[:, c * _LANES:(c + 1) * _LANES]
        shift = 1
        while shift < _LANES:
            x = x + jnp.where(lane >= shift, pltpu.roll(x, shift, 1), 0.0)
            shift *= 2
        x = x + carry
        c_ref[0, :, c * _LANES:(c + 1) * _LANES] = x
        carry = jnp.broadcast_to(x[:, _LANES - 1:_LANES], (nh, _LANES))
    carry_ref[...] = carry


def _fox_gate(hb3, wf_t, bias, ts=512):
    b, s, d = hb3.shape
    nh = wf_t.shape[0]
    ts = min(ts, s)
    return pl.pallas_call(
        _fox_gate_kernel,
        grid=(b, s // ts),
        in_specs=[pl.BlockSpec((nh, d), lambda bi, si: (0, 0)),
                  pl.BlockSpec((1, ts, d), lambda bi, si: (bi, si, 0)),
                  pl.BlockSpec((nh, 1), lambda bi, si: (0, 0))],
        out_specs=pl.BlockSpec((1, nh, ts), lambda bi, si: (bi, 0, si)),
        out_shape=jax.ShapeDtypeStruct((b, nh, s), _F32),
        scratch_shapes=[pltpu.VMEM((nh, _LANES), _F32)],
        compiler_params=_params(("parallel", "arbitrary")),
        name="fox_gate_cumsum",
    )(wf_t, hb3, bias.reshape(nh, 1).astype(_F32))


def _fox_kernel(q_ref, k_ref, v_ref, c_ref, o_ref, m_ref, acc_ref, *, tq, heads):
    qi = pl.program_id(2)
    hd = _HEAD_DIM
    m_ref[...] = jnp.full_like(m_ref, _NEG)
    acc_ref[...] = jnp.zeros_like(acc_ref)
    ones = jnp.ones((tq, hd), _BF16)

    def step(kb, diagonal):
        k_start = pl.multiple_of(kb * tq, tq)
        sls = [slice(g * hd, (g + 1) * hd) for g in range(heads)]
        logits = []
        for g, sl in enumerate(sls):
            s = _dot_nt(q_ref[0, :, sl], k_ref[0, pl.ds(k_start, tq), sl]) - c_ref[0, g, :, pl.ds(k_start, tq)]
            if diagonal:
                row = lax.broadcasted_iota(jnp.int32, (tq, tq), 0)
                col = lax.broadcasted_iota(jnp.int32, (tq, tq), 1)
                s = jnp.where(col <= row, s, _NEG)
            logits.append(s)
        probs, alphas = [], []
        for g, s in enumerate(logits):
            m_old = m_ref[g]
            m_new = jnp.maximum(m_old, jnp.max(s, axis=1, keepdims=True))
            m_ref[g] = m_new
            alphas.append(jnp.exp2(m_old - m_new))
            probs.append(jnp.exp2(s - jnp.concatenate([m_new] * (tq // hd), axis=1)).astype(_BF16))
        for g, (sl, p, alpha) in enumerate(zip(sls, probs, alphas)):
            v1 = jnp.concatenate([v_ref[0, pl.ds(k_start, tq), sl], ones], axis=1)
            acc_ref[g] = (jnp.concatenate([alpha, alpha], axis=1) * acc_ref[g]
                          + jnp.dot(p, v1, preferred_element_type=_F32))

    lax.fori_loop(0, qi, lambda kb, carry: step(kb, False), None)
    step(qi, True)
    for g in range(heads):
        o_ref[0, :, g * hd:(g + 1) * hd] = (acc_ref[g, :, :hd] / acc_ref[g, :, hd:]).astype(o_ref.dtype)


def _fox_attention(qkv, c, n_heads, tq=512, heads=4):
    b, s, _ = qkv.shape
    tq = min(tq, s)
    heads = min(heads, n_heads)
    assert n_heads % heads == 0 and tq % _HEAD_DIM == 0
    ng = n_heads // heads
    gw = heads * _HEAD_DIM
    return pl.pallas_call(
        functools.partial(_fox_kernel, tq=tq, heads=heads),
        grid=(b, ng, s // tq),
        in_specs=[pl.BlockSpec((1, tq, gw), lambda bi, h, qi: (bi, qi, h)),
                  pl.BlockSpec((1, s, gw), lambda bi, h, qi: (bi, 0, ng + h)),
                  pl.BlockSpec((1, s, gw), lambda bi, h, qi: (bi, 0, 2 * ng + h)),
                  pl.BlockSpec((1, heads, 1, s), lambda bi, h, qi: (bi, h, 0, 0))],
        out_specs=pl.BlockSpec((1, tq, gw), lambda bi, h, qi: (bi, qi, h)),
        out_shape=jax.ShapeDtypeStruct((b, s, n_heads * _HEAD_DIM), _BF16),
        scratch_shapes=[pltpu.VMEM((heads, tq, _HEAD_DIM), _F32),
                        pltpu.VMEM((heads, tq, 2 * _HEAD_DIM), _F32)],
        compiler_params=_params(("parallel", "parallel", "parallel")),
        name="fox_attention",
    )(qkv, qkv, qkv, c.reshape(b, n_heads, 1, s))


def _forgetting_mixer(hb, w_in, f_bias, w_out, layer, b, s):
    d = hb.shape[1]
    n_heads = d // _HEAD_DIM
    width = n_heads * _HEAD_DIM
    q_scale = _LOG2E / math.sqrt(_HEAD_DIM)
    col_scale = jnp.ones((3, width), _F32).at[0].set(q_scale).reshape(-1)
    qkv = _matmul(hb, w_in, layer, _BF16, n_cols=3 * width, col_scale=col_scale).reshape(b, s, 3 * width)
    wf_t = w_in[layer, :, 3 * width:].T.astype(_BF16)
    c = _fox_gate(hb.reshape(b, s, d), wf_t, f_bias)
    o = _fox_attention(qkv, c, n_heads)
    return _matmul(o.reshape(b * s, width), w_out, layer, _F32)


def _sorting_network(n):
    comps = []

    def merge(lo, m, r):
        step = 2 * r
        if step < m:
            merge(lo, m, step)
            merge(lo + r, m, step)
            comps.extend((i, i + r) for i in range(lo + r, lo + m - r, step))
        else:
            comps.append((lo, lo + r))

    def sort(lo, m):
        if m > 1:
            sort(lo, m // 2)
            sort(lo + m // 2, m // 2)
            merge(lo, m, 1)

    sort(0, n)
    return comps


def _pop_columns(cols, count, singles=None):
    r, t = cols[0].shape
    row = lax.broadcasted_iota(jnp.int32, (r, t), 0)
    ninf = jnp.full((r, t), -jnp.inf, _F32)
    cols = list(cols)
    if singles is not None:
        r2 = singles.shape[0]
        row2 = lax.broadcasted_iota(jnp.int32, (r2, t), 0) + r
    vals = []
    for k in range(count):
        head = cols[0]
        m = jnp.max(head, axis=0, keepdims=True)
        if singles is not None:
            m = jnp.maximum(m, jnp.max(singles, axis=0, keepdims=True))
        vals.append(m)
        left = count - 1 - k
        if left == 0:
            break
        none = r + (0 if singles is None else singles.shape[0])
        first = jnp.min(jnp.where(head == m, row, none), axis=0, keepdims=True)
        if singles is not None:
            first = jnp.minimum(first, jnp.min(jnp.where(singles == m, row2, none),
                                               axis=0, keepdims=True))
            singles = jnp.where(row2 == first, -jnp.inf, singles)
        hit = row == first
        depth = min(left, len(cols))
        cols = [jnp.where(hit, cols[d + 1] if d + 1 < len(cols) else ninf, cols[d])
                for d in range(depth)]
    return vals


def _top_values(scores, count):
    n = scores.shape[0]
    stacks = [scores[v * _SUBLANES:(v + 1) * _SUBLANES] for v in range(n // _SUBLANES)]
    for i, j in _sorting_network(len(stacks)):
        stacks[i], stacks[j] = jnp.maximum(stacks[i], stacks[j]), jnp.minimum(stacks[i], stacks[j])
    return _pop_columns(stacks, count)


def _stack_rows(rows_1t, n_rows):
    t = rows_1t[0].shape[1]
    row = lax.broadcasted_iota(jnp.int32, (n_rows, t), 0)
    out = jnp.full((n_rows, t), -jnp.inf, _F32)
    for k, v in enumerate(rows_1t):
        out = jnp.where(row == k, v, out)
    return out


def _store_tile_rows(ref, h, x, ni):
    per = _SUBLANES // ni
    for g in range(x.shape[0] // _SUBLANES):
        grp = x[g * _SUBLANES:(g + 1) * _SUBLANES]
        for j in range(per):
            ref[h, g * per + j] = grp if j == 0 else pltpu.roll(grp, _SUBLANES - j * ni, 0)


def _peer_gate_kernel(q_ref, keys_ref, thr_ref, e0_ref, s1_ref, e1z_ref, *, n_heads, topk, ni):
    half = keys_ref.shape[2]
    tt = q_ref.shape[0]
    k0 = keys_ref[0]
    k1 = keys_ref[1]
    nkeep = topk + 1
    rows = _SUBLANES * (-(-nkeep // _SUBLANES))
    row = lax.broadcasted_iota(jnp.int32, (rows, tt), 0)
    for h in range(n_heads):
        s0 = _dot_nt(k0, q_ref[:, (2 * h) * half:(2 * h + 1) * half])
        s1 = _dot_nt(k1, q_ref[:, (2 * h + 1) * half:(2 * h + 2) * half])
        a_vals = _top_values(s0, nkeep)
        b_vals = _top_values(s1, nkeep)
        a_rows = _stack_rows(a_vals, rows)
        stacks = [jnp.where(row[:_SUBLANES] < nkeep // (j + 1), a_rows[:_SUBLANES] + b_vals[j],
                            -jnp.inf) for j in range(nkeep)]
        assert nkeep // 2 <= _SUBLANES
        singles = jnp.where(row[_SUBLANES:] < nkeep, a_rows[_SUBLANES:] + b_vals[0], -jnp.inf)
        top = _pop_columns(stacks, nkeep, singles)
        z = sum(jnp.exp(tv - top[0]) for tv in top[:topk])
        thr = 0.5 * (top[topk - 1] + top[topk])
        _store_tile_rows(thr_ref, h, thr - s0, ni)
        _store_tile_rows(e0_ref, h, jnp.exp(s0 - a_vals[0]), ni)
        s1_ref[h] = s1
        e1z_ref[h] = jnp.exp(s1 - b_vals[0]) / z


def _peer_gate(q, sub_keys, n_heads, ni, tt=128):
    t = q.shape[0]
    nk, half = sub_keys.shape[1:]
    tt = min(tt, t)
    assert _SUBLANES % ni == 0 and nk % _SUBLANES == 0
    n_tiles = nk // ni
    row_spec = pl.BlockSpec((n_heads, n_tiles, _SUBLANES, tt), lambda i: (0, 0, 0, i))
    row_shape = jax.ShapeDtypeStruct((n_heads, n_tiles, _SUBLANES, t), _F32)
    col_spec = pl.BlockSpec((n_heads, nk, tt), lambda i: (0, 0, i))
    col_shape = jax.ShapeDtypeStruct((n_heads, nk, t), _F32)
    return pl.pallas_call(
        functools.partial(_peer_gate_kernel, n_heads=n_heads, topk=_PEER_TOPK, ni=ni),
        grid=(t // tt,),
        in_specs=[pl.BlockSpec((tt, q.shape[1]), lambda i: (i, 0)),
                  pl.BlockSpec((2, nk, half), lambda i: (0, 0, 0))],
        out_specs=[row_spec, row_spec, col_spec, col_spec],
        out_shape=[row_shape, row_shape, col_shape, col_shape],
        compiler_params=_params(("parallel",)),
        name="peer_gate",
    )(q, sub_keys)


def _gelu(x):
    return 0.5 * x * (1.0 + lax.erf(x * (1.0 / math.sqrt(2.0))))


def _peer_kernel(hb_ref, u_ref, vt_ref, thr_ref, e0_ref, s1_ref, e1z_ref, o_ref, a_ref, w_ref,
                 *, ni, n_heads, n_tiles, n_work):
    k = pl.program_id(0)

    @pl.when(k == 0)
    def _():
        a_ref[...] = jnp.zeros_like(a_ref)

    @pl.when(jnp.clip(k - 1, 0, n_work - 1) % n_tiles == 0)
    def _():
        o_ref[...] = jnp.zeros_like(o_ref)

    nk = s1_ref.shape[1]
    tt = hb_ref.shape[0]
    gate_rows = 16
    groups = 4

    def gate_block(ii):
        for tc in range(tt // _LANES):
            cs = slice(tc * _LANES, (tc + 1) * _LANES)
            for j0 in range(0, nk, gate_rows):
                js = slice(j0, j0 + gate_rows)
                gate = jnp.zeros((gate_rows, _LANES), _F32)
                for h in range(n_heads):
                    thr = thr_ref[h, ii:ii + 1, cs]
                    e0 = e0_ref[h, ii:ii + 1, cs]
                    gate = gate + jnp.where(s1_ref[h, js, cs] >= thr, e1z_ref[h, js, cs] * e0, 0.0)
                rs = slice(ii * nk + j0, ii * nk + j0 + gate_rows)
                w_ref[rs, cs] = (_gelu(a_ref[rs, cs]) * gate).astype(_BF16)

    per = ni // groups
    for g in range(groups):
        for ii in range(g * per, (g + 1) * per):
            gate_block(ii)
        es = slice(g * per * nk, (g + 1) * per * nk)
        o_ref[...] += jnp.dot(vt_ref[:, es], w_ref[es, :], preferred_element_type=_F32)
    a_ref[...] = _dot_nt(u_ref[...], hb_ref[...])


def _peer_expert_slabs(u, v, ni, nk):
    layers, _, d = v.shape
    et = ni * nk
    vt = v.astype(_BF16).reshape(layers, -1, et, d).transpose(0, 1, 3, 2)
    return u.astype(_BF16), vt


def _peer_dense(hb, u, vt, layer, thr, e0, s1, e1z, n_heads, tt=512, ni=4):
    t, d = hb.shape
    nk = s1.shape[1]
    tt = min(tt, t)
    et = ni * nk
    n_tiles = nk // ni
    n_work = (t // tt) * n_tiles

    def stage(lag):
        return lambda k: jnp.clip(k - lag, 0, n_work - 1)

    once = pl.Buffered(1)
    gate_spec = pl.BlockSpec((n_heads, nk, tt), lambda k: (0, 0, stage(1)(k) // n_tiles),
                             pipeline_mode=once)
    row_spec = pl.BlockSpec((n_heads, None, _SUBLANES, tt),
                            lambda k: (0, stage(1)(k) % n_tiles, 0, stage(1)(k) // n_tiles))
    return pl.pallas_call(
        functools.partial(_peer_kernel, ni=ni, n_heads=n_heads, n_tiles=n_tiles, n_work=n_work),
        grid=(n_work + 1,),
        in_specs=[pl.BlockSpec((tt, d), lambda k: (stage(0)(k) // n_tiles, 0)),
                  pl.BlockSpec((None, et, d), lambda k: (layer, stage(0)(k) % n_tiles, 0)),
                  pl.BlockSpec((None, None, d, et), lambda k: (layer, stage(1)(k) % n_tiles, 0, 0)),
                  row_spec, row_spec, gate_spec, gate_spec],
        out_specs=pl.BlockSpec((d, tt), lambda k: (0, stage(1)(k) // n_tiles), pipeline_mode=once),
        out_shape=jax.ShapeDtypeStruct((d, t), _F32),
        scratch_shapes=[pltpu.VMEM((et, tt), _F32), pltpu.VMEM((et, tt), _BF16)],
        compiler_params=_params(("arbitrary",)),
        name="peer_dense",
    )(hb, u, vt, thr, e0, s1, e1z)


_PEER_TILE_KEYS = 8


def _peer_ffn(hb, w_q, layer, sub_keys, u, vt):
    half = sub_keys.shape[2]
    n_heads = w_q.shape[2] // (2 * half)
    q = _matmul(hb, w_q, layer, _BF16)
    ni = _PEER_TILE_KEYS
    thr, e0, s1, e1z = _peer_gate(q, sub_keys.astype(_BF16), n_heads, ni)
    return _peer_dense(hb, u, vt, layer, thr, e0, s1, e1z, n_heads, ni=ni)


def kernel(x, a_w_in, a_w_out, b_w_in, b_f_bias, b_w_out, peer_w_q, peer_sub_keys, peer_u, peer_v,
           ln_mix_g, ln_mix_b, ln_ffn_g, ln_ffn_b):
    b, s, d = x.shape
    depth = ln_mix_g.shape[0]
    alpha = (2 * depth) ** 0.25
    h = x.reshape(b * s, d)
    hb = None
    u_b, vt_b = _peer_expert_slabs(peer_u, peer_v, _PEER_TILE_KEYS, peer_sub_keys.shape[2])
    for i in range(depth):
        j = i // 2
        if i % 2 == 0:
            mix = _dilated_mixer(h, hb, a_w_in, a_w_out, j, b, s)
        else:
            mix = _forgetting_mixer(hb, b_w_in, b_f_bias[j], b_w_out, j, b, s)
        h, hb = _ln_residual(h, mix, ln_mix_g[i], ln_mix_b[i], alpha)
        ffn_t = _peer_ffn(hb, peer_w_q, i, peer_sub_keys[i], u_b, vt_b)
        h, hb = _ln_residual(h, ffn_t, ln_ffn_g[i], ln_ffn_b[i], alpha, transposed_mix=True)
    return h.reshape(b, s, d)
```
